```python
import math
import jax, jax.numpy as jnp
from jax import lax
import numpy as np

D_MODEL = 1024
BATCH = 8
SEQ = 2048
DEPTH = 4
DEC_BATCH = 32
DEC_SEQ = 4
PAST_LEN = 8192
PAGE_SIZE = 128

N_META = 16
N_EVEN = (DEPTH + 1) // 2
N_ODD = DEPTH // 2
A_HEADS = 8
A_HEAD_DIM = D_MODEL // 16
A_KV_HEADS = 2
A_GROUP = A_HEADS // A_KV_HEADS
A_WIDTH = A_HEADS * A_HEAD_DIM
IDX_HEADS = 8
IDX_DIM = 64
TOPK_MAX = 256
Q_BLOCK = 128
POOL_WINDOWS = (2, 4, 8, 16)
POOL_GROUPS = 4
POOL_WIDTH = D_MODEL // 2
POOL_GROUP_DIM = POOL_WIDTH // POOL_GROUPS
POOL_BUF = 15
EVEN_SIZES = (A_WIDTH, A_KV_HEADS * A_HEAD_DIM, A_KV_HEADS * A_HEAD_DIM, IDX_HEADS * IDX_DIM, IDX_DIM, IDX_HEADS, POOL_WIDTH)
EVEN_IN = A_WIDTH + 2 * A_KV_HEADS * A_HEAD_DIM + IDX_HEADS * IDX_DIM + IDX_DIM + IDX_HEADS + POOL_WIDTH
EVEN_MIX = A_WIDTH + POOL_WIDTH
GLA_HEADS = 4
GLA_DK = D_MODEL // 2 // GLA_HEADS
GLA_DV = D_MODEL // GLA_HEADS
GLA_GATE_RANK = 16
GLA_GATE_NORM = 16.0
GLA_CHUNK = 64
ODD_SIZES = (GLA_HEADS * GLA_DK, GLA_HEADS * GLA_DK, GLA_HEADS * GLA_DV, GLA_HEADS * GLA_DV, GLA_GATE_RANK)
ODD_IN = 2 * GLA_HEADS * GLA_DK + 2 * GLA_HEADS * GLA_DV + GLA_GATE_RANK
ODD_MIX = GLA_HEADS * GLA_DV
D_FF = ((8 * D_MODEL + 3 * 256 - 1) // (3 * 256)) * 256
NORM_EPS = 1e-6

kernel_name = 'dsa_pool_gla_meta_hybrid_step'


def rmsnorm(x, g):
    xf = x.astype(jnp.float32)
    y = xf * lax.rsqrt(jnp.mean(xf * xf, axis=-1, keepdims=True) + NORM_EPS)
    return (y * g.astype(jnp.float32)).astype(x.dtype)


def split_cols(p, sizes):
    out, off = [], 0
    for s in sizes:
        out.append(p[..., off:off + s])
        off += s
    return out


def swiglu(h, wg, wu, wd):
    return (jax.nn.silu(h @ wg) * (h @ wu)) @ wd


def dsa_core(q, qi, wi, qpos, kidx, gather_kv, topk):
    B, T = q.shape[:2]
    Lk = kidx.shape[1]
    s = jnp.einsum('bthd,bsd->bths', qi.astype(jnp.float32), kidx.astype(jnp.float32))
    score = jnp.einsum('bths,bth->bts', jax.nn.relu(s), wi.astype(jnp.float32))
    vis = jnp.arange(Lk)[None, :] <= qpos[:, None]
    score = jnp.where(vis[None], score, -jnp.inf)
    _, idx = lax.top_k(score, topk)
    valid = idx <= qpos[None, :, None]
    k_sel, v_sel = gather_kv(idx)
    qg = q.reshape(B, T, A_KV_HEADS, A_GROUP, A_HEAD_DIM)
    logits = jnp.einsum('btkgd,btjkd->btkgj', qg, k_sel).astype(jnp.float32) * (A_HEAD_DIM ** -0.5)
    logits = jnp.where(valid[:, :, None, None, :], logits, -jnp.inf)
    p = jax.nn.softmax(logits, axis=-1).astype(v_sel.dtype)
    o = jnp.einsum('btkgj,btjkd->btkgd', p, v_sel)
    return o.reshape(B, T, A_WIDTH)


def rows_gather(src, idx):
    return jax.vmap(lambda sb, ib: sb[ib])(src, idx)


def dsa_prompt(q, k, v, qi, ki, wi):
    B, L = q.shape[:2]
    topk = min(TOPK_MAX, L // 4)
    nb = -(-L // Q_BLOCK)
    Lpad = nb * Q_BLOCK
    pad = lambda t: jnp.pad(t, [(0, 0), (0, Lpad - L)] + [(0, 0)] * (t.ndim - 2))
    qp, qip, wip = pad(q), pad(qi), pad(wi)
    gather = lambda idx: (rows_gather(k, idx), rows_gather(v, idx))

    def block(start):
        sl = lambda t: lax.dynamic_slice_in_dim(t, start, Q_BLOCK, axis=1)
        qpos = start + jnp.arange(Q_BLOCK)
        return dsa_core(sl(qp), sl(qip), sl(wip), qpos, ki, gather, topk)

    out = lax.map(block, jnp.arange(nb) * Q_BLOCK)
    return jnp.moveaxis(out, 0, 1).reshape(B, Lpad, A_WIDTH)[:, :L]


def dsa_sample(q, k_new, v_new, qi, ki_new, wi, pool_k, pool_v, pool_ki, page_table):
    DB, T = q.shape[:2]
    past = page_table.shape[1] * PAGE_SIZE
    topk = min(TOPK_MAX, (past + T) // 4)
    ki_past = pool_ki[page_table].reshape(DB, past, IDX_DIM)
    ki_all = jnp.concatenate([ki_past.astype(ki_new.dtype), ki_new], axis=1)
    kflat = pool_k.reshape(-1, A_KV_HEADS, A_HEAD_DIM)
    vflat = pool_v.reshape(-1, A_KV_HEADS, A_HEAD_DIM)

    def gather(idx):
        ip = jnp.minimum(idx, past - 1)
        page = jax.vmap(lambda pt, i: pt[i])(page_table, ip // PAGE_SIZE)
        phys = page * PAGE_SIZE + ip % PAGE_SIZE
        inew = jnp.clip(idx - past, 0, T - 1)
        is_new = (idx >= past)[..., None, None]
        k_sel = jnp.where(is_new, rows_gather(k_new, inew), kflat[phys].astype(k_new.dtype))
        v_sel = jnp.where(is_new, rows_gather(v_new, inew), vflat[phys].astype(v_new.dtype))
        return k_sel, v_sel

    qpos = past + jnp.arange(T)
    return dsa_core(q, qi, wi, qpos, ki_all, gather, topk)


def multiscale_pool(u, prefix, pos, w_grp, scale):
    B, T, C = u.shape
    ext = jnp.concatenate([prefix, u], axis=1)
    extf = ext.astype(jnp.float32)
    cs = jnp.pad(jnp.cumsum(extf, axis=1), ((0, 0), (1, 0), (0, 0)))
    end = cs[:, POOL_BUF + 1:]
    outs = []
    for g, w in enumerate(POOL_WINDOWS):
        sl = slice(g * POOL_GROUP_DIM, (g + 1) * POOL_GROUP_DIM)
        start = cs[:, POOL_BUF + 1 - w:POOL_BUF + 1 - w + T, sl]
        cnt = jnp.minimum(w, pos + 1).astype(jnp.float32)[None, :, None]
        outs.append((end[..., sl] - start) / cnt - extf[:, POOL_BUF:, sl])
    d = jnp.stack(outs, axis=2).astype(u.dtype)
    y = jnp.einsum('btgc,gce->btge', d, w_grp).reshape(B, T, C)
    return y * scale, ext[:, -POOL_BUF:]


def even_project(h, w_in):
    B, T, _ = h.shape
    q, k, v, qi, ki, wi, u = split_cols(h @ w_in, EVEN_SIZES)
    return (q.reshape(B, T, A_HEADS, A_HEAD_DIM), k.reshape(B, T, A_KV_HEADS, A_HEAD_DIM),
            v.reshape(B, T, A_KV_HEADS, A_HEAD_DIM), qi.reshape(B, T, IDX_HEADS, IDX_DIM), ki, wi, u)


def even_prompt(h, w_in, w_out, pw, ps):
    B, L, _ = h.shape
    q, k, v, qi, ki, wi, u = even_project(h, w_in)
    oa = dsa_prompt(q, k, v, qi, ki, wi)
    prefix = jnp.zeros((B, POOL_BUF, POOL_WIDTH), u.dtype)
    ob, buf = multiscale_pool(u, prefix, jnp.arange(L), pw, ps)
    y = jnp.concatenate([oa.astype(h.dtype), ob], axis=-1) @ w_out
    return y, k, v, ki, buf


def even_sample(h, pool_k, pool_v, pool_ki, buf0, page_table, w_in, w_out, pw, ps):
    T = h.shape[1]
    q, k, v, qi, ki, wi, u = even_project(h, w_in)
    past = page_table.shape[1] * PAGE_SIZE
    oa = dsa_sample(q, k, v, qi, ki, wi, pool_k, pool_v, pool_ki, page_table)
    ob, buf = multiscale_pool(u, buf0.astype(u.dtype), past + jnp.arange(T), pw, ps)
    y = jnp.concatenate([oa.astype(h.dtype), ob], axis=-1) @ w_out
    return y, k, v, ki, buf


def gla_scan(q, k, v, loga, S, chunk):
    B, T, H, DK = q.shape
    DV = v.shape[-1]
    n = T // chunk
    to_chunks = lambda t: jnp.moveaxis(t.astype(jnp.float32).reshape(B, n, chunk, H, t.shape[-1]), 1, 0)
    causal = jnp.tril(jnp.ones((chunk, chunk), bool))[None, :, :, None, None]

    def step(S, xs):
        qc, kc, vc, gc = xs
        b = jnp.cumsum(gc, axis=1)
        o_inter = jnp.einsum('bthd,bhde->bthe', qc * jnp.exp(b), S)
        diff = b[:, :, None] - b[:, None, :]
        decay = jnp.exp(jnp.where(causal, diff, -jnp.inf))
        att = jnp.einsum('bthd,bshd,btshd->bths', qc, kc, decay)
        o_intra = jnp.einsum('bths,bshe->bthe', att, vc)
        bl = b[:, -1]
        S = jnp.exp(bl)[..., None] * S + jnp.einsum('bshd,bshe->bhde', kc * jnp.exp(bl[:, None] - b), vc)
        return S, o_inter + o_intra

    S, o = lax.scan(step, S.astype(jnp.float32), (to_chunks(q), to_chunks(k), to_chunks(v), to_chunks(loga)))
    return jnp.moveaxis(o, 0, 1).reshape(B, T, H, DV), S


def gla_project(h, w_in, wgu, bg):
    B, T, _ = h.shape
    q, k, v, g, gd = split_cols(h @ w_in, ODD_SIZES)
    q = q.reshape(B, T, GLA_HEADS, GLA_DK) * (GLA_DK ** -0.5)
    k = k.reshape(B, T, GLA_HEADS, GLA_DK)
    v = v.reshape(B, T, GLA_HEADS, GLA_DV)
    loga = jax.nn.log_sigmoid((gd @ wgu + bg).astype(jnp.float32)) / GLA_GATE_NORM
    return q, k, v, g, loga.reshape(B, T, GLA_HEADS, GLA_DK)


def gla_output(o, g, gn, w_out, dtype):
    B, T = o.shape[:2]
    on = rmsnorm(o, gn).astype(dtype)
    on = on * jax.nn.silu(g.reshape(B, T, GLA_HEADS, GLA_DV))
    return on.reshape(B, T, ODD_MIX) @ w_out


def odd_prompt(h, w_in, wgu, bg, gn, w_out):
    B, L, _ = h.shape
    q, k, v, g, la = gla_project(h, w_in, wgu, bg)
    S0 = jnp.zeros((B, GLA_HEADS, GLA_DK, GLA_DV), jnp.float32)
    head = lambda t: t[:, :N_META]
    tail = lambda t: t[:, N_META:]
    o_m, S = gla_scan(head(q), head(k), head(v), head(la), S0, N_META)
    o_r, S = gla_scan(tail(q), tail(k), tail(v), tail(la), S, math.gcd(GLA_CHUNK, L - N_META))
    o = jnp.concatenate([o_m, o_r], axis=1)
    return gla_output(o, g, gn, w_out, h.dtype), S.astype(h.dtype)


def odd_sample(h, S0, w_in, wgu, bg, gn, w_out):
    T = h.shape[1]
    q, k, v, g, la = gla_project(h, w_in, wgu, bg)
    o, S = gla_scan(q, k, v, la, S0, math.gcd(GLA_CHUNK, T))
    return gla_output(o, g, gn, w_out, h.dtype), S.astype(h.dtype)


def setup_inputs(seed: int = 0) -> dict:
    key = jax.random.key(seed)
    ks = jax.random.split(key, 26)
    nrm = lambda k, shape, s: jax.random.normal(k, shape, jnp.float32) * s
    n_pages = PAST_LEN // PAGE_SIZE
    n_used = DEC_BATCH * n_pages
    n_pool = n_used + max(1, n_used // 4)
    page_table = jax.random.permutation(ks[0], n_pool)[:n_used].reshape(DEC_BATCH, n_pages).astype(jnp.int32)
    out_s = (2 * DEPTH) ** -0.5
    return {
        'x_prompt': nrm(ks[1], (BATCH, SEQ, D_MODEL), 1.0),
        'x_sample': nrm(ks[2], (DEC_BATCH, DEC_SEQ, D_MODEL), 1.0),
        'cache_k': nrm(ks[3], (N_EVEN, n_pool, PAGE_SIZE, A_KV_HEADS, A_HEAD_DIM), 1.0),
        'cache_v': nrm(ks[4], (N_EVEN, n_pool, PAGE_SIZE, A_KV_HEADS, A_HEAD_DIM), 1.0),
        'cache_kidx': nrm(ks[5], (N_EVEN, n_pool, PAGE_SIZE, IDX_DIM), 1.0),
        'state_pool': nrm(ks[6], (N_EVEN, DEC_BATCH, POOL_BUF, POOL_WIDTH), 1.0),
        'state_gla': nrm(ks[7], (N_ODD, DEC_BATCH, GLA_HEADS, GLA_DK, GLA_DV), 1.0),
        'page_table': page_table,
        'meta_tokens': nrm(ks[8], (N_META, D_MODEL), 1.0),
        'norm_mix': 1.0 + nrm(ks[9], (DEPTH, D_MODEL), 0.05),
        'norm_ffn': 1.0 + nrm(ks[10], (DEPTH, D_MODEL), 0.05),
        'norm_final': 1.0 + nrm(ks[11], (D_MODEL,), 0.05),
        'w_in_even': nrm(ks[12], (N_EVEN, D_MODEL, EVEN_IN), D_MODEL ** -0.5),
        'w_out_even': nrm(ks[13], (N_EVEN, EVEN_MIX, D_MODEL), EVEN_MIX ** -0.5 * out_s),
        'pool_w': nrm(ks[14], (N_EVEN, POOL_GROUPS, POOL_GROUP_DIM, POOL_GROUP_DIM), POOL_GROUP_DIM ** -0.5),
        'pool_scale': 1.0 + nrm(ks[15], (N_EVEN, POOL_WIDTH), 0.1),
        'w_in_odd': nrm(ks[16], (N_ODD, D_MODEL, ODD_IN), D_MODEL ** -0.5),
        'gla_w_gate_up': nrm(ks[17], (N_ODD, GLA_GATE_RANK, GLA_HEADS * GLA_DK), GLA_GATE_RANK ** -0.5),
        'gla_b_gate': nrm(ks[18], (N_ODD, GLA_HEADS * GLA_DK), 0.1),
        'gla_norm': 1.0 + nrm(ks[19], (N_ODD, GLA_DV), 0.05),
        'w_out_odd': nrm(ks[20], (N_ODD, ODD_MIX, D_MODEL), ODD_MIX ** -0.5 * out_s),
        'ffn_w_gate': nrm(ks[21], (DEPTH, D_MODEL, D_FF), D_MODEL ** -0.5),
        'ffn_w_up': nrm(ks[22], (DEPTH, D_MODEL, D_FF), D_MODEL ** -0.5),
        'ffn_w_down': nrm(ks[23], (DEPTH, D_FF, D_MODEL), D_FF ** -0.5 * out_s),
    }


def reference(x_prompt, x_sample, cache_k, cache_v, cache_kidx, state_pool, state_gla, page_table,
              meta_tokens, norm_mix, norm_ffn, norm_final, w_in_even, w_out_even, pool_w, pool_scale,
              w_in_odd, gla_w_gate_up, gla_b_gate, gla_norm, w_out_odd, ffn_w_gate, ffn_w_up, ffn_w_down):
    B = x_prompt.shape[0]
    meta = jnp.broadcast_to(meta_tokens.astype(x_prompt.dtype)[None], (B, N_META, D_MODEL))
    xp = jnp.concatenate([meta, x_prompt], axis=1)
    xs = x_sample
    kp_l, vp_l, kip_l, bp_l, ks_l, vs_l, kis_l, bs_l, Sp_l, Ss_l = [], [], [], [], [], [], [], [], [], []
    for l in range(DEPTH):
        hp = rmsnorm(xp, norm_mix[l])
        hs = rmsnorm(xs, norm_mix[l])
        if l % 2 == 0:
            e = l // 2
            yp, kp, vp, kip, bp = even_prompt(hp, w_in_even[e], w_out_even[e], pool_w[e], pool_scale[e])
            ys, kss, vss, kis, bs = even_sample(hs, cache_k[e], cache_v[e], cache_kidx[e], state_pool[e], page_table,
                                                w_in_even[e], w_out_even[e], pool_w[e], pool_scale[e])
            kp_l.append(kp); vp_l.append(vp); kip_l.append(kip); bp_l.append(bp)
            ks_l.append(kss); vs_l.append(vss); kis_l.append(kis); bs_l.append(bs)
        else:
            o = l // 2
            yp, Sp = odd_prompt(hp, w_in_odd[o], gla_w_gate_up[o], gla_b_gate[o], gla_norm[o], w_out_odd[o])
            ys, Ss = odd_sample(hs, state_gla[o], w_in_odd[o], gla_w_gate_up[o], gla_b_gate[o], gla_norm[o], w_out_odd[o])
            Sp_l.append(Sp); Ss_l.append(Ss)
        xp = xp + yp
        xs = xs + ys
        xp = xp + swiglu(rmsnorm(xp, norm_ffn[l]), ffn_w_gate[l], ffn_w_up[l], ffn_w_down[l])
        xs = xs + swiglu(rmsnorm(xs, norm_ffn[l]), ffn_w_gate[l], ffn_w_up[l], ffn_w_down[l])
    y_prompt = rmsnorm(xp, norm_final)[:, N_META:]
    y_sample = rmsnorm(xs, norm_final)
    return (y_prompt, y_sample,
            jnp.stack(kp_l), jnp.stack(vp_l), jnp.stack(kip_l),
            jnp.stack(ks_l), jnp.stack(vs_l), jnp.stack(kis_l),
            jnp.stack(bp_l), jnp.stack(bs_l),
            jnp.stack(Sp_l), jnp.stack(Ss_l))
```

```python
import functools

import jax
import jax.numpy as jnp
from jax import lax
from jax.experimental import pallas as pl
from jax.experimental.pallas import tpu as pltpu

F32 = jnp.float32
BF16 = jnp.bfloat16
I32 = jnp.int32

NORM_EPS = 1e-6
N_META = 16
A_HEADS = 8
A_KV_HEADS = 2
A_GROUP = A_HEADS // A_KV_HEADS
A_HEAD_DIM = 64
IDX_HEADS = 8
IDX_DIM = 64
TOPK_MAX = 256
PAGE_SIZE = 128
POOL_WINDOWS = (2, 4, 8, 16)
POOL_GROUP_DIM = 128
POOL_BUF = 15
GLA_HEADS = 4
GLA_DK = 128
GLA_DV = 256
GLA_GATE_RANK = 16
GLA_GATE_NORM = 16.0

LANES = 128
SUBLANES = 8
BF16_ROWS = 16
VMEM_LIMIT_BYTES = 56 * 1024 * 1024

INT_MIN = -(2 ** 31)
MASK_BIAS = -1e30
HIGHEST = lax.Precision.HIGHEST


def _round_up(n, m):
    return (n + m - 1) // m * m


def _cparams(*sem):
    return pltpu.CompilerParams(dimension_semantics=sem, vmem_limit_bytes=VMEM_LIMIT_BYTES)


def _rms(x, g):
    ms = jnp.mean(x * x, axis=-1, keepdims=True)
    return (x * lax.rsqrt(ms + NORM_EPS)) * g


def _dot_nt(a, b):
    return lax.dot_general(a, b, (((1,), (1,)), ((), ())), preferred_element_type=F32)


def _dot_tn(a, b):
    return lax.dot_general(a, b, (((0,), (0,)), ((), ())), preferred_element_type=F32)


def _norm_matmul_kernel(x_ref, g_ref, w_ref, *o_refs, widths):
    h = _rms(x_ref[...], g_ref[...]).astype(BF16)
    off = 0
    for o_ref, wd in zip(o_refs, widths):
        o_ref[...] = jnp.dot(h, w_ref[:, off:off + wd], preferred_element_type=F32)
        off += wd


def norm_matmul(x, g, w, widths, tm):
    n, d = x.shape
    assert n % tm == 0 and sum(widths) == w.shape[1]
    return pl.pallas_call(
        functools.partial(_norm_matmul_kernel, widths=widths),
        grid=(n // tm,),
        in_specs=[pl.BlockSpec((tm, d), lambda i: (i, 0)),
                  pl.BlockSpec((1, d), lambda i: (0, 0)),
                  pl.BlockSpec((d, w.shape[1]), lambda i: (0, 0))],
        out_specs=[pl.BlockSpec((tm, wd), lambda i: (i, 0)) for wd in widths],
        out_shape=[jax.ShapeDtypeStruct((n, wd), F32) for wd in widths],
        compiler_params=_cparams("parallel"),
        name="norm_matmul",
    )(x, g.reshape(1, d), w)


def _mix_ffn_kernel(*refs, n_mix):
    x_ref = refs[0]
    mix = refs[1:1 + 2 * n_mix]
    g_ref, wg_ref, wu_ref, wd_ref, o_ref, h_s = refs[1 + 2 * n_mix:]
    j = pl.program_id(1)

    @pl.when(j == 0)
    def _():
        xm = x_ref[...]
        for m in range(n_mix):
            a_ref, w_ref = mix[2 * m], mix[2 * m + 1]
            xm = xm + jnp.dot(a_ref[...].astype(BF16), w_ref[...], preferred_element_type=F32)
        o_ref[...] = xm
        h_s[...] = _rms(xm, g_ref[...]).astype(BF16)

    h = h_s[...]
    gate = jnp.dot(h, wg_ref[...], preferred_element_type=F32)
    up = jnp.dot(h, wu_ref[...], preferred_element_type=F32)
    act = (gate * jax.nn.sigmoid(gate) * up).astype(BF16)
    o_ref[...] += jnp.dot(act, wd_ref[...], preferred_element_type=F32)


def mix_ffn(x, mixes, g, wg, wu, wd, tm, tf):
    n, d = x.shape
    f = wg.shape[1]
    assert n % tm == 0 and f % tf == 0
    in_specs = [pl.BlockSpec((tm, d), lambda i, j: (i, 0))]
    args = [x]
    for a, w in mixes:
        in_specs += [pl.BlockSpec((tm, a.shape[1]), lambda i, j: (i, 0)),
                     pl.BlockSpec(w.shape, lambda i, j: (0, 0))]
        args += [a, w]
    in_specs += [pl.BlockSpec((1, d), lambda i, j: (0, 0)),
                 pl.BlockSpec((d, tf), lambda i, j: (0, j)),
                 pl.BlockSpec((d, tf), lambda i, j: (0, j)),
                 pl.BlockSpec((tf, d), lambda i, j: (j, 0))]
    args += [g.reshape(1, d), wg, wu, wd]
    return pl.pallas_call(
        functools.partial(_mix_ffn_kernel, n_mix=len(mixes)),
        grid=(n // tm, f // tf),
        in_specs=in_specs,
        out_specs=pl.BlockSpec((tm, d), lambda i, j: (i, 0)),
        out_shape=jax.ShapeDtypeStruct((n, d), F32),
        scratch_shapes=[pltpu.VMEM((tm, d), BF16)],
        compiler_params=_cparams("parallel", "arbitrary"),
        name="mix_ffn",
    )(*args)


def _final_norm_kernel(x_ref, g_ref, o_ref, *, skip, rows, chunk):
    for c in range(rows // chunk):
        x = x_ref[0, skip + c * chunk: skip + (c + 1) * chunk, :]
        o_ref[0, c * chunk:(c + 1) * chunk, :] = _rms(x, g_ref[...])


def final_norm(x, g, skip):
    b, t, d = x.shape
    rows = t - skip
    chunk = 256 if rows % 256 == 0 else rows
    return pl.pallas_call(
        functools.partial(_final_norm_kernel, skip=skip, rows=rows, chunk=chunk),
        grid=(b,),
        in_specs=[pl.BlockSpec((1, t, d), lambda i: (i, 0, 0)),
                  pl.BlockSpec((1, d), lambda i: (0, 0))],
        out_specs=pl.BlockSpec((1, rows, d), lambda i: (i, 0, 0)),
        out_shape=jax.ShapeDtypeStruct((b, rows, d), F32),
        compiler_params=_cparams("parallel"),
        name="final_norm",
    )(x, g.reshape(1, d))


def _pool_kernel(u_ref, pre_ref, pw_ref, sc_ref, ob_ref, buf_ref, ext_s, *, t, pos0, rc):
    head = POOL_BUF + 1
    tproc = _round_up(t, rc)
    ext_s[0:head, :] = pre_ref[0]
    ext_s[head:head + t, :] = u_ref[0]
    if tproc > t:
        ext_s[head + t:head + tproc, :] = jnp.zeros((tproc - t, ext_s.shape[1]), F32)
    buf_ref[0] = ext_s[t + 1:t + head, :]
    for c in range(tproc // rc):
        r0 = c * rc
        nst = min(rc, t - r0)
        pos = pos0 + r0 + lax.broadcasted_iota(I32, (rc, 1), 0)
        for gi, w in enumerate(POOL_WINDOWS):
            sl = slice(gi * POOL_GROUP_DIM, (gi + 1) * POOL_GROUP_DIM)
            xt = ext_s[head + r0:head + r0 + rc, sl]
            acc = xt
            for r in range(1, w):
                acc = acc + ext_s[head + r0 - r:head + r0 - r + rc, sl]
            cnt = jnp.minimum(w, pos + 1).astype(F32)
            dlt = acc / cnt - xt
            y = jnp.dot(dlt.astype(BF16), pw_ref[gi], preferred_element_type=F32) * sc_ref[:, sl]
            ob_ref[0, r0:r0 + nst, sl] = y[0:nst].astype(ob_ref.dtype)


def pool_mixer(u, prefix16, pw, scale, pos0, rc, out_dtype):
    b, t, c = u.shape
    head = POOL_BUF + 1
    return pl.pallas_call(
        functools.partial(_pool_kernel, t=t, pos0=pos0, rc=rc),
        grid=(b,),
        in_specs=[pl.BlockSpec((1, t, c), lambda i: (i, 0, 0)),
                  pl.BlockSpec((1, head, c), lambda i: (i, 0, 0)),
                  pl.BlockSpec(pw.shape, lambda i: (0, 0, 0)),
                  pl.BlockSpec((1, c), lambda i: (0, 0))],
        out_specs=[pl.BlockSpec((1, t, c), lambda i: (i, 0, 0)),
                   pl.BlockSpec((1, POOL_BUF, c), lambda i: (i, 0, 0))],
        out_shape=[jax.ShapeDtypeStruct((b, t, c), out_dtype),
                   jax.ShapeDtypeStruct((b, POOL_BUF, c), F32)],
        scratch_shapes=[pltpu.VMEM((head + _round_up(t, rc), c), F32)],
        compiler_params=_cparams("parallel"),
        name="pool_mixer",
    )(u, prefix16, pw, scale.reshape(1, c))


def _sortable_key(score):
    bits = lax.bitcast_convert_type(score, I32)
    return bits ^ ((bits >> 31) & 0x7FFFFFFF)


def _count(mask):
    return jnp.sum(mask.astype(F32), axis=-1, keepdims=True)


def _topk_bias(key_s, jcut_s, vis, col, kk, ncols):
    kf = float(kk)
    t0 = jnp.where(_count(key_s[...] >= 0) >= kf, 0, INT_MIN).astype(I32)

    def bit_step(it, t):
        cand = t + lax.shift_left(jnp.int32(1), 30 - it)
        return jnp.where(_count(key_s[...] >= cand) >= kf, cand, t)

    thr = lax.fori_loop(0, 31, bit_step, t0)
    keys = key_s[...]
    n_ge = _count(keys >= thr)
    tie = (n_ge > kf) & (thr > INT_MIN)
    jcut_s[...] = jnp.full(jcut_s.shape, ncols, I32)

    @pl.when(jnp.max(tie.astype(F32)) > 0.0)
    def _():
        need = kf - _count(key_s[...] > thr)
        nbits = max(1, (ncols - 1).bit_length())

        def idx_step(it, jc):
            cand = jc + lax.shift_left(jnp.int32(1), nbits - 1 - it)
            n_lt = _count((key_s[...] == thr) & (col < cand))
            return jnp.where(n_lt < need, cand, jc)

        jc = lax.fori_loop(0, nbits, idx_step, jnp.zeros(thr.shape, I32))
        jcut_s[...] = jnp.where(tie, jc, ncols)

    sel = vis & ((keys > thr) | ((keys == thr) & (col <= jcut_s[...])))
    return jnp.where(sel, 0.0, MASK_BIAS).astype(F32)


def _dsa_prompt_kernel(q_ref, qi_ref, k_ref, v_ref, kiwi_ref, o_ref,
                       kb_s, vb_s, kib_s, key_s, bias_s, jcut_s, *, length, tq, topk):
    lp = kib_s.shape[0]
    pad = lp - length
    it = pl.program_id(1)

    @pl.when(it == 0)
    def _():
        for kh in range(A_KV_HEADS):
            sl = slice(kh * A_HEAD_DIM, (kh + 1) * A_HEAD_DIM)
            kb_s[kh, 0:length, :] = k_ref[0, :, sl].astype(BF16)
            vb_s[kh, 0:length, :] = v_ref[0, :, sl].astype(BF16)
            if pad:
                kb_s[kh, length:lp, :] = jnp.zeros((pad, A_HEAD_DIM), BF16)
                vb_s[kh, length:lp, :] = jnp.zeros((pad, A_HEAD_DIM), BF16)
        kib_s[0:length, :] = kiwi_ref[0, :, 0:IDX_DIM].astype(BF16)
        if pad:
            kib_s[length:lp, :] = jnp.zeros((pad, IDX_DIM), BF16)

    def tile(i):
        r0 = i * tq
        ke = min(lp, _round_up(r0 + tq, LANES))
        kib = kib_s[0:ke, :]
        score = jnp.zeros((tq, ke), F32)
        for h in range(IDX_HEADS):
            qih = qi_ref[0, :, h * IDX_DIM:(h + 1) * IDX_DIM].astype(BF16)
            wih = kiwi_ref[0, r0:r0 + tq, IDX_DIM + h:IDX_DIM + h + 1]
            score = score + jnp.maximum(_dot_nt(qih, kib), 0.0) * wih
        col = lax.broadcasted_iota(I32, (tq, ke), 1)
        pos = r0 + lax.broadcasted_iota(I32, (tq, ke), 0)
        vis = col <= pos
        key_v = key_s.at[:, 0:ke]
        key_v[...] = jnp.where(vis, _sortable_key(score), INT_MIN)
        bias_s[:, 0:ke] = _topk_bias(key_v, jcut_s, vis, col, topk, ke)

        for hd in range(A_HEADS):
            kh = hd // A_GROUP
            hs = slice(hd * A_HEAD_DIM, (hd + 1) * A_HEAD_DIM)
            qh = (q_ref[0, :, hs] * (A_HEAD_DIM ** -0.5)).astype(BF16)
            logits = _dot_nt(qh, kb_s[kh, 0:ke, :]) + bias_s[:, 0:ke]
            m = jnp.max(logits, axis=-1, keepdims=True)
            e = jnp.exp(logits - m)
            den = jnp.sum(e, axis=-1, keepdims=True)
            o = jnp.dot(e.astype(BF16), vb_s[kh, 0:ke, :], preferred_element_type=F32)
            o_ref[0, :, hs] = o / den

    for i in range(length // tq):
        pl.when(it == i)(functools.partial(tile, i))


def dsa_prompt(q, qi, k, v, kiwi, tq):
    b, length, _ = q.shape
    assert length % tq == 0
    topk = min(TOPK_MAX, length // 4)
    lp = _round_up(length, LANES)
    seq = lambda c: pl.BlockSpec((1, length, c), lambda i, j: (i, 0, 0))
    til = lambda c: pl.BlockSpec((1, tq, c), lambda i, j: (i, j, 0))
    return pl.pallas_call(
        functools.partial(_dsa_prompt_kernel, length=length, tq=tq, topk=topk),
        grid=(b, length // tq),
        in_specs=[til(q.shape[2]), til(qi.shape[2]), seq(k.shape[2]), seq(v.shape[2]), seq(kiwi.shape[2])],
        out_specs=til(q.shape[2]),
        out_shape=jax.ShapeDtypeStruct(q.shape, F32),
        scratch_shapes=[pltpu.VMEM((A_KV_HEADS, lp, A_HEAD_DIM), BF16),
                        pltpu.VMEM((A_KV_HEADS, lp, A_HEAD_DIM), BF16),
                        pltpu.VMEM((lp, IDX_DIM), BF16),
                        pltpu.VMEM((tq, lp), I32),
                        pltpu.VMEM((tq, lp), F32),
                        pltpu.VMEM((tq, 1), I32)],
        compiler_params=_cparams("parallel", "arbitrary"),
        name="dsa_prompt",
    )(q, qi, k, v, kiwi)


def _dsa_sample_score_kernel(pt_ref, qi_ref, wi_ref, kinew_ref, *rest, pp, t):
    pages = rest[:pp]
    bias_ref, key_s, jcut_s, kin_s = rest[pp:]
    j = pl.program_id(1)
    nsteps = pl.num_programs(1)
    cw = pp * PAGE_SIZE
    past = key_s.shape[1] - LANES
    qi = qi_ref[0].astype(BF16)
    wi = wi_ref[0]

    def scores(ki):
        s = jnp.maximum(_dot_nt(qi, ki), 0.0) * wi
        return jnp.sum(s.reshape(t, IDX_HEADS, ki.shape[0]), axis=1)

    ki = jnp.concatenate([p[0, 0] for p in pages], axis=0).astype(BF16)
    key_chunk = _sortable_key(scores(ki))
    for jj in range(past // cw):
        @pl.when(j == jj)
        def _(jj=jj):
            key_s[:, jj * cw:(jj + 1) * cw] = key_chunk

    @pl.when(j == nsteps - 1)
    def _():
        kin_s[...] = jnp.zeros(kin_s.shape, F32)
        kin_s[0:t, :] = kinew_ref[0]
        snew = scores(kin_s[...].astype(BF16))
        cn = lax.broadcasted_iota(I32, (t, LANES), 1)
        rn = lax.broadcasted_iota(I32, (t, LANES), 0)
        key_s[:, past:past + LANES] = jnp.where(cn <= rn, _sortable_key(snew), INT_MIN)
        ncols = past + LANES
        col = lax.broadcasted_iota(I32, (t, ncols), 1)
        pos = past + lax.broadcasted_iota(I32, (t, ncols), 0)
        vis = col <= pos
        topk = min(TOPK_MAX, (past + t) // 4)
        bias = _topk_bias(key_s, jcut_s, vis, col, topk, ncols)
        for g in range(A_GROUP):
            bias_ref[0, g * t:(g + 1) * t, :] = bias


def _dsa_sample_attn_kernel(pt_ref, q_ref, bias_ref, biasnew_ref, knew_ref, vnew_ref, *rest, pp, t):
    kpages = rest[:pp]
    vpages = rest[pp:2 * pp]
    o_ref, m_s, l_s, acc_s, new_s = rest[2 * pp:]
    j = pl.program_id(1)
    nsteps = pl.num_programs(1)

    @pl.when(j == 0)
    def _():
        m_s[...] = jnp.full(m_s.shape, MASK_BIAS, F32)
        l_s[...] = jnp.zeros(l_s.shape, F32)
        acc_s[...] = jnp.zeros(acc_s.shape, F32)

    def update(kh, kb, vb, bias):
        qh = (q_ref[0, kh] * (A_HEAD_DIM ** -0.5)).astype(BF16)
        logits = _dot_nt(qh, kb) + bias
        m_old = m_s[kh]
        m_new = jnp.maximum(m_old, jnp.max(logits, axis=-1, keepdims=True))
        alpha = jnp.exp(m_old - m_new)
        e = jnp.exp(logits - m_new)
        l_s[kh] = alpha * l_s[kh] + jnp.sum(e, axis=-1, keepdims=True)
        acc_s[kh] = alpha * acc_s[kh] + jnp.dot(e.astype(BF16), vb, preferred_element_type=F32)
        m_s[kh] = m_new

    kc = jnp.concatenate([p[0, 0] for p in kpages], axis=0).astype(BF16)
    vc = jnp.concatenate([p[0, 0] for p in vpages], axis=0).astype(BF16)
    for kh in range(A_KV_HEADS):
        sl = slice(kh * A_HEAD_DIM, (kh + 1) * A_HEAD_DIM)
        update(kh, kc[:, sl], vc[:, sl], bias_ref[0])

    @pl.when(j == nsteps - 1)
    def _():
        for kh in range(A_KV_HEADS):
            sl = slice(kh * A_HEAD_DIM, (kh + 1) * A_HEAD_DIM)
            new_s[...] = jnp.zeros(new_s.shape, F32)
            new_s[0, 0:t, :] = knew_ref[0, :, sl]
            new_s[1, 0:t, :] = vnew_ref[0, :, sl]
            update(kh, new_s[0].astype(BF16), new_s[1].astype(BF16), biasnew_ref[0])
            o_ref[0, kh] = acc_s[kh] / l_s[kh]


def dsa_sample(q_g, qi_t, wi_t, ki_new, k_new, v_new, cache_k, cache_v, cache_ki, page_flat, layer, pp):
    db, _, gt, hd = q_g.shape
    t = gt // A_GROUP
    npg = page_flat.shape[0] // db
    assert npg % pp == 0
    nsteps = npg // pp
    past = npg * PAGE_SIZE
    ncols = past + LANES
    cw = pp * PAGE_SIZE
    kvw = cache_k.shape[3]

    def page_spec(width, i):
        return pl.BlockSpec((1, 1, PAGE_SIZE, width),
                            lambda b, j, pt, i=i: (layer, pt[b * npg + j * pp + i], 0, 0))

    bias = pl.pallas_call(
        functools.partial(_dsa_sample_score_kernel, pp=pp, t=t),
        grid_spec=pltpu.PrefetchScalarGridSpec(
            num_scalar_prefetch=1,
            grid=(db, nsteps),
            in_specs=[pl.BlockSpec((1, t * IDX_HEADS, IDX_DIM), lambda b, j, pt: (b, 0, 0)),
                      pl.BlockSpec((1, t * IDX_HEADS, 1), lambda b, j, pt: (b, 0, 0)),
                      pl.BlockSpec((1, t, IDX_DIM), lambda b, j, pt: (b, 0, 0))]
                     + [page_spec(IDX_DIM, i) for i in range(pp)],
            out_specs=pl.BlockSpec((1, gt, ncols), lambda b, j, pt: (b, 0, 0)),
            scratch_shapes=[pltpu.VMEM((t, ncols), I32),
                            pltpu.VMEM((t, 1), I32),
                            pltpu.VMEM((LANES, IDX_DIM), F32)]),
        out_shape=jax.ShapeDtypeStruct((db, gt, ncols), F32),
        compiler_params=_cparams("parallel", "arbitrary"),
        name="dsa_sample_score",
    )(page_flat, qi_t, wi_t, ki_new, *([cache_ki] * pp))

    out = pl.pallas_call(
        functools.partial(_dsa_sample_attn_kernel, pp=pp, t=t),
        grid_spec=pltpu.PrefetchScalarGridSpec(
            num_scalar_prefetch=1,
            grid=(db, nsteps),
            in_specs=[pl.BlockSpec((1, A_KV_HEADS, gt, hd), lambda b, j, pt: (b, 0, 0, 0)),
                      pl.BlockSpec((1, gt, cw), lambda b, j, pt: (b, 0, j)),
                      pl.BlockSpec((1, gt, LANES), lambda b, j, pt: (b, 0, past // LANES)),
                      pl.BlockSpec((1, t, kvw), lambda b, j, pt: (b, 0, 0)),
                      pl.BlockSpec((1, t, kvw), lambda b, j, pt: (b, 0, 0))]
                     + [page_spec(kvw, i) for i in range(pp)]
                     + [page_spec(kvw, i) for i in range(pp)],
            out_specs=pl.BlockSpec((1, A_KV_HEADS, gt, hd), lambda b, j, pt: (b, 0, 0, 0)),
            scratch_shapes=[pltpu.VMEM((A_KV_HEADS, gt, 1), F32),
                            pltpu.VMEM((A_KV_HEADS, gt, 1), F32),
                            pltpu.VMEM((A_KV_HEADS, gt, hd), F32),
                            pltpu.VMEM((2, LANES, hd), F32)]),
        out_shape=jax.ShapeDtypeStruct((db, A_KV_HEADS, gt, hd), F32),
        compiler_params=_cparams("parallel", "arbitrary"),
        name="dsa_sample_attn",
    )(page_flat, q_g, bias, bias, k_new, v_new, *([cache_k] * pp), *([cache_v] * pp))
    return out


def _gla_kernel(*refs, tb, tv, chunk, sub, has_s0):
    if has_s0:
        q_ref, k_ref, v_ref, g_ref, gd_ref, wgu_ref, bg_ref, gn_ref, s0_ref, on_ref, sout_ref, la_s, st_s = refs
    else:
        q_ref, k_ref, v_ref, g_ref, gd_ref, wgu_ref, bg_ref, gn_ref, on_ref, sout_ref, la_s, st_s = refs
    jb = pl.program_id(1)
    nb = pl.num_programs(1)

    @pl.when(jb == 0)
    def _():
        for h in range(GLA_HEADS):
            if has_s0:
                st_s[h] = s0_ref[0, h].T
            else:
                st_s[h] = jnp.zeros((GLA_DV, GLA_DK), F32)

    x = jnp.dot(gd_ref[0], wgu_ref[...], precision=HIGHEST, preferred_element_type=F32) + bg_ref[...]
    la = (jnp.minimum(x, 0.0) - jnp.log1p(jnp.exp(-jnp.abs(x)))) * (1.0 / GLA_GATE_NORM)
    if tv is not None:
        row = jb * tb + lax.broadcasted_iota(I32, (tb, 1), 0)
        la = jnp.where(row < tv, la, 0.0)
    la_s[...] = la

    def do_chunk(r0, c):
        rows = pl.ds(r0, c)
        ri = lax.broadcasted_iota(I32, (c, c), 0)
        ci = lax.broadcasted_iota(I32, (c, c), 1)
        tri = (ri >= ci).astype(F32)
        b = jnp.dot(tri, la_s[rows, :], precision=HIGHEST, preferred_element_type=F32)
        qc = q_ref[0, rows, :] * (GLA_DK ** -0.5)
        kc = k_ref[0, rows, :]
        if tv is not None:
            rowc = jb * tb + r0 + lax.broadcasted_iota(I32, (c, 1), 0)
            kc = jnp.where(rowc < tv, kc, 0.0)
        vc = v_ref[0, rows, :].astype(BF16)
        gc = g_ref[0, rows, :]
        for h in range(GLA_HEADS):
            sk = slice(h * GLA_DK, (h + 1) * GLA_DK)
            sv = slice(h * GLA_DV, (h + 1) * GLA_DV)
            bh, qh, kh, vh = b[:, sk], qc[:, sk], kc[:, sk], vc[:, sv]
            st = st_s[h]
            o_inter = _dot_nt((qh * jnp.exp(bh)).astype(BF16), st.astype(BF16))
            parts = []
            for i in range(c // sub):
                lo, hi = i * sub, (i + 1) * sub
                ref_b = bh[lo - 1:lo, :] if i > 0 else jnp.zeros((1, GLA_DK), F32)
                qt = (qh[lo:hi] * jnp.exp(bh[lo:hi] - ref_b)).astype(BF16)
                kt = (kh[0:hi] * jnp.exp(ref_b - bh[0:hi])).astype(BF16)
                att = _dot_nt(qt, kt)
                causal = (lax.broadcasted_iota(I32, (sub, hi), 1)
                          <= lo + lax.broadcasted_iota(I32, (sub, hi), 0))
                att = jnp.where(causal, att, 0.0).astype(BF16)
                parts.append(jnp.dot(att, vh[0:hi], preferred_element_type=F32))
            o = o_inter + (jnp.concatenate(parts, axis=0) if len(parts) > 1 else parts[0])
            bl = bh[c - 1:c, :]
            kd = (kh * jnp.exp(bl - bh)).astype(BF16)
            st_s[h] = jnp.exp(bl) * st + _dot_tn(vh, kd)
            on = _rms(o, gn_ref[...])
            gh = gc[:, sv]
            on_ref[0, rows, sv] = (on * (gh * jax.nn.sigmoid(gh))).astype(on_ref.dtype)

    nfull, rem = tb // chunk, tb % chunk
    if nfull == 1:
        do_chunk(0, chunk)
    elif nfull > 1:
        def body(ic, carry):
            do_chunk(pl.multiple_of(ic * chunk, BF16_ROWS), chunk)
            return carry
        lax.fori_loop(0, nfull, body, 0)
    if rem:
        do_chunk(nfull * chunk, rem)

    @pl.when(jb == nb - 1)
    def _():
        for h in range(GLA_HEADS):
            sout_ref[0, h] = st_s[h].T


def gla_mixer(q, k, v, g, gd, wgu, bg, gn, s0, tb, chunk, tv):
    b, t, _ = q.shape
    sub = BF16_ROWS
    assert t % tb == 0 and tb % sub == 0 and chunk % sub == 0
    blk = lambda c: pl.BlockSpec((1, tb, c), lambda i, j: (i, j, 0))
    cst = lambda a: pl.BlockSpec(a.shape, lambda i, j: (0,) * a.ndim)
    st_spec = pl.BlockSpec((1, GLA_HEADS, GLA_DK, GLA_DV), lambda i, j: (i, 0, 0, 0))
    bg2, gn2 = bg.reshape(1, -1), gn.reshape(1, -1)
    args = [q, k, v, g, gd, wgu, bg2, gn2]
    in_specs = [blk(q.shape[2]), blk(k.shape[2]), blk(v.shape[2]), blk(g.shape[2]), blk(gd.shape[2]),
                cst(wgu), cst(bg2), cst(gn2)]
    if s0 is not None:
        args.append(s0)
        in_specs.append(st_spec)
    return pl.pallas_call(
        functools.partial(_gla_kernel, tb=tb, tv=tv, chunk=chunk, sub=sub, has_s0=s0 is not None),
        grid=(b, t // tb),
        in_specs=in_specs,
        out_specs=[blk(v.shape[2]), st_spec],
        out_shape=[jax.ShapeDtypeStruct(v.shape, BF16),
                   jax.ShapeDtypeStruct((b, GLA_HEADS, GLA_DK, GLA_DV), F32)],
        scratch_shapes=[pltpu.VMEM((tb, GLA_HEADS * GLA_DK), F32),
                        pltpu.VMEM((GLA_HEADS, GLA_DV, GLA_DK), F32)],
        compiler_params=_cparams("parallel", "arbitrary"),
        name="gla_mixer",
    )(*args)


def _pick_tile(n, candidates):
    for c in candidates:
        if n % c == 0:
            return c
    return n


def kernel(x_prompt, x_sample, cache_k, cache_v, cache_kidx, state_pool, state_gla, page_table,
           meta_tokens, norm_mix, norm_ffn, norm_final, w_in_even, w_out_even, pool_w, pool_scale,
           w_in_odd, gla_w_gate_up, gla_b_gate, gla_norm, w_out_odd, ffn_w_gate, ffn_w_up, ffn_w_down):
    bsz, seq, d = x_prompt.shape
    db, dseq, _ = x_sample.shape
    depth = norm_mix.shape[0]
    length = seq + N_META
    n_even = cache_k.shape[0]
    n_pool = cache_k.shape[1]
    npg = page_table.shape[1]
    past = npg * PAGE_SIZE
    a_width = A_HEADS * A_HEAD_DIM
    kv_width = A_KV_HEADS * A_HEAD_DIM
    pool_width = pool_scale.shape[1]
    gk = GLA_HEADS * GLA_DK
    gv = GLA_HEADS * GLA_DV
    d_ff = ffn_w_gate.shape[2]

    meta = jnp.broadcast_to(meta_tokens[None], (bsz, N_META, d))
    xp = jnp.concatenate([meta, x_prompt], axis=1).reshape(bsz * length, d)
    xs = x_sample.reshape(db * dseq, d)
    n_p, n_s = xp.shape[0], xs.shape[0]

    tm_p = _pick_tile(n_p, (688, 512, 256, 128, 64, 32, 16, 8))
    tm_s = _pick_tile(n_s, (128, 64, 32, 16, 8))
    tf = _pick_tile(d_ff, (1408, 1024, 512, 256, 128))
    tq = _pick_tile(length, (344, 256, 128, 64, 32, 16, 8))
    pool_rc = _pick_tile(length, (48, 32, 16, 8))
    gla_tb = _pick_tile(length, (688, 512, 256, 128, 64, 32, 16))
    pp = _pick_tile(npg, (16, 8, 4, 2, 1))

    e_off = [0]
    for s in (a_width, kv_width, kv_width, IDX_HEADS * IDX_DIM, IDX_DIM, IDX_HEADS, pool_width):
        e_off.append(e_off[-1] + s)
    kiwi_pad = LANES - IDX_DIM - IDX_HEADS
    even_widths = (a_width, kv_width, kv_width, IDX_HEADS * IDX_DIM, pool_width, LANES)
    o_off = [0]
    for s in (gk, gk, gv, gv, GLA_GATE_RANK):
        o_off.append(o_off[-1] + s)
    odd_widths = (gk, gk, gv, gv, LANES)

    cache_k4 = cache_k.reshape(n_even, n_pool, PAGE_SIZE, kv_width)
    cache_v4 = cache_v.reshape(n_even, n_pool, PAGE_SIZE, kv_width)
    page_flat = page_table.reshape(-1).astype(I32)

    kp_l, vp_l, kip_l, bp_l, ks_l, vs_l, kis_l, bs_l, sp_l, ss_l = ([] for _ in range(10))
    for l in range(depth):
        if l % 2 == 0:
            e = l // 2
            w = w_in_even[e]
            w_in = jnp.concatenate(
                [w[:, e_off[0]:e_off[4]], w[:, e_off[6]:e_off[7]], w[:, e_off[4]:e_off[6]],
                 jnp.zeros((d, kiwi_pad), w.dtype)], axis=1).astype(BF16)
            w_out = w_out_even[e].astype(BF16)
            pw = pool_w[e].astype(BF16)

            q, k, v, qi, u, kiwi = norm_matmul(xp, norm_mix[l], w_in, even_widths, tm_p)
            r3 = lambda a: a.reshape(bsz, length, a.shape[1])
            oa = dsa_prompt(r3(q), r3(qi), r3(k), r3(v), r3(kiwi), tq)
            ob, buf = pool_mixer(r3(u), jnp.zeros((bsz, POOL_BUF + 1, pool_width), F32), pw,
                                 pool_scale[e], 0, pool_rc, BF16)
            mix_p = [(oa.reshape(n_p, a_width), w_out[:a_width]),
                     (ob.reshape(n_p, pool_width), w_out[a_width:])]
            kp_l.append(k.reshape(bsz, length, A_KV_HEADS, A_HEAD_DIM))
            vp_l.append(v.reshape(bsz, length, A_KV_HEADS, A_HEAD_DIM))
            kip_l.append(kiwi[:, :IDX_DIM].reshape(bsz, length, IDX_DIM))
            bp_l.append(buf)

            q, k, v, qi, u, kiwi = norm_matmul(xs, norm_mix[l], w_in, even_widths, tm_s)
            s3 = lambda a: a.reshape(db, dseq, a.shape[1])
            q_g = q.reshape(db, dseq, A_KV_HEADS, A_GROUP, A_HEAD_DIM).transpose(0, 2, 3, 1, 4)
            q_g = q_g.reshape(db, A_KV_HEADS, A_GROUP * dseq, A_HEAD_DIM)
            qi_t = qi.reshape(db, dseq * IDX_HEADS, IDX_DIM)
            wi_t = kiwi[:, IDX_DIM:IDX_DIM + IDX_HEADS].reshape(db, dseq * IDX_HEADS, 1)
            ki_new = kiwi[:, :IDX_DIM].reshape(db, dseq, IDX_DIM)
            oa = dsa_sample(q_g, qi_t, wi_t, ki_new, s3(k), s3(v), cache_k4, cache_v4, cache_kidx,
                            page_flat, e, pp)
            oa = oa.reshape(db, A_KV_HEADS, A_GROUP, dseq, A_HEAD_DIM).transpose(0, 3, 1, 2, 4)
            prefix = jnp.pad(state_pool[e], ((0, 0), (1, 0), (0, 0)))
            ob, buf = pool_mixer(s3(u), prefix, pw, pool_scale[e], past, BF16_ROWS, F32)
            mix_s = [(oa.reshape(n_s, a_width), w_out[:a_width]),
                     (ob.reshape(n_s, pool_width), w_out[a_width:])]
            ks_l.append(k.reshape(db, dseq, A_KV_HEADS, A_HEAD_DIM))
            vs_l.append(v.reshape(db, dseq, A_KV_HEADS, A_HEAD_DIM))
            kis_l.append(ki_new)
            bs_l.append(buf)
        else:
            o = l // 2
            w = w_in_odd[o]
            w_in = jnp.concatenate([w, jnp.zeros((d, LANES - GLA_GATE_RANK), w.dtype)], axis=1).astype(BF16)
            w_out = w_out_odd[o].astype(BF16)
            wgu = jnp.pad(gla_w_gate_up[o], ((0, LANES - GLA_GATE_RANK), (0, 0)))

            q, k, v, g, gd = norm_matmul(xp, norm_mix[l], w_in, odd_widths, tm_p)
            r3 = lambda a: a.reshape(bsz, length, a.shape[1])
            on, s_p = gla_mixer(r3(q), r3(k), r3(v), r3(g), r3(gd), wgu, gla_b_gate[o], gla_norm[o],
                                None, gla_tb, 48, None)
            mix_p = [(on.reshape(n_p, gv), w_out)]
            sp_l.append(s_p)

            q, k, v, g, gd = norm_matmul(xs, norm_mix[l], w_in, odd_widths, tm_s)
            tpad = _round_up(dseq, BF16_ROWS)
            p3 = lambda a: jnp.pad(a.reshape(db, dseq, a.shape[1]), ((0, 0), (0, tpad - dseq), (0, 0)))
            on, s_s = gla_mixer(p3(q), p3(k), p3(v), p3(g), p3(gd), wgu, gla_b_gate[o], gla_norm[o],
                                state_gla[o], tpad, tpad, dseq)
            mix_s = [(on[:, :dseq].reshape(n_s, gv), w_out)]
            ss_l.append(s_s)

        wg, wu, wd = ffn_w_gate[l].astype(BF16), ffn_w_up[l].astype(BF16), ffn_w_down[l].astype(BF16)
        xp = mix_ffn(xp, mix_p, norm_ffn[l], wg, wu, wd, tm_p, tf)
        xs = mix_ffn(xs, mix_s, norm_ffn[l], wg, wu, wd, tm_s, tf)

    y_prompt = final_norm(xp.reshape(bsz, length, d), norm_final, N_META)
    y_sample = final_norm(xs.reshape(1, n_s, d), norm_final, 0).reshape(db, dseq, d)
    return (y_prompt, y_sample,
            jnp.stack(kp_l), jnp.stack(vp_l), jnp.stack(kip_l),
            jnp.stack(ks_l), jnp.stack(vs_l), jnp.stack(kis_l),
            jnp.stack(bp_l), jnp.stack(bs_l),
            jnp.stack(sp_l), jnp.stack(ss_l))
```

```python
import functools

import jax
import jax.numpy as jnp
from jax import lax
from jax.experimental import pallas as pl
from jax.experimental.pallas import tpu as pltpu

F32 = jnp.float32
BF16 = jnp.bfloat16
I32 = jnp.int32

NORM_EPS = 1e-6
N_META = 16
A_HEADS = 8
A_KV_HEADS = 2
A_GROUP = A_HEADS // A_KV_HEADS
A_HEAD_DIM = 64
IDX_HEADS = 8
IDX_DIM = 64
TOPK_MAX = 256
PAGE_SIZE = 128
POOL_WINDOWS = (2, 4, 8, 16)
POOL_GROUP_DIM = 128
POOL_BUF = 15
GLA_HEADS = 4
GLA_DK = 128
GLA_DV = 256
GLA_GATE_RANK = 16
GLA_GATE_NORM = 16.0

LANES = 128
SUBLANES = 8
BF16_ROWS = 16
VMEM_LIMIT_BYTES = 56 * 1024 * 1024

DSA_KEY_CHUNK = 2 * LANES

INT_MIN = -(2 ** 31)
MASK_BIAS = -1e30
HIGHEST = lax.Precision.HIGHEST


def _round_up(n, m):
    return (n + m - 1) // m * m


def _cparams(*sem):
    return pltpu.CompilerParams(dimension_semantics=sem, vmem_limit_bytes=VMEM_LIMIT_BYTES)


def _rms(x, g):
    ms = jnp.mean(x * x, axis=-1, keepdims=True)
    return (x * lax.rsqrt(ms + NORM_EPS)) * g


def _dot_nt(a, b):
    return lax.dot_general(a, b, (((1,), (1,)), ((), ())), preferred_element_type=F32)


def _dot_tn(a, b):
    return lax.dot_general(a, b, (((0,), (0,)), ((), ())), preferred_element_type=F32)


def _norm_matmul_kernel(x_ref, g_ref, w_ref, *o_refs, widths):
    h = _rms(x_ref[...], g_ref[...]).astype(BF16)
    off = 0
    for o_ref, wd in zip(o_refs, widths):
        o_ref[...] = jnp.dot(h, w_ref[:, off:off + wd], preferred_element_type=F32)
        off += wd


def norm_matmul(x, g, w, widths, tm):
    n, d = x.shape
    assert n % tm == 0 and sum(widths) == w.shape[1]
    return pl.pallas_call(
        functools.partial(_norm_matmul_kernel, widths=widths),
        grid=(n // tm,),
        in_specs=[pl.BlockSpec((tm, d), lambda i: (i, 0)),
                  pl.BlockSpec((1, d), lambda i: (0, 0)),
                  pl.BlockSpec((d, w.shape[1]), lambda i: (0, 0))],
        out_specs=[pl.BlockSpec((tm, wd), lambda i: (i, 0)) for wd in widths],
        out_shape=[jax.ShapeDtypeStruct((n, wd), F32) for wd in widths],
        compiler_params=_cparams("parallel"),
        name="norm_matmul",
    )(x, g.reshape(1, d), w)


def _mix_ffn_kernel(*refs, n_mix):
    x_ref = refs[0]
    mix = refs[1:1 + 2 * n_mix]
    g_ref, wg_ref, wu_ref, wd_ref, o_ref, h_s = refs[1 + 2 * n_mix:]
    j = pl.program_id(1)

    @pl.when(j == 0)
    def _():
        xm = x_ref[...]
        for m in range(n_mix):
            a_ref, w_ref = mix[2 * m], mix[2 * m + 1]
            xm = xm + jnp.dot(a_ref[...].astype(BF16), w_ref[...], preferred_element_type=F32)
        o_ref[...] = xm
        h_s[...] = _rms(xm, g_ref[...]).astype(BF16)

    h = h_s[...]
    gate = jnp.dot(h, wg_ref[...], preferred_element_type=F32)
    up = jnp.dot(h, wu_ref[...], preferred_element_type=F32)
    act = (gate * jax.nn.sigmoid(gate) * up).astype(BF16)
    o_ref[...] += jnp.dot(act, wd_ref[...], preferred_element_type=F32)


def mix_ffn(x, mixes, g, wg, wu, wd, tm, tf):
    n, d = x.shape
    f = wg.shape[1]
    assert n % tm == 0 and f % tf == 0
    in_specs = [pl.BlockSpec((tm, d), lambda i, j: (i, 0))]
    args = [x]
    for a, w in mixes:
        in_specs += [pl.BlockSpec((tm, a.shape[1]), lambda i, j: (i, 0)),
                     pl.BlockSpec(w.shape, lambda i, j: (0, 0))]
        args += [a, w]
    in_specs += [pl.BlockSpec((1, d), lambda i, j: (0, 0)),
                 pl.BlockSpec((d, tf), lambda i, j: (0, j)),
                 pl.BlockSpec((d, tf), lambda i, j: (0, j)),
                 pl.BlockSpec((tf, d), lambda i, j: (j, 0))]
    args += [g.reshape(1, d), wg, wu, wd]
    return pl.pallas_call(
        functools.partial(_mix_ffn_kernel, n_mix=len(mixes)),
        grid=(n // tm, f // tf),
        in_specs=in_specs,
        out_specs=pl.BlockSpec((tm, d), lambda i, j: (i, 0)),
        out_shape=jax.ShapeDtypeStruct((n, d), F32),
        scratch_shapes=[pltpu.VMEM((tm, d), BF16)],
        compiler_params=_cparams("parallel", "arbitrary"),
        name="mix_ffn",
    )(*args)


def _final_norm_kernel(x_ref, g_ref, o_ref, *, skip, rows, chunk):
    for c in range(rows // chunk):
        x = x_ref[0, skip + c * chunk: skip + (c + 1) * chunk, :]
        o_ref[0, c * chunk:(c + 1) * chunk, :] = _rms(x, g_ref[...])


def final_norm(x, g, skip):
    b, t, d = x.shape
    rows = t - skip
    chunk = 256 if rows % 256 == 0 else rows
    return pl.pallas_call(
        functools.partial(_final_norm_kernel, skip=skip, rows=rows, chunk=chunk),
        grid=(b,),
        in_specs=[pl.BlockSpec((1, t, d), lambda i: (i, 0, 0)),
                  pl.BlockSpec((1, d), lambda i: (0, 0))],
        out_specs=pl.BlockSpec((1, rows, d), lambda i: (i, 0, 0)),
        out_shape=jax.ShapeDtypeStruct((b, rows, d), F32),
        compiler_params=_cparams("parallel"),
        name="final_norm",
    )(x, g.reshape(1, d))


def _pool_kernel(u_ref, pre_ref, pw_ref, sc_ref, ob_ref, buf_ref, ext_s, *, t, pos0, rc):
    head = POOL_BUF + 1
    tproc = _round_up(t, rc)
    ext_s[0:head, :] = pre_ref[0]
    ext_s[head:head + t, :] = u_ref[0]
    if tproc > t:
        ext_s[head + t:head + tproc, :] = jnp.zeros((tproc - t, ext_s.shape[1]), F32)
    buf_ref[0] = ext_s[t + 1:t + head, :]
    for c in range(tproc // rc):
        r0 = c * rc
        nst = min(rc, t - r0)
        pos = pos0 + r0 + lax.broadcasted_iota(I32, (rc, 1), 0)
        for gi, w in enumerate(POOL_WINDOWS):
            sl = slice(gi * POOL_GROUP_DIM, (gi + 1) * POOL_GROUP_DIM)
            xt = ext_s[head + r0:head + r0 + rc, sl]
            acc = xt
            for r in range(1, w):
                acc = acc + ext_s[head + r0 - r:head + r0 - r + rc, sl]
            cnt = jnp.minimum(w, pos + 1).astype(F32)
            dlt = acc / cnt - xt
            y = jnp.dot(dlt.astype(BF16), pw_ref[gi], preferred_element_type=F32) * sc_ref[:, sl]
            ob_ref[0, r0:r0 + nst, sl] = y[0:nst].astype(ob_ref.dtype)


def pool_mixer(u, prefix16, pw, scale, pos0, rc, out_dtype):
    b, t, c = u.shape
    head = POOL_BUF + 1
    return pl.pallas_call(
        functools.partial(_pool_kernel, t=t, pos0=pos0, rc=rc),
        grid=(b,),
        in_specs=[pl.BlockSpec((1, t, c), lambda i: (i, 0, 0)),
                  pl.BlockSpec((1, head, c), lambda i: (i, 0, 0)),
                  pl.BlockSpec(pw.shape, lambda i: (0, 0, 0)),
                  pl.BlockSpec((1, c), lambda i: (0, 0))],
        out_specs=[pl.BlockSpec((1, t, c), lambda i: (i, 0, 0)),
                   pl.BlockSpec((1, POOL_BUF, c), lambda i: (i, 0, 0))],
        out_shape=[jax.ShapeDtypeStruct((b, t, c), out_dtype),
                   jax.ShapeDtypeStruct((b, POOL_BUF, c), F32)],
        scratch_shapes=[pltpu.VMEM((head + _round_up(t, rc), c), F32)],
        compiler_params=_cparams("parallel"),
        name="pool_mixer",
    )(u, prefix16, pw, scale.reshape(1, c))


def _sortable_key(score):
    bits = lax.bitcast_convert_type(score, I32)
    return bits ^ ((bits >> 31) & 0x7FFFFFFF)


def _count(mask):
    return jnp.sum(mask.astype(F32), axis=-1, keepdims=True)


def _topk_bias(key_s, jcut_s, vis, col, kk, ncols):
    kf = float(kk)
    t0 = jnp.where(_count(key_s[...] >= 0) >= kf, 0, INT_MIN).astype(I32)

    def bit_step(it, t):
        cand = t + lax.shift_left(jnp.int32(1), 30 - it)
        return jnp.where(_count(key_s[...] >= cand) >= kf, cand, t)

    thr = lax.fori_loop(0, 31, bit_step, t0)
    keys = key_s[...]
    n_ge = _count(keys >= thr)
    tie = (n_ge > kf) & (thr > INT_MIN)
    jcut_s[...] = jnp.full(jcut_s.shape, ncols, I32)

    @pl.when(jnp.max(tie.astype(F32)) > 0.0)
    def _():
        need = kf - _count(key_s[...] > thr)
        nbits = max(1, (ncols - 1).bit_length())

        def idx_step(it, jc):
            cand = jc + lax.shift_left(jnp.int32(1), nbits - 1 - it)
            n_lt = _count((key_s[...] == thr) & (col < cand))
            return jnp.where(n_lt < need, cand, jc)

        jc = lax.fori_loop(0, nbits, idx_step, jnp.zeros(thr.shape, I32))
        jcut_s[...] = jnp.where(tie, jc, ncols)

    sel = vis & ((keys > thr) | ((keys == thr) & (col <= jcut_s[...])))
    return jnp.where(sel, 0.0, MASK_BIAS).astype(F32)


def _lane_tile(x, n):
    return x if n == 1 else jnp.concatenate([x] * n, axis=1)


def _row_total(x):
    return jnp.broadcast_to(jnp.sum(x, axis=-1, keepdims=True), x.shape)


def _dsa_prompt_kernel(q_ref, qi_ref, k_ref, v_ref, kiwi_ref, o_ref,
                       kb_s, vb_s, kib_s, qb_s, qib_s, wib_s, key_s, m_s, acc_s,
                       *, length, tq, kc, topk):
    nkc = kib_s.shape[0]
    nl = kc // LANES
    it = pl.program_id(1)
    r0 = it * tq
    nk = (r0 + tq + kc - 1) // kc
    kf = float(topk)

    @pl.when(it == 0)
    def _():
        ones_col = (lax.broadcasted_iota(I32, (kc, LANES - A_HEAD_DIM), 1) == 0).astype(BF16)
        for c in range(nkc):
            lo = c * kc
            valid = max(0, min(kc, length - lo))
            for kh in range(A_KV_HEADS):
                sl = slice(kh * A_HEAD_DIM, (kh + 1) * A_HEAD_DIM)
                if valid < kc:
                    kb_s[kh, c] = jnp.zeros((kc, A_HEAD_DIM), BF16)
                    vb_s[kh, c, :, 0:A_HEAD_DIM] = jnp.zeros((kc, A_HEAD_DIM), BF16)
                if valid:
                    kb_s[kh, c, 0:valid, :] = k_ref[0, lo:lo + valid, sl].astype(BF16)
                    vb_s[kh, c, 0:valid, 0:A_HEAD_DIM] = v_ref[0, lo:lo + valid, sl].astype(BF16)
                vb_s[kh, c, :, A_HEAD_DIM:LANES] = ones_col
            if valid < kc:
                kib_s[c] = jnp.zeros((kc, IDX_DIM), BF16)
            if valid:
                kib_s[c, 0:valid, :] = kiwi_ref[0, lo:lo + valid, 0:IDX_DIM].astype(BF16)

    kiwi_t = kiwi_ref[0, pl.ds(pl.multiple_of(r0, SUBLANES), tq), :]
    for h in range(IDX_HEADS):
        qib_s[h] = qi_ref[0, :, h * IDX_DIM:(h + 1) * IDX_DIM].astype(BF16)
        wib_s[h] = jnp.broadcast_to(kiwi_t[:, IDX_DIM + h:IDX_DIM + h + 1], (tq, LANES))
    for hd in range(A_HEADS):
        qb_s[hd] = (q_ref[0, :, hd * A_HEAD_DIM:(hd + 1) * A_HEAD_DIM] * (A_HEAD_DIM ** -0.5)).astype(BF16)
        m_s[hd] = jnp.full((tq, LANES), MASK_BIAS, F32)
        acc_s[hd] = jnp.zeros((tq, LANES), F32)

    def positions(c):
        col = c * kc + lax.broadcasted_iota(I32, (tq, kc), 1)
        pos = r0 + lax.broadcasted_iota(I32, (tq, kc), 0)
        return col, pos

    def score_chunk(c, carry):
        kic = kib_s[c]
        score = jnp.zeros((tq, kc), F32)
        for h in range(IDX_HEADS):
            score = score + jnp.maximum(_dot_nt(qib_s[h], kic), 0.0) * _lane_tile(wib_s[h], nl)
        col, pos = positions(c)
        key_s[c] = jnp.where(col <= pos, _sortable_key(score), INT_MIN)
        return carry

    lax.fori_loop(0, nk, score_chunk, 0)

    def count(pred):
        def body(c, acc):
            hit = pred(key_s[c], c).astype(F32)
            for j in range(nl):
                acc = acc + hit[:, j * LANES:(j + 1) * LANES]
            return acc
        return _row_total(lax.fori_loop(0, nk, body, jnp.zeros((tq, LANES), F32)))

    t0 = jnp.where(count(lambda k, c: k >= 0) >= kf, 0, INT_MIN).astype(I32)

    def bit_step(b, t):
        cand = t + lax.shift_left(jnp.int32(1), 30 - b)
        cand_t = _lane_tile(cand, nl)
        return jnp.where(count(lambda k, c: k >= cand_t) >= kf, cand, t)

    thr = lax.fori_loop(0, 31, bit_step, t0)
    thr_t = _lane_tile(thr, nl)
    tie = (count(lambda k, c: k >= thr_t) > kf) & (thr > INT_MIN)

    ncols = nkc * kc
    nbits = max(1, (ncols - 1).bit_length())

    def break_ties():
        need = kf - count(lambda k, c: k > thr_t)

        def idx_step(b, jc):
            cand_t = _lane_tile(jc + lax.shift_left(jnp.int32(1), nbits - 1 - b), nl)
            n_lt = count(lambda k, c: (k == thr_t) & (positions(c)[0] < cand_t))
            return jnp.where(n_lt < need, cand_t[:, 0:LANES], jc)

        jc = lax.fori_loop(0, nbits, idx_step, jnp.zeros((tq, LANES), I32))
        return jnp.where(tie, jc, ncols)

    jcut = lax.cond(jnp.max(tie.astype(F32)) > 0.0, break_ties,
                    lambda: jnp.full((tq, LANES), ncols, I32))
    jcut_t = _lane_tile(jcut, nl)

    def attend_chunk(c, carry):
        keys = key_s[c]
        col, pos = positions(c)
        sel = (col <= pos) & ((keys > thr_t) | ((keys == thr_t) & (col <= jcut_t)))
        bias = jnp.where(sel, 0.0, MASK_BIAS).astype(F32)
        for hd in range(A_HEADS):
            kh = hd // A_GROUP
            logits = _dot_nt(qb_s[hd], kb_s[kh, c]) + bias
            m_old = m_s[hd]
            m_new = jnp.maximum(m_old, jnp.broadcast_to(jnp.max(logits, axis=-1, keepdims=True), (tq, LANES)))
            e = jnp.exp(logits - _lane_tile(m_new, nl))
            pv = jnp.dot(e.astype(BF16), vb_s[kh, c], preferred_element_type=F32)
            acc_s[hd] = jnp.exp(m_old - m_new) * acc_s[hd] + pv
            m_s[hd] = m_new
        return carry

    lax.fori_loop(0, nk, attend_chunk, 0)

    for hd in range(A_HEADS):
        acc = acc_s[hd]
        o_ref[0, :, hd * A_HEAD_DIM:(hd + 1) * A_HEAD_DIM] = (
            acc[:, 0:A_HEAD_DIM] / acc[:, A_HEAD_DIM:A_HEAD_DIM + 1])


def dsa_prompt(q, qi, k, v, kiwi, tq, kc):
    b, length, _ = q.shape
    assert length % tq == 0 and kc % LANES == 0
    topk = min(TOPK_MAX, length // 4)
    nkc = -(-length // kc)
    seq = lambda c: pl.BlockSpec((1, length, c), lambda i, j: (i, 0, 0))
    til = lambda c: pl.BlockSpec((1, tq, c), lambda i, j: (i, j, 0))
    return pl.pallas_call(
        functools.partial(_dsa_prompt_kernel, length=length, tq=tq, kc=kc, topk=topk),
        grid=(b, length // tq),
        in_specs=[til(q.shape[2]), til(qi.shape[2]), seq(k.shape[2]), seq(v.shape[2]), seq(kiwi.shape[2])],
        out_specs=til(q.shape[2]),
        out_shape=jax.ShapeDtypeStruct(q.shape, F32),
        scratch_shapes=[pltpu.VMEM((A_KV_HEADS, nkc, kc, A_HEAD_DIM), BF16),
                        pltpu.VMEM((A_KV_HEADS, nkc, kc, LANES), BF16),
                        pltpu.VMEM((nkc, kc, IDX_DIM), BF16),
                        pltpu.VMEM((A_HEADS, tq, A_HEAD_DIM), BF16),
                        pltpu.VMEM((IDX_HEADS, tq, IDX_DIM), BF16),
                        pltpu.VMEM((IDX_HEADS, tq, LANES), F32),
                        pltpu.VMEM((nkc, tq, kc), I32),
                        pltpu.VMEM((A_HEADS, tq, LANES), F32),
                        pltpu.VMEM((A_HEADS, tq, LANES), F32)],
        compiler_params=_cparams("parallel", "arbitrary"),
        name="dsa_prompt",
    )(q, qi, k, v, kiwi)


def _dsa_sample_score_kernel(pt_ref, qi_ref, wi_ref, kinew_ref, *rest, pp, t):
    pages = rest[:pp]
    bias_ref, key_s, jcut_s, kin_s = rest[pp:]
    j = pl.program_id(1)
    nsteps = pl.num_programs(1)
    cw = pp * PAGE_SIZE
    past = key_s.shape[1] - LANES
    qi = qi_ref[0].astype(BF16)
    wi = wi_ref[0]

    def scores(ki):
        s = jnp.maximum(_dot_nt(qi, ki), 0.0) * wi
        return jnp.sum(s.reshape(t, IDX_HEADS, ki.shape[0]), axis=1)

    ki = jnp.concatenate([p[0, 0] for p in pages], axis=0).astype(BF16)
    key_chunk = _sortable_key(scores(ki))
    for jj in range(past // cw):
        @pl.when(j == jj)
        def _(jj=jj):
            key_s[:, jj * cw:(jj + 1) * cw] = key_chunk

    @pl.when(j == nsteps - 1)
    def _():
        kin_s[...] = jnp.zeros(kin_s.shape, F32)
        kin_s[0:t, :] = kinew_ref[0]
        snew = scores(kin_s[...].astype(BF16))
        cn = lax.broadcasted_iota(I32, (t, LANES), 1)
        rn = lax.broadcasted_iota(I32, (t, LANES), 0)
        key_s[:, past:past + LANES] = jnp.where(cn <= rn, _sortable_key(snew), INT_MIN)
        ncols = past + LANES
        col = lax.broadcasted_iota(I32, (t, ncols), 1)
        pos = past + lax.broadcasted_iota(I32, (t, ncols), 0)
        vis = col <= pos
        topk = min(TOPK_MAX, (past + t) // 4)
        bias = _topk_bias(key_s, jcut_s, vis, col, topk, ncols)
        for g in range(A_GROUP):
            bias_ref[0, g * t:(g + 1) * t, :] = bias


def _dsa_sample_attn_kernel(pt_ref, q_ref, bias_ref, biasnew_ref, knew_ref, vnew_ref, *rest, pp, t):
    kpages = rest[:pp]
    vpages = rest[pp:2 * pp]
    o_ref, m_s, l_s, acc_s, new_s = rest[2 * pp:]
    j = pl.program_id(1)
    nsteps = pl.num_programs(1)

    @pl.when(j == 0)
    def _():
        m_s[...] = jnp.full(m_s.shape, MASK_BIAS, F32)
        l_s[...] = jnp.zeros(l_s.shape, F32)
        acc_s[...] = jnp.zeros(acc_s.shape, F32)

    def update(kh, kb, vb, bias):
        qh = (q_ref[0, kh] * (A_HEAD_DIM ** -0.5)).astype(BF16)
        logits = _dot_nt(qh, kb) + bias
        m_old = m_s[kh]
        m_new = jnp.maximum(m_old, jnp.max(logits, axis=-1, keepdims=True))
        alpha = jnp.exp(m_old - m_new)
        e = jnp.exp(logits - m_new)
        l_s[kh] = alpha * l_s[kh] + jnp.sum(e, axis=-1, keepdims=True)
        acc_s[kh] = alpha * acc_s[kh] + jnp.dot(e.astype(BF16), vb, preferred_element_type=F32)
        m_s[kh] = m_new

    kc = jnp.concatenate([p[0, 0] for p in kpages], axis=0).astype(BF16)
    vc = jnp.concatenate([p[0, 0] for p in vpages], axis=0).astype(BF16)
    for kh in range(A_KV_HEADS):
        sl = slice(kh * A_HEAD_DIM, (kh + 1) * A_HEAD_DIM)
        update(kh, kc[:, sl], vc[:, sl], bias_ref[0])

    @pl.when(j == nsteps - 1)
    def _():
        for kh in range(A_KV_HEADS):
            sl = slice(kh * A_HEAD_DIM, (kh + 1) * A_HEAD_DIM)
            new_s[...] = jnp.zeros(new_s.shape, F32)
            new_s[0, 0:t, :] = knew_ref[0, :, sl]
            new_s[1, 0:t, :] = vnew_ref[0, :, sl]
            update(kh, new_s[0].astype(BF16), new_s[1].astype(BF16), biasnew_ref[0])
            o_ref[0, kh] = acc_s[kh] / l_s[kh]


def dsa_sample(q_g, qi_t, wi_t, ki_new, k_new, v_new, cache_k, cache_v, cache_ki, page_flat, layer, pp):
    db, _, gt, hd = q_g.shape
    t = gt // A_GROUP
    npg = page_flat.shape[0] // db
    assert npg % pp == 0
    nsteps = npg // pp
    past = npg * PAGE_SIZE
    ncols = past + LANES
    cw = pp * PAGE_SIZE
    kvw = cache_k.shape[3]

    def page_spec(width, i):
        return pl.BlockSpec((1, 1, PAGE_SIZE, width),
                            lambda b, j, pt, i=i: (layer, pt[b * npg + j * pp + i], 0, 0))

    bias = pl.pallas_call(
        functools.partial(_dsa_sample_score_kernel, pp=pp, t=t),
        grid_spec=pltpu.PrefetchScalarGridSpec(
            num_scalar_prefetch=1,
            grid=(db, nsteps),
            in_specs=[pl.BlockSpec((1, t * IDX_HEADS, IDX_DIM), lambda b, j, pt: (b, 0, 0)),
                      pl.BlockSpec((1, t * IDX_HEADS, 1), lambda b, j, pt: (b, 0, 0)),
                      pl.BlockSpec((1, t, IDX_DIM), lambda b, j, pt: (b, 0, 0))]
                     + [page_spec(IDX_DIM, i) for i in range(pp)],
            out_specs=pl.BlockSpec((1, gt, ncols), lambda b, j, pt: (b, 0, 0)),
            scratch_shapes=[pltpu.VMEM((t, ncols), I32),
                            pltpu.VMEM((t, 1), I32),
                            pltpu.VMEM((LANES, IDX_DIM), F32)]),
        out_shape=jax.ShapeDtypeStruct((db, gt, ncols), F32),
        compiler_params=_cparams("parallel", "arbitrary"),
        name="dsa_sample_score",
    )(page_flat, qi_t, wi_t, ki_new, *([cache_ki] * pp))

    out = pl.pallas_call(
        functools.partial(_dsa_sample_attn_kernel, pp=pp, t=t),
        grid_spec=pltpu.PrefetchScalarGridSpec(
            num_scalar_prefetch=1,
            grid=(db, nsteps),
            in_specs=[pl.BlockSpec((1, A_KV_HEADS, gt, hd), lambda b, j, pt: (b, 0, 0, 0)),
                      pl.BlockSpec((1, gt, cw), lambda b, j, pt: (b, 0, j)),
                      pl.BlockSpec((1, gt, LANES), lambda b, j, pt: (b, 0, past // LANES)),
                      pl.BlockSpec((1, t, kvw), lambda b, j, pt: (b, 0, 0)),
                      pl.BlockSpec((1, t, kvw), lambda b, j, pt: (b, 0, 0))]
                     + [page_spec(kvw, i) for i in range(pp)]
                     + [page_spec(kvw, i) for i in range(pp)],
            out_specs=pl.BlockSpec((1, A_KV_HEADS, gt, hd), lambda b, j, pt: (b, 0, 0, 0)),
            scratch_shapes=[pltpu.VMEM((A_KV_HEADS, gt, 1), F32),
                            pltpu.VMEM((A_KV_HEADS, gt, 1), F32),
                            pltpu.VMEM((A_KV_HEADS, gt, hd), F32),
                            pltpu.VMEM((2, LANES, hd), F32)]),
        out_shape=jax.ShapeDtypeStruct((db, A_KV_HEADS, gt, hd), F32),
        compiler_params=_cparams("parallel", "arbitrary"),
        name="dsa_sample_attn",
    )(page_flat, q_g, bias, bias, k_new, v_new, *([cache_k] * pp), *([cache_v] * pp))
    return out


def _gla_kernel(*refs, tb, tv, chunk, sub, has_s0):
    if has_s0:
        q_ref, k_ref, v_ref, g_ref, gd_ref, wgu_ref, bg_ref, gn_ref, s0_ref, on_ref, sout_ref, la_s, st_s = refs
    else:
        q_ref, k_ref, v_ref, g_ref, gd_ref, wgu_ref, bg_ref, gn_ref, on_ref, sout_ref, la_s, st_s = refs
    jb = pl.program_id(1)
    nb = pl.num_programs(1)

    @pl.when(jb == 0)
    def _():
        for h in range(GLA_HEADS):
            if has_s0:
                st_s[h] = s0_ref[0, h].T
            else:
                st_s[h] = jnp.zeros((GLA_DV, GLA_DK), F32)

    x = jnp.dot(gd_ref[0], wgu_ref[...], precision=HIGHEST, preferred_element_type=F32) + bg_ref[...]
    la = (jnp.minimum(x, 0.0) - jnp.log1p(jnp.exp(-jnp.abs(x)))) * (1.0 / GLA_GATE_NORM)
    if tv is not None:
        row = jb * tb + lax.broadcasted_iota(I32, (tb, 1), 0)
        la = jnp.where(row < tv, la, 0.0)
    la_s[...] = la

    def do_chunk(r0, c):
        rows = pl.ds(r0, c)
        ri = lax.broadcasted_iota(I32, (c, c), 0)
        ci = lax.broadcasted_iota(I32, (c, c), 1)
        tri = (ri >= ci).astype(F32)
        b = jnp.dot(tri, la_s[rows, :], precision=HIGHEST, preferred_element_type=F32)
        qc = q_ref[0, rows, :] * (GLA_DK ** -0.5)
        kc = k_ref[0, rows, :]
        if tv is not None:
            rowc = jb * tb + r0 + lax.broadcasted_iota(I32, (c, 1), 0)
            kc = jnp.where(rowc < tv, kc, 0.0)
        vc = v_ref[0, rows, :].astype(BF16)
        gc = g_ref[0, rows, :]
        for h in range(GLA_HEADS):
            sk = slice(h * GLA_DK, (h + 1) * GLA_DK)
            sv = slice(h * GLA_DV, (h + 1) * GLA_DV)
            bh, qh, kh, vh = b[:, sk], qc[:, sk], kc[:, sk], vc[:, sv]
            st = st_s[h]
            o_inter = _dot_nt((qh * jnp.exp(bh)).astype(BF16), st.astype(BF16))
            parts = []
            for i in range(c // sub):
                lo, hi = i * sub, (i + 1) * sub
                ref_b = bh[lo - 1:lo, :] if i > 0 else jnp.zeros((1, GLA_DK), F32)
                qt = (qh[lo:hi] * jnp.exp(bh[lo:hi] - ref_b)).astype(BF16)
                kt = (kh[0:hi] * jnp.exp(ref_b - bh[0:hi])).astype(BF16)
                att = _dot_nt(qt, kt)
                causal = (lax.broadcasted_iota(I32, (sub, hi), 1)
                          <= lo + lax.broadcasted_iota(I32, (sub, hi), 0))
                att = jnp.where(causal, att, 0.0).astype(BF16)
                parts.append(jnp.dot(att, vh[0:hi], preferred_element_type=F32))
            o = o_inter + (jnp.concatenate(parts, axis=0) if len(parts) > 1 else parts[0])
            bl = bh[c - 1:c, :]
            kd = (kh * jnp.exp(bl - bh)).astype(BF16)
            st_s[h] = jnp.exp(bl) * st + _dot_tn(vh, kd)
            on = _rms(o, gn_ref[...])
            gh = gc[:, sv]
            on_ref[0, rows, sv] = (on * (gh * jax.nn.sigmoid(gh))).astype(on_ref.dtype)

    nfull, rem = tb // chunk, tb % chunk
    if nfull == 1:
        do_chunk(0, chunk)
    elif nfull > 1:
        def body(ic, carry):
            do_chunk(pl.multiple_of(ic * chunk, BF16_ROWS), chunk)
            return carry
        lax.fori_loop(0, nfull, body, 0)
    if rem:
        do_chunk(nfull * chunk, rem)

    @pl.when(jb == nb - 1)
    def _():
        for h in range(GLA_HEADS):
            sout_ref[0, h] = st_s[h].T


def gla_mixer(q, k, v, g, gd, wgu, bg, gn, s0, tb, chunk, tv):
    b, t, _ = q.shape
    sub = BF16_ROWS
    assert t % tb == 0 and tb % sub == 0 and chunk % sub == 0
    blk = lambda c: pl.BlockSpec((1, tb, c), lambda i, j: (i, j, 0))
    cst = lambda a: pl.BlockSpec(a.shape, lambda i, j: (0,) * a.ndim)
    st_spec = pl.BlockSpec((1, GLA_HEADS, GLA_DK, GLA_DV), lambda i, j: (i, 0, 0, 0))
    bg2, gn2 = bg.reshape(1, -1), gn.reshape(1, -1)
    args = [q, k, v, g, gd, wgu, bg2, gn2]
    in_specs = [blk(q.shape[2]), blk(k.shape[2]), blk(v.shape[2]), blk(g.shape[2]), blk(gd.shape[2]),
                cst(wgu), cst(bg2), cst(gn2)]
    if s0 is not None:
        args.append(s0)
        in_specs.append(st_spec)
    return pl.pallas_call(
        functools.partial(_gla_kernel, tb=tb, tv=tv, chunk=chunk, sub=sub, has_s0=s0 is not None),
        grid=(b, t // tb),
        in_specs=in_specs,
        out_specs=[blk(v.shape[2]), st_spec],
        out_shape=[jax.ShapeDtypeStruct(v.shape, BF16),
                   jax.ShapeDtypeStruct((b, GLA_HEADS, GLA_DK, GLA_DV), F32)],
        scratch_shapes=[pltpu.VMEM((tb, GLA_HEADS * GLA_DK), F32),
                        pltpu.VMEM((GLA_HEADS, GLA_DV, GLA_DK), F32)],
        compiler_params=_cparams("parallel", "arbitrary"),
        name="gla_mixer",
    )(*args)


def _pick_tile(n, candidates):
    for c in candidates:
        if n % c == 0:
            return c
    return n


def kernel(x_prompt, x_sample, cache_k, cache_v, cache_kidx, state_pool, state_gla, page_table,
           meta_tokens, norm_mix, norm_ffn, norm_final, w_in_even, w_out_even, pool_w, pool_scale,
           w_in_odd, gla_w_gate_up, gla_b_gate, gla_norm, w_out_odd, ffn_w_gate, ffn_w_up, ffn_w_down):
    bsz, seq, d = x_prompt.shape
    db, dseq, _ = x_sample.shape
    depth = norm_mix.shape[0]
    length = seq + N_META
    n_even = cache_k.shape[0]
    n_pool = cache_k.shape[1]
    npg = page_table.shape[1]
    past = npg * PAGE_SIZE
    a_width = A_HEADS * A_HEAD_DIM
    kv_width = A_KV_HEADS * A_HEAD_DIM
    pool_width = pool_scale.shape[1]
    gk = GLA_HEADS * GLA_DK
    gv = GLA_HEADS * GLA_DV
    d_ff = ffn_w_gate.shape[2]

    meta = jnp.broadcast_to(meta_tokens[None], (bsz, N_META, d))
    xp = jnp.concatenate([meta, x_prompt], axis=1).reshape(bsz * length, d)
    xs = x_sample.reshape(db * dseq, d)
    n_p, n_s = xp.shape[0], xs.shape[0]

    tm_p = _pick_tile(n_p, (688, 512, 256, 128, 64, 32, 16, 8))
    tm_s = _pick_tile(n_s, (128, 64, 32, 16, 8))
    tf = _pick_tile(d_ff, (1408, 1024, 512, 256, 128))
    tq = _pick_tile(length, (344, 256, 128, 64, 32, 16, 8))
    pool_rc = _pick_tile(length, (48, 32, 16, 8))
    gla_tb = _pick_tile(length, (688, 512, 256, 128, 64, 32, 16))
    pp = _pick_tile(npg, (16, 8, 4, 2, 1))

    e_off = [0]
    for s in (a_width, kv_width, kv_width, IDX_HEADS * IDX_DIM, IDX_DIM, IDX_HEADS, pool_width):
        e_off.append(e_off[-1] + s)
    kiwi_pad = LANES - IDX_DIM - IDX_HEADS
    even_widths = (a_width, kv_width, kv_width, IDX_HEADS * IDX_DIM, pool_width, LANES)
    o_off = [0]
    for s in (gk, gk, gv, gv, GLA_GATE_RANK):
        o_off.append(o_off[-1] + s)
    odd_widths = (gk, gk, gv, gv, LANES)

    cache_k4 = cache_k.reshape(n_even, n_pool, PAGE_SIZE, kv_width)
    cache_v4 = cache_v.reshape(n_even, n_pool, PAGE_SIZE, kv_width)
    page_flat = page_table.reshape(-1).astype(I32)

    kp_l, vp_l, kip_l, bp_l, ks_l, vs_l, kis_l, bs_l, sp_l, ss_l = ([] for _ in range(10))
    for l in range(depth):
        if l % 2 == 0:
            e = l // 2
            w = w_in_even[e]
            w_in = jnp.concatenate(
                [w[:, e_off[0]:e_off[4]], w[:, e_off[6]:e_off[7]], w[:, e_off[4]:e_off[6]],
                 jnp.zeros((d, kiwi_pad), w.dtype)], axis=1).astype(BF16)
            w_out = w_out_even[e].astype(BF16)
            pw = pool_w[e].astype(BF16)

            q, k, v, qi, u, kiwi = norm_matmul(xp, norm_mix[l], w_in, even_widths, tm_p)
            r3 = lambda a: a.reshape(bsz, length, a.shape[1])
            oa = dsa_prompt(r3(q), r3(qi), r3(k), r3(v), r3(kiwi), tq, DSA_KEY_CHUNK)
            ob, buf = pool_mixer(r3(u), jnp.zeros((bsz, POOL_BUF + 1, pool_width), F32), pw,
                                 pool_scale[e], 0, pool_rc, BF16)
            mix_p = [(oa.reshape(n_p, a_width), w_out[:a_width]),
                     (ob.reshape(n_p, pool_width), w_out[a_width:])]
            kp_l.append(k.reshape(bsz, length, A_KV_HEADS, A_HEAD_DIM))
            vp_l.append(v.reshape(bsz, length, A_KV_HEADS, A_HEAD_DIM))
            kip_l.append(kiwi[:, :IDX_DIM].reshape(bsz, length, IDX_DIM))
            bp_l.append(buf)

            q, k, v, qi, u, kiwi = norm_matmul(xs, norm_mix[l], w_in, even_widths, tm_s)
            s3 = lambda a: a.reshape(db, dseq, a.shape[1])
            q_g = q.reshape(db, dseq, A_KV_HEADS, A_GROUP, A_HEAD_DIM).transpose(0, 2, 3, 1, 4)
            q_g = q_g.reshape(db, A_KV_HEADS, A_GROUP * dseq, A_HEAD_DIM)
            qi_t = qi.reshape(db, dseq * IDX_HEADS, IDX_DIM)
            wi_t = kiwi[:, IDX_DIM:IDX_DIM + IDX_HEADS].reshape(db, dseq * IDX_HEADS, 1)
            ki_new = kiwi[:, :IDX_DIM].reshape(db, dseq, IDX_DIM)
            oa = dsa_sample(q_g, qi_t, wi_t, ki_new, s3(k), s3(v), cache_k4, cache_v4, cache_kidx,
                            page_flat, e, pp)
            oa = oa.reshape(db, A_KV_HEADS, A_GROUP, dseq, A_HEAD_DIM).transpose(0, 3, 1, 2, 4)
            prefix = jnp.pad(state_pool[e], ((0, 0), (1, 0), (0, 0)))
            ob, buf = pool_mixer(s3(u), prefix, pw, pool_scale[e], past, BF16_ROWS, F32)
            mix_s = [(oa.reshape(n_s, a_width), w_out[:a_width]),
                     (ob.reshape(n_s, pool_width), w_out[a_width:])]
            ks_l.append(k.reshape(db, dseq, A_KV_HEADS, A_HEAD_DIM))
            vs_l.append(v.reshape(db, dseq, A_KV_HEADS, A_HEAD_DIM))
            kis_l.append(ki_new)
            bs_l.append(buf)
        else:
            o = l // 2
            w = w_in_odd[o]
            w_in = jnp.concatenate([w, jnp.zeros((d, LANES - GLA_GATE_RANK), w.dtype)], axis=1).astype(BF16)
            w_out = w_out_odd[o].astype(BF16)
            wgu = jnp.pad(gla_w_gate_up[o], ((0, LANES - GLA_GATE_RANK), (0, 0)))

            q, k, v, g, gd = norm_matmul(xp, norm_mix[l], w_in, odd_widths, tm_p)
            r3 = lambda a: a.reshape(bsz, length, a.shape[1])
            on, s_p = gla_mixer(r3(q), r3(k), r3(v), r3(g), r3(gd), wgu, gla_b_gate[o], gla_norm[o],
                                None, gla_tb, 48, None)
            mix_p = [(on.reshape(n_p, gv), w_out)]
            sp_l.append(s_p)

            q, k, v, g, gd = norm_matmul(xs, norm_mix[l], w_in, odd_widths, tm_s)
            tpad = _round_up(dseq, BF16_ROWS)
            p3 = lambda a: jnp.pad(a.reshape(db, dseq, a.shape[1]), ((0, 0), (0, tpad - dseq), (0, 0)))
            on, s_s = gla_mixer(p3(q), p3(k), p3(v), p3(g), p3(gd), wgu, gla_b_gate[o], gla_norm[o],
                                state_gla[o], tpad, tpad, dseq)
            mix_s = [(on[:, :dseq].reshape(n_s, gv), w_out)]
            ss_l.append(s_s)

        wg, wu, wd = ffn_w_gate[l].astype(BF16), ffn_w_up[l].astype(BF16), ffn_w_down[l].astype(BF16)
        xp = mix_ffn(xp, mix_p, norm_ffn[l], wg, wu, wd, tm_p, tf)
        xs = mix_ffn(xs, mix_s, norm_ffn[l], wg, wu, wd, tm_s, tf)

    y_prompt = final_norm(xp.reshape(bsz, length, d), norm_final, N_META)
    y_sample = final_norm(xs.reshape(1, n_s, d), norm_final, 0).reshape(db, dseq, d)
    return (y_prompt, y_sample,
            jnp.stack(kp_l), jnp.stack(vp_l), jnp.stack(kip_l),
            jnp.stack(ks_l), jnp.stack(vs_l), jnp.stack(kis_l),
            jnp.stack(bp_l), jnp.stack(bs_l),
            jnp.stack(sp_l), jnp.stack(ss_l))
```

```python
import functools

import jax
import jax.numpy as jnp
from jax import lax
from jax.experimental import pallas as pl
from jax.experimental.pallas import tpu as pltpu

F32 = jnp.float32
BF16 = jnp.bfloat16
I32 = jnp.int32

NORM_EPS = 1e-6
N_META = 16
A_HEADS = 8
A_KV_HEADS = 2
A_GROUP = A_HEADS // A_KV_HEADS
A_HEAD_DIM = 64
IDX_HEADS = 8
IDX_DIM = 64
TOPK_MAX = 256
PAGE_SIZE = 128
POOL_WINDOWS = (2, 4, 8, 16)
POOL_GROUP_DIM = 128
POOL_BUF = 15
GLA_HEADS = 4
GLA_DK = 128
GLA_DV = 256
GLA_GATE_RANK = 16
GLA_GATE_NORM = 16.0

LANES = 128
SUBLANES = 8
BF16_ROWS = 16
VMEM_LIMIT_BYTES = 56 * 1024 * 1024

DSA_KEY_CHUNK = 2 * LANES
GLA_CHUNK = 96

INT_MIN = -(2 ** 31)
MASK_BIAS = -1e30
HIGHEST = lax.Precision.HIGHEST
QK_SCALE_LOG2 = (A_HEAD_DIM ** -0.5) * 1.4426950408889634


def _round_up(n, m):
    return (n + m - 1) // m * m


def _cparams(*sem):
    return pltpu.CompilerParams(dimension_semantics=sem, vmem_limit_bytes=VMEM_LIMIT_BYTES)


def _rms(x, g):
    ms = jnp.mean(x * x, axis=-1, keepdims=True)
    return (x * lax.rsqrt(ms + NORM_EPS)) * g


def _dot_nt(a, b):
    return lax.dot_general(a, b, (((1,), (1,)), ((), ())), preferred_element_type=F32)


def _dot_tn(a, b):
    return lax.dot_general(a, b, (((0,), (0,)), ((), ())), preferred_element_type=F32)


def _norm_matmul_kernel(x_ref, g_ref, w_ref, *o_refs, widths):
    h = _rms(x_ref[...], g_ref[...]).astype(BF16)
    off = 0
    for o_ref, wd in zip(o_refs, widths):
        o_ref[...] = jnp.dot(h, w_ref[:, off:off + wd], preferred_element_type=F32)
        off += wd


def norm_matmul(x, g, w, widths, tm):
    n, d = x.shape
    assert n % tm == 0 and sum(widths) == w.shape[1]
    return pl.pallas_call(
        functools.partial(_norm_matmul_kernel, widths=widths),
        grid=(n // tm,),
        in_specs=[pl.BlockSpec((tm, d), lambda i: (i, 0)),
                  pl.BlockSpec((1, d), lambda i: (0, 0)),
                  pl.BlockSpec((d, w.shape[1]), lambda i: (0, 0))],
        out_specs=[pl.BlockSpec((tm, wd), lambda i: (i, 0)) for wd in widths],
        out_shape=[jax.ShapeDtypeStruct((n, wd), F32) for wd in widths],
        compiler_params=_cparams("parallel"),
        name="norm_matmul",
    )(x, g.reshape(1, d), w)


def _mix_ffn_kernel(*refs, n_mix):
    x_ref = refs[0]
    mix = refs[1:1 + 2 * n_mix]
    g_ref, wg_ref, wu_ref, wd_ref, o_ref, h_s = refs[1 + 2 * n_mix:]
    j = pl.program_id(1)

    @pl.when(j == 0)
    def _():
        xm = x_ref[...]
        for m in range(n_mix):
            a_ref, w_ref = mix[2 * m], mix[2 * m + 1]
            xm = xm + jnp.dot(a_ref[...].astype(BF16), w_ref[...], preferred_element_type=F32)
        o_ref[...] = xm
        h_s[...] = _rms(xm, g_ref[...]).astype(BF16)

    h = h_s[...]
    gate = jnp.dot(h, wg_ref[...], preferred_element_type=F32)
    up = jnp.dot(h, wu_ref[...], preferred_element_type=F32)
    act = (gate * jax.nn.sigmoid(gate) * up).astype(BF16)
    o_ref[...] += jnp.dot(act, wd_ref[...], preferred_element_type=F32)


def mix_ffn(x, mixes, g, wg, wu, wd, tm, tf):
    n, d = x.shape
    f = wg.shape[1]
    assert n % tm == 0 and f % tf == 0
    in_specs = [pl.BlockSpec((tm, d), lambda i, j: (i, 0))]
    args = [x]
    for a, w in mixes:
        in_specs += [pl.BlockSpec((tm, a.shape[1]), lambda i, j: (i, 0)),
                     pl.BlockSpec(w.shape, lambda i, j: (0, 0))]
        args += [a, w]
    in_specs += [pl.BlockSpec((1, d), lambda i, j: (0, 0)),
                 pl.BlockSpec((d, tf), lambda i, j: (0, j)),
                 pl.BlockSpec((d, tf), lambda i, j: (0, j)),
                 pl.BlockSpec((tf, d), lambda i, j: (j, 0))]
    args += [g.reshape(1, d), wg, wu, wd]
    return pl.pallas_call(
        functools.partial(_mix_ffn_kernel, n_mix=len(mixes)),
        grid=(n // tm, f // tf),
        in_specs=in_specs,
        out_specs=pl.BlockSpec((tm, d), lambda i, j: (i, 0)),
        out_shape=jax.ShapeDtypeStruct((n, d), F32),
        scratch_shapes=[pltpu.VMEM((tm, d), BF16)],
        compiler_params=_cparams("parallel", "arbitrary"),
        name="mix_ffn",
    )(*args)


def _final_norm_kernel(x_ref, g_ref, o_ref, *, skip, rows, chunk):
    for c in range(rows // chunk):
        x = x_ref[0, skip + c * chunk: skip + (c + 1) * chunk, :]
        o_ref[0, c * chunk:(c + 1) * chunk, :] = _rms(x, g_ref[...])


def final_norm(x, g, skip):
    b, t, d = x.shape
    rows = t - skip
    chunk = 256 if rows % 256 == 0 else rows
    return pl.pallas_call(
        functools.partial(_final_norm_kernel, skip=skip, rows=rows, chunk=chunk),
        grid=(b,),
        in_specs=[pl.BlockSpec((1, t, d), lambda i: (i, 0, 0)),
                  pl.BlockSpec((1, d), lambda i: (0, 0))],
        out_specs=pl.BlockSpec((1, rows, d), lambda i: (i, 0, 0)),
        out_shape=jax.ShapeDtypeStruct((b, rows, d), F32),
        compiler_params=_cparams("parallel"),
        name="final_norm",
    )(x, g.reshape(1, d))


def _pool_kernel(u_ref, pre_ref, pw_ref, sc_ref, ob_ref, buf_ref, ext_s, *, t, pos0, rc):
    head = POOL_BUF + 1
    tproc = _round_up(t, rc)
    ext_s[0:head, :] = pre_ref[0]
    ext_s[head:head + t, :] = u_ref[0]
    if tproc > t:
        ext_s[head + t:head + tproc, :] = jnp.zeros((tproc - t, ext_s.shape[1]), F32)
    buf_ref[0] = ext_s[t + 1:t + head, :]
    for c in range(tproc // rc):
        r0 = c * rc
        nst = min(rc, t - r0)
        pos = pos0 + r0 + lax.broadcasted_iota(I32, (rc, 1), 0)
        for gi, w in enumerate(POOL_WINDOWS):
            sl = slice(gi * POOL_GROUP_DIM, (gi + 1) * POOL_GROUP_DIM)
            xt = ext_s[head + r0:head + r0 + rc, sl]
            acc = xt
            for r in range(1, w):
                acc = acc + ext_s[head + r0 - r:head + r0 - r + rc, sl]
            cnt = jnp.minimum(w, pos + 1).astype(F32)
            dlt = acc / cnt - xt
            y = jnp.dot(dlt.astype(BF16), pw_ref[gi], preferred_element_type=F32) * sc_ref[:, sl]
            ob_ref[0, r0:r0 + nst, sl] = y[0:nst].astype(ob_ref.dtype)


def pool_mixer(u, prefix16, pw, scale, pos0, rc, out_dtype):
    b, t, c = u.shape
    head = POOL_BUF + 1
    return pl.pallas_call(
        functools.partial(_pool_kernel, t=t, pos0=pos0, rc=rc),
        grid=(b,),
        in_specs=[pl.BlockSpec((1, t, c), lambda i: (i, 0, 0)),
                  pl.BlockSpec((1, head, c), lambda i: (i, 0, 0)),
                  pl.BlockSpec(pw.shape, lambda i: (0, 0, 0)),
                  pl.BlockSpec((1, c), lambda i: (0, 0))],
        out_specs=[pl.BlockSpec((1, t, c), lambda i: (i, 0, 0)),
                   pl.BlockSpec((1, POOL_BUF, c), lambda i: (i, 0, 0))],
        out_shape=[jax.ShapeDtypeStruct((b, t, c), out_dtype),
                   jax.ShapeDtypeStruct((b, POOL_BUF, c), F32)],
        scratch_shapes=[pltpu.VMEM((head + _round_up(t, rc), c), F32)],
        compiler_params=_cparams("parallel"),
        name="pool_mixer",
    )(u, prefix16, pw, scale.reshape(1, c))


def _sortable_key(score):
    bits = lax.bitcast_convert_type(score, I32)
    return bits ^ ((bits >> 31) & 0x7FFFFFFF)


def _count(mask):
    return jnp.sum(mask.astype(F32), axis=-1, keepdims=True)


def _topk_bias(key_s, jcut_s, vis, col, kk, ncols):
    kf = float(kk)
    t0 = jnp.where(_count(key_s[...] >= 0) >= kf, 0, INT_MIN).astype(I32)

    def bit_step(it, t):
        cand = t + lax.shift_left(jnp.int32(1), 30 - it)
        return jnp.where(_count(key_s[...] >= cand) >= kf, cand, t)

    thr = lax.fori_loop(0, 31, bit_step, t0)
    keys = key_s[...]
    n_ge = _count(keys >= thr)
    tie = (n_ge > kf) & (thr > INT_MIN)
    jcut_s[...] = jnp.full(jcut_s.shape, ncols, I32)

    @pl.when(jnp.max(tie.astype(F32)) > 0.0)
    def _():
        need = kf - _count(key_s[...] > thr)
        nbits = max(1, (ncols - 1).bit_length())

        def idx_step(it, jc):
            cand = jc + lax.shift_left(jnp.int32(1), nbits - 1 - it)
            n_lt = _count((key_s[...] == thr) & (col < cand))
            return jnp.where(n_lt < need, cand, jc)

        jc = lax.fori_loop(0, nbits, idx_step, jnp.zeros(thr.shape, I32))
        jcut_s[...] = jnp.where(tie, jc, ncols)

    sel = vis & ((keys > thr) | ((keys == thr) & (col <= jcut_s[...])))
    return jnp.where(sel, 0.0, MASK_BIAS).astype(F32)


def _lane_tile(x, n):
    return x if n == 1 else jnp.concatenate([x] * n, axis=1)


def _row_total(x):
    return jnp.broadcast_to(jnp.sum(x, axis=-1, keepdims=True), x.shape)


def _dsa_prompt_kernel(q_ref, qi_ref, k_ref, v_ref, kiwi_ref, o_ref,
                       kb_s, vb_s, kib_s, qb_s, qib_s, wib_s, key_s, keyt_s, m_s, acc_s,
                       *, length, tq, kc, topk):
    nkc = kib_s.shape[0]
    nl = kc // LANES
    tqp = keyt_s.shape[2]
    tqb = qb_s.shape[1] // A_GROUP
    it = pl.program_id(1)
    r0 = it * tq
    nk = (r0 + tq + kc - 1) // kc
    kf = float(topk)

    @pl.when(it == 0)
    def _():
        ones_col = (lax.broadcasted_iota(I32, (kc, LANES - A_HEAD_DIM), 1) == 0).astype(BF16)
        for c in range(nkc):
            lo = c * kc
            valid = max(0, min(kc, length - lo))
            for kh in range(A_KV_HEADS):
                sl = slice(kh * A_HEAD_DIM, (kh + 1) * A_HEAD_DIM)
                if valid < kc:
                    kb_s[kh, c] = jnp.zeros((kc, A_HEAD_DIM), BF16)
                    vb_s[kh, c, :, 0:A_HEAD_DIM] = jnp.zeros((kc, A_HEAD_DIM), BF16)
                if valid:
                    kb_s[kh, c, 0:valid, :] = k_ref[0, lo:lo + valid, sl].astype(BF16)
                    vb_s[kh, c, 0:valid, 0:A_HEAD_DIM] = v_ref[0, lo:lo + valid, sl].astype(BF16)
                vb_s[kh, c, :, A_HEAD_DIM:LANES] = ones_col
            if valid < kc:
                kib_s[c] = jnp.zeros((kc, IDX_DIM), BF16)
            if valid:
                kib_s[c, 0:valid, :] = kiwi_ref[0, lo:lo + valid, 0:IDX_DIM].astype(BF16)

    def pad_rows(x, n):
        return x if x.shape[0] == n else jnp.concatenate([x, jnp.zeros((n - x.shape[0], x.shape[1]), x.dtype)], axis=0)

    kiwi_t = pad_rows(kiwi_ref[0, pl.ds(pl.multiple_of(r0, SUBLANES), tq), :], tqp)
    for h in range(IDX_HEADS):
        qih = pad_rows(qi_ref[0, :, h * IDX_DIM:(h + 1) * IDX_DIM], tqp)
        qib_s[h * tqp:(h + 1) * tqp, :] = qih.astype(BF16)
        wib_s[h] = jnp.broadcast_to(kiwi_t[:, IDX_DIM + h:IDX_DIM + h + 1], (tqp, LANES))
    for hd in range(A_HEADS):
        kh, g = divmod(hd, A_GROUP)
        qh = pad_rows(q_ref[0, :, hd * A_HEAD_DIM:(hd + 1) * A_HEAD_DIM] * QK_SCALE_LOG2, tqb)
        qb_s[kh, g * tqb:(g + 1) * tqb, :] = qh.astype(BF16)
        m_s[hd] = jnp.full((tq, LANES), MASK_BIAS, F32)
        acc_s[hd] = jnp.zeros((tq, LANES), F32)

    def score_chunk(c, carry):
        s_all = _dot_nt(qib_s[...], kib_s[c])
        score = jnp.zeros((tqp, kc), F32)
        for h in range(IDX_HEADS):
            score = score + jnp.maximum(s_all[h * tqp:(h + 1) * tqp], 0.0) * _lane_tile(wib_s[h], nl)
        col = c * kc + lax.broadcasted_iota(I32, (tqp, kc), 1)
        pos = r0 + lax.broadcasted_iota(I32, (tqp, kc), 0)
        key = jnp.where(col <= pos, _sortable_key(score), INT_MIN)
        key_s[c] = key
        keyt_s[c] = key.T
        return carry

    lax.fori_loop(0, nk, score_chunk, 0)

    def count(pred):
        def body(c, acc):
            hit = pred(keyt_s[c], c).astype(F32)
            return acc + jnp.sum(hit.reshape(kc // SUBLANES, SUBLANES, tqp), axis=0)
        acc = lax.fori_loop(0, nk, body, jnp.zeros((SUBLANES, tqp), F32))
        return jnp.sum(acc, axis=0, keepdims=True)

    t0 = jnp.where(count(lambda k, c: k >= 0) >= kf, 0, INT_MIN).astype(I32)

    def bit_step(b, t):
        cand = t + lax.shift_left(jnp.int32(1), 30 - b)
        return jnp.where(count(lambda k, c: k >= cand) >= kf, cand, t)

    thr_q = lax.fori_loop(0, 31, bit_step, t0)
    tie_q = (count(lambda k, c: k >= thr_q) > kf) & (thr_q > INT_MIN)

    ncols = nkc * kc
    nbits = max(1, (ncols - 1).bit_length())

    def break_ties():
        need = kf - count(lambda k, c: k > thr_q)
        kidx = lax.broadcasted_iota(I32, (kc, tqp), 0)

        def idx_step(b, jc):
            cand = jc + lax.shift_left(jnp.int32(1), nbits - 1 - b)
            n_lt = count(lambda k, c: (k == thr_q) & (c * kc + kidx < cand))
            return jnp.where(n_lt < need, cand, jc)

        jc = lax.fori_loop(0, nbits, idx_step, jnp.zeros((1, tqp), I32))
        return jnp.where(tie_q, jc, ncols)

    jcut_q = lax.cond(jnp.max(tie_q.astype(F32)) > 0.0, break_ties,
                      lambda: jnp.full((1, tqp), ncols, I32))

    def per_row(x_q):
        return jnp.broadcast_to(x_q, (LANES, tqp)).T[0:tq]

    thr_t = _lane_tile(per_row(thr_q), nl)
    jcut_t = _lane_tile(per_row(jcut_q), nl)

    def attend_chunk(c, carry):
        keys = key_s[c, 0:tq, :]
        col = c * kc + lax.broadcasted_iota(I32, (tq, kc), 1)
        pos = r0 + lax.broadcasted_iota(I32, (tq, kc), 0)
        sel = (col <= pos) & ((keys > thr_t) | ((keys == thr_t) & (col <= jcut_t)))
        bias = jnp.where(sel, 0.0, MASK_BIAS).astype(F32)
        for kh in range(A_KV_HEADS):
            l_all = _dot_nt(qb_s[kh], kb_s[kh, c])
            es, stats = [], []
            for g in range(A_GROUP):
                hd = kh * A_GROUP + g
                logits = l_all[g * tqb:g * tqb + tq] + bias
                m_old = m_s[hd]
                m_new = jnp.maximum(m_old, jnp.broadcast_to(jnp.max(logits, axis=-1, keepdims=True), (tq, LANES)))
                es.append(pad_rows(jnp.exp2(logits - _lane_tile(m_new, nl)), tqb).astype(BF16))
                stats.append((hd, m_old, m_new))
            pv_all = jnp.dot(jnp.concatenate(es, axis=0), vb_s[kh, c], preferred_element_type=F32)
            for g, (hd, m_old, m_new) in enumerate(stats):
                acc_s[hd] = jnp.exp2(m_old - m_new) * acc_s[hd] + pv_all[g * tqb:g * tqb + tq]
                m_s[hd] = m_new
        return carry

    lax.fori_loop(0, nk, attend_chunk, 0)

    for hd in range(A_HEADS):
        acc = acc_s[hd]
        o_ref[0, :, hd * A_HEAD_DIM:(hd + 1) * A_HEAD_DIM] = (
            acc[:, 0:A_HEAD_DIM] / acc[:, A_HEAD_DIM:A_HEAD_DIM + 1])


def dsa_prompt(q, qi, k, v, kiwi, tq, kc):
    b, length, _ = q.shape
    assert length % tq == 0 and kc % LANES == 0
    topk = min(TOPK_MAX, length // 4)
    nkc = -(-length // kc)
    tqp = _round_up(tq, LANES)
    tqb = _round_up(tq, BF16_ROWS)
    seq = lambda c: pl.BlockSpec((1, length, c), lambda i, j: (i, 0, 0))
    til = lambda c: pl.BlockSpec((1, tq, c), lambda i, j: (i, j, 0))
    return pl.pallas_call(
        functools.partial(_dsa_prompt_kernel, length=length, tq=tq, kc=kc, topk=topk),
        grid=(b, length // tq),
        in_specs=[til(q.shape[2]), til(qi.shape[2]), seq(k.shape[2]), seq(v.shape[2]), seq(kiwi.shape[2])],
        out_specs=til(q.shape[2]),
        out_shape=jax.ShapeDtypeStruct(q.shape, F32),
        scratch_shapes=[pltpu.VMEM((A_KV_HEADS, nkc, kc, A_HEAD_DIM), BF16),
                        pltpu.VMEM((A_KV_HEADS, nkc, kc, LANES), BF16),
                        pltpu.VMEM((nkc, kc, IDX_DIM), BF16),
                        pltpu.VMEM((A_KV_HEADS, A_GROUP * tqb, A_HEAD_DIM), BF16),
                        pltpu.VMEM((IDX_HEADS * tqp, IDX_DIM), BF16),
                        pltpu.VMEM((IDX_HEADS, tqp, LANES), F32),
                        pltpu.VMEM((nkc, tqp, kc), I32),
                        pltpu.VMEM((nkc, kc, tqp), I32),
                        pltpu.VMEM((A_HEADS, tq, LANES), F32),
                        pltpu.VMEM((A_HEADS, tq, LANES), F32)],
        compiler_params=_cparams("parallel", "arbitrary"),
        name="dsa_prompt",
    )(q, qi, k, v, kiwi)


def _dsa_sample_score_kernel(pt_ref, qi_ref, wi_ref, kinew_ref, *rest, pp, t):
    pages = rest[:pp]
    bias_ref, key_s, jcut_s, kin_s = rest[pp:]
    j = pl.program_id(1)
    nsteps = pl.num_programs(1)
    cw = pp * PAGE_SIZE
    past = key_s.shape[1] - LANES
    qi = qi_ref[0].astype(BF16)
    wi = wi_ref[0]

    def scores(s):
        s = jnp.maximum(s, 0.0) * wi
        return jnp.sum(s.reshape(t, IDX_HEADS, s.shape[1]), axis=1)

    ki_t = jnp.concatenate([p[0, 0] for p in pages], axis=1).astype(BF16)
    key_chunk = _sortable_key(scores(jnp.dot(qi, ki_t, preferred_element_type=F32)))
    for jj in range(past // cw):
        @pl.when(j == jj)
        def _(jj=jj):
            key_s[:, jj * cw:(jj + 1) * cw] = key_chunk

    @pl.when(j == nsteps - 1)
    def _():
        kin_s[...] = jnp.zeros(kin_s.shape, F32)
        kin_s[0:t, :] = kinew_ref[0]
        snew = scores(_dot_nt(qi, kin_s[...].astype(BF16)))
        cn = lax.broadcasted_iota(I32, (t, LANES), 1)
        rn = lax.broadcasted_iota(I32, (t, LANES), 0)
        key_s[:, past:past + LANES] = jnp.where(cn <= rn, _sortable_key(snew), INT_MIN)
        ncols = past + LANES
        col = lax.broadcasted_iota(I32, (t, ncols), 1)
        pos = past + lax.broadcasted_iota(I32, (t, ncols), 0)
        vis = col <= pos
        topk = min(TOPK_MAX, (past + t) // 4)
        bias = _topk_bias(key_s, jcut_s, vis, col, topk, ncols)
        for g in range(A_GROUP):
            bias_ref[0, g * t:(g + 1) * t, :] = bias


def _dsa_sample_attn_kernel(pt_ref, q_ref, bias_ref, biasnew_ref, knew_ref, vnew_ref, *rest, pp, t):
    kpages = rest[:pp]
    vpages = rest[pp:2 * pp]
    o_ref, m_s, l_s, acc_s, new_s = rest[2 * pp:]
    j = pl.program_id(1)
    nsteps = pl.num_programs(1)

    @pl.when(j == 0)
    def _():
        m_s[...] = jnp.full(m_s.shape, MASK_BIAS, F32)
        l_s[...] = jnp.zeros(l_s.shape, F32)
        acc_s[...] = jnp.zeros(acc_s.shape, F32)

    def update(kh, kb, vb, bias, key_major):
        qh = (q_ref[0, kh] * (A_HEAD_DIM ** -0.5)).astype(BF16)
        qk = _dot_nt(qh, kb) if key_major else jnp.dot(qh, kb, preferred_element_type=F32)
        logits = qk + bias
        m_old = m_s[kh]
        m_new = jnp.maximum(m_old, jnp.max(logits, axis=-1, keepdims=True))
        alpha = jnp.exp(m_old - m_new)
        e = jnp.exp(logits - m_new)
        p = e.astype(BF16)
        pv = jnp.dot(p, vb, preferred_element_type=F32) if key_major else _dot_nt(p, vb)
        l_s[kh] = alpha * l_s[kh] + jnp.sum(e, axis=-1, keepdims=True)
        acc_s[kh] = alpha * acc_s[kh] + pv
        m_s[kh] = m_new

    for kh in range(A_KV_HEADS):
        k_t = jnp.concatenate([p[0, 0, kh] for p in kpages], axis=1).astype(BF16)
        v_t = jnp.concatenate([p[0, 0, kh] for p in vpages], axis=1).astype(BF16)
        update(kh, k_t, v_t, bias_ref[0], False)

    @pl.when(j == nsteps - 1)
    def _():
        for kh in range(A_KV_HEADS):
            sl = slice(kh * A_HEAD_DIM, (kh + 1) * A_HEAD_DIM)
            new_s[...] = jnp.zeros(new_s.shape, F32)
            new_s[0, 0:t, :] = knew_ref[0, :, sl]
            new_s[1, 0:t, :] = vnew_ref[0, :, sl]
            update(kh, new_s[0].astype(BF16), new_s[1].astype(BF16), biasnew_ref[0], True)
            o_ref[0, kh] = acc_s[kh] / l_s[kh]


def dsa_sample(q_g, qi_t, wi_t, ki_new, k_new, v_new, cache_k, cache_v, cache_ki, page_flat, layer, pp):
    db, _, gt, hd = q_g.shape
    t = gt // A_GROUP
    npg = page_flat.shape[0] // db
    assert npg % pp == 0
    nsteps = npg // pp
    past = npg * PAGE_SIZE
    ncols = past + LANES
    cw = pp * PAGE_SIZE
    kvw = k_new.shape[2]

    def page_spec(shape, i):
        nz = (0,) * len(shape)
        return pl.BlockSpec((1, 1) + shape, lambda b, j, pt, i=i: (layer, pt[b * npg + j * pp + i]) + nz)

    ki_page = (IDX_DIM, PAGE_SIZE)
    kv_page = (A_KV_HEADS, hd, PAGE_SIZE)

    bias = pl.pallas_call(
        functools.partial(_dsa_sample_score_kernel, pp=pp, t=t),
        grid_spec=pltpu.PrefetchScalarGridSpec(
            num_scalar_prefetch=1,
            grid=(db, nsteps),
            in_specs=[pl.BlockSpec((1, t * IDX_HEADS, IDX_DIM), lambda b, j, pt: (b, 0, 0)),
                      pl.BlockSpec((1, t * IDX_HEADS, 1), lambda b, j, pt: (b, 0, 0)),
                      pl.BlockSpec((1, t, IDX_DIM), lambda b, j, pt: (b, 0, 0))]
                     + [page_spec(ki_page, i) for i in range(pp)],
            out_specs=pl.BlockSpec((1, gt, ncols), lambda b, j, pt: (b, 0, 0)),
            scratch_shapes=[pltpu.VMEM((t, ncols), I32),
                            pltpu.VMEM((t, 1), I32),
                            pltpu.VMEM((LANES, IDX_DIM), F32)]),
        out_shape=jax.ShapeDtypeStruct((db, gt, ncols), F32),
        compiler_params=_cparams("parallel", "arbitrary"),
        name="dsa_sample_score",
    )(page_flat, qi_t, wi_t, ki_new, *([cache_ki] * pp))

    out = pl.pallas_call(
        functools.partial(_dsa_sample_attn_kernel, pp=pp, t=t),
        grid_spec=pltpu.PrefetchScalarGridSpec(
            num_scalar_prefetch=1,
            grid=(db, nsteps),
            in_specs=[pl.BlockSpec((1, A_KV_HEADS, gt, hd), lambda b, j, pt: (b, 0, 0, 0)),
                      pl.BlockSpec((1, gt, cw), lambda b, j, pt: (b, 0, j)),
                      pl.BlockSpec((1, gt, LANES), lambda b, j, pt: (b, 0, past // LANES)),
                      pl.BlockSpec((1, t, kvw), lambda b, j, pt: (b, 0, 0)),
                      pl.BlockSpec((1, t, kvw), lambda b, j, pt: (b, 0, 0))]
                     + [page_spec(kv_page, i) for i in range(pp)]
                     + [page_spec(kv_page, i) for i in range(pp)],
            out_specs=pl.BlockSpec((1, A_KV_HEADS, gt, hd), lambda b, j, pt: (b, 0, 0, 0)),
            scratch_shapes=[pltpu.VMEM((A_KV_HEADS, gt, 1), F32),
                            pltpu.VMEM((A_KV_HEADS, gt, 1), F32),
                            pltpu.VMEM((A_KV_HEADS, gt, hd), F32),
                            pltpu.VMEM((2, LANES, hd), F32)]),
        out_shape=jax.ShapeDtypeStruct((db, A_KV_HEADS, gt, hd), F32),
        compiler_params=_cparams("parallel", "arbitrary"),
        name="dsa_sample_attn",
    )(page_flat, q_g, bias, bias, k_new, v_new, *([cache_k] * pp), *([cache_v] * pp))
    return out


def _gla_kernel(*refs, tb, tv, chunk, sub, has_s0):
    if has_s0:
        (q_ref, k_ref, v_ref, g_ref, gd_ref, wgu_ref, bg_ref, gn_ref, s0_ref,
         on_ref, sout_ref, b_s, dec_s, st_s) = refs
    else:
        q_ref, k_ref, v_ref, g_ref, gd_ref, wgu_ref, bg_ref, gn_ref, on_ref, sout_ref, b_s, dec_s, st_s = refs
    jb = pl.program_id(1)
    nb = pl.num_programs(1)
    nfull, rem = tb // chunk, tb % chunk
    spans = [(ic * chunk, chunk) for ic in range(nfull)] + ([(nfull * chunk, rem)] if rem else [])

    @pl.when(jb == 0)
    def _():
        for h in range(GLA_HEADS):
            st_s[h] = s0_ref[0, h] if has_s0 else jnp.zeros((GLA_DK, GLA_DV), F32)

    x = jnp.dot(gd_ref[0], wgu_ref[...], precision=HIGHEST, preferred_element_type=F32) + bg_ref[...]
    la = (jnp.minimum(x, 0.0) - jnp.log1p(jnp.exp(-jnp.abs(x)))) * (1.0 / GLA_GATE_NORM)
    if tv is not None:
        row = jb * tb + lax.broadcasted_iota(I32, (tb, 1), 0)
        la = jnp.where(row < tv, la, 0.0)
    for ic, (r0, c) in enumerate(spans):
        lac = la[r0:r0 + c]
        tri = (lax.broadcasted_iota(I32, (c, c), 0) >= lax.broadcasted_iota(I32, (c, c), 1)).astype(F32)
        b_s[r0:r0 + c, :] = jnp.dot(tri, lac, precision=HIGHEST, preferred_element_type=F32)
        tot = lax.dot_general(lac, jnp.ones((c, LANES), F32), (((0,), (0,)), ((), ())),
                              precision=HIGHEST, preferred_element_type=F32)
        dec_s[ic] = jnp.exp(tot)

    def do_chunk(ic, r0, c):
        rows = pl.ds(r0, c)
        b = b_s[rows, :]
        qc = q_ref[0, rows, :] * (GLA_DK ** -0.5)
        kc = k_ref[0, rows, :]
        if tv is not None:
            rowc = jb * tb + r0 + lax.broadcasted_iota(I32, (c, 1), 0)
            kc = jnp.where(rowc < tv, kc, 0.0)
        vc = v_ref[0, rows, :].astype(BF16)
        gc = g_ref[0, rows, :]
        dec = dec_s[ic]
        causal = lax.broadcasted_iota(I32, (c, LANES), 1) <= lax.broadcasted_iota(I32, (c, LANES), 0)
        heads, states = [], []
        for h in range(GLA_HEADS):
            sk = slice(h * GLA_DK, (h + 1) * GLA_DK)
            sv = slice(h * GLA_DV, (h + 1) * GLA_DV)
            bh, qh, kh, vh = b[:, sk], qc[:, sk], kc[:, sk], vc[:, sv]
            st = st_s[h]
            att = []
            for i in range(c // sub):
                lo, hi = i * sub, (i + 1) * sub
                ref_b = bh[lo - 1:lo, :] if i > 0 else jnp.zeros((1, GLA_DK), F32)
                qt = (qh[lo:hi] * jnp.exp(bh[lo:hi] - ref_b)).astype(BF16)
                kt = (kh[0:hi] * jnp.exp(ref_b - bh[0:hi])).astype(BF16)
                kt = jnp.concatenate([kt, jnp.zeros((LANES - hi, GLA_DK), BF16)], axis=0)
                att.append(_dot_nt(qt, kt))
            att = jnp.concatenate(att, axis=0) if len(att) > 1 else att[0]
            att = jnp.where(causal, att, 0.0).astype(BF16)
            lhs = jnp.concatenate([(qh * jnp.exp(bh)).astype(BF16), att], axis=1)
            vpad = vh if c == LANES else jnp.concatenate([vh, jnp.zeros((LANES - c, GLA_DV), BF16)], axis=0)
            rhs = jnp.concatenate([st.astype(BF16), vpad], axis=0)
            o = jnp.dot(lhs, rhs, preferred_element_type=F32)
            bl = bh[c - 1:c, :]
            kd = (kh * jnp.exp(bl - bh)).astype(BF16)
            dh = dec[sk, :]
            states.append(jnp.concatenate([dh, dh], axis=1) * st + _dot_tn(kd, vh))
            gh = gc[:, sv]
            heads.append(_rms(o, gn_ref[...]) * (gh * jax.nn.sigmoid(gh)))
        on_ref[0, rows, :] = jnp.concatenate(heads, axis=1).astype(on_ref.dtype)
        for h in range(GLA_HEADS):
            st_s[h] = states[h]

    if nfull == 1:
        do_chunk(0, 0, chunk)
    elif nfull > 1:
        def body(ic, carry):
            do_chunk(ic, pl.multiple_of(ic * chunk, BF16_ROWS), chunk)
            return carry
        lax.fori_loop(0, nfull, body, 0)
    if rem:
        do_chunk(nfull, nfull * chunk, rem)

    @pl.when(jb == nb - 1)
    def _():
        for h in range(GLA_HEADS):
            sout_ref[0, h] = st_s[h]


def gla_mixer(q, k, v, g, gd, wgu, bg, gn, s0, tb, chunk, tv):
    b, t, _ = q.shape
    sub = BF16_ROWS
    assert t % tb == 0 and tb % sub == 0 and chunk % sub == 0 and chunk <= LANES and GLA_DV == 2 * LANES
    nspans = -(-tb // chunk)
    blk = lambda c: pl.BlockSpec((1, tb, c), lambda i, j: (i, j, 0))
    cst = lambda a: pl.BlockSpec(a.shape, lambda i, j: (0,) * a.ndim)
    st_spec = pl.BlockSpec((1, GLA_HEADS, GLA_DK, GLA_DV), lambda i, j: (i, 0, 0, 0))
    bg2, gn2 = bg.reshape(1, -1), gn.reshape(1, -1)
    args = [q, k, v, g, gd, wgu, bg2, gn2]
    in_specs = [blk(q.shape[2]), blk(k.shape[2]), blk(v.shape[2]), blk(g.shape[2]), blk(gd.shape[2]),
                cst(wgu), cst(bg2), cst(gn2)]
    if s0 is not None:
        args.append(s0)
        in_specs.append(st_spec)
    return pl.pallas_call(
        functools.partial(_gla_kernel, tb=tb, tv=tv, chunk=chunk, sub=sub, has_s0=s0 is not None),
        grid=(b, t // tb),
        in_specs=in_specs,
        out_specs=[blk(v.shape[2]), st_spec],
        out_shape=[jax.ShapeDtypeStruct(v.shape, BF16),
                   jax.ShapeDtypeStruct((b, GLA_HEADS, GLA_DK, GLA_DV), F32)],
        scratch_shapes=[pltpu.VMEM((tb, GLA_HEADS * GLA_DK), F32),
                        pltpu.VMEM((nspans, GLA_HEADS * GLA_DK, LANES), F32),
                        pltpu.VMEM((GLA_HEADS, GLA_DK, GLA_DV), F32)],
        compiler_params=_cparams("parallel", "arbitrary"),
        name="gla_mixer",
    )(*args)


def _pick_tile(n, candidates):
    for c in candidates:
        if n % c == 0:
            return c
    return n


def kernel(x_prompt, x_sample, cache_k, cache_v, cache_kidx, state_pool, state_gla, page_table,
           meta_tokens, norm_mix, norm_ffn, norm_final, w_in_even, w_out_even, pool_w, pool_scale,
           w_in_odd, gla_w_gate_up, gla_b_gate, gla_norm, w_out_odd, ffn_w_gate, ffn_w_up, ffn_w_down):
    bsz, seq, d = x_prompt.shape
    db, dseq, _ = x_sample.shape
    depth = norm_mix.shape[0]
    length = seq + N_META
    n_even = cache_k.shape[0]
    n_pool = cache_k.shape[1]
    npg = page_table.shape[1]
    past = npg * PAGE_SIZE
    a_width = A_HEADS * A_HEAD_DIM
    kv_width = A_KV_HEADS * A_HEAD_DIM
    pool_width = pool_scale.shape[1]
    gk = GLA_HEADS * GLA_DK
    gv = GLA_HEADS * GLA_DV
    d_ff = ffn_w_gate.shape[2]

    meta = jnp.broadcast_to(meta_tokens[None], (bsz, N_META, d))
    xp = jnp.concatenate([meta, x_prompt], axis=1).reshape(bsz * length, d)
    xs = x_sample.reshape(db * dseq, d)
    n_p, n_s = xp.shape[0], xs.shape[0]

    tm_p = _pick_tile(n_p, (688, 512, 256, 128, 64, 32, 16, 8))
    tm_s = _pick_tile(n_s, (128, 64, 32, 16, 8))
    tf = _pick_tile(d_ff, (1408, 1024, 512, 256, 128))
    tq = _pick_tile(length, (344, 256, 128, 64, 32, 16, 8))
    pool_rc = _pick_tile(length, (48, 32, 16, 8))
    gla_tb = _pick_tile(length, (688, 512, 256, 128, 64, 32, 16))
    pp = _pick_tile(npg, (16, 8, 4, 2, 1))

    e_off = [0]
    for s in (a_width, kv_width, kv_width, IDX_HEADS * IDX_DIM, IDX_DIM, IDX_HEADS, pool_width):
        e_off.append(e_off[-1] + s)
    kiwi_pad = LANES - IDX_DIM - IDX_HEADS
    even_widths = (a_width, kv_width, kv_width, IDX_HEADS * IDX_DIM, pool_width, LANES)
    o_off = [0]
    for s in (gk, gk, gv, gv, GLA_GATE_RANK):
        o_off.append(o_off[-1] + s)
    odd_widths = (gk, gk, gv, gv, LANES)

    cache_kt = cache_k.transpose(0, 1, 3, 4, 2)
    cache_vt = cache_v.transpose(0, 1, 3, 4, 2)
    cache_kit = cache_kidx.transpose(0, 1, 3, 2)
    page_flat = page_table.reshape(-1).astype(I32)

    kp_l, vp_l, kip_l, bp_l, ks_l, vs_l, kis_l, bs_l, sp_l, ss_l = ([] for _ in range(10))
    for l in range(depth):
        if l % 2 == 0:
            e = l // 2
            w = w_in_even[e]
            w_in = jnp.concatenate(
                [w[:, e_off[0]:e_off[4]], w[:, e_off[6]:e_off[7]], w[:, e_off[4]:e_off[6]],
                 jnp.zeros((d, kiwi_pad), w.dtype)], axis=1).astype(BF16)
            w_out = w_out_even[e].astype(BF16)
            pw = pool_w[e].astype(BF16)

            q, k, v, qi, u, kiwi = norm_matmul(xp, norm_mix[l], w_in, even_widths, tm_p)
            r3 = lambda a: a.reshape(bsz, length, a.shape[1])
            oa = dsa_prompt(r3(q), r3(qi), r3(k), r3(v), r3(kiwi), tq, DSA_KEY_CHUNK)
            ob, buf = pool_mixer(r3(u), jnp.zeros((bsz, POOL_BUF + 1, pool_width), F32), pw,
                                 pool_scale[e], 0, pool_rc, BF16)
            mix_p = [(oa.reshape(n_p, a_width), w_out[:a_width]),
                     (ob.reshape(n_p, pool_width), w_out[a_width:])]
            kp_l.append(k.reshape(bsz, length, A_KV_HEADS, A_HEAD_DIM))
            vp_l.append(v.reshape(bsz, length, A_KV_HEADS, A_HEAD_DIM))
            kip_l.append(kiwi[:, :IDX_DIM].reshape(bsz, length, IDX_DIM))
            bp_l.append(buf)

            q, k, v, qi, u, kiwi = norm_matmul(xs, norm_mix[l], w_in, even_widths, tm_s)
            s3 = lambda a: a.reshape(db, dseq, a.shape[1])
            q_g = q.reshape(db, dseq, A_KV_HEADS, A_GROUP, A_HEAD_DIM).transpose(0, 2, 3, 1, 4)
            q_g = q_g.reshape(db, A_KV_HEADS, A_GROUP * dseq, A_HEAD_DIM)
            qi_t = qi.reshape(db, dseq * IDX_HEADS, IDX_DIM)
            wi_t = kiwi[:, IDX_DIM:IDX_DIM + IDX_HEADS].reshape(db, dseq * IDX_HEADS, 1)
            ki_new = kiwi[:, :IDX_DIM].reshape(db, dseq, IDX_DIM)
            oa = dsa_sample(q_g, qi_t, wi_t, ki_new, s3(k), s3(v), cache_kt, cache_vt, cache_kit,
                            page_flat, e, pp)
            oa = oa.reshape(db, A_KV_HEADS, A_GROUP, dseq, A_HEAD_DIM).transpose(0, 3, 1, 2, 4)
            prefix = jnp.pad(state_pool[e], ((0, 0), (1, 0), (0, 0)))
            ob, buf = pool_mixer(s3(u), prefix, pw, pool_scale[e], past, BF16_ROWS, F32)
            mix_s = [(oa.reshape(n_s, a_width), w_out[:a_width]),
                     (ob.reshape(n_s, pool_width), w_out[a_width:])]
            ks_l.append(k.reshape(db, dseq, A_KV_HEADS, A_HEAD_DIM))
            vs_l.append(v.reshape(db, dseq, A_KV_HEADS, A_HEAD_DIM))
            kis_l.append(ki_new)
            bs_l.append(buf)
        else:
            o = l // 2
            w = w_in_odd[o]
            w_in = jnp.concatenate([w, jnp.zeros((d, LANES - GLA_GATE_RANK), w.dtype)], axis=1).astype(BF16)
            w_out = w_out_odd[o].astype(BF16)
            wgu = jnp.pad(gla_w_gate_up[o], ((0, LANES - GLA_GATE_RANK), (0, 0)))

            q, k, v, g, gd = norm_matmul(xp, norm_mix[l], w_in, odd_widths, tm_p)
            r3 = lambda a: a.reshape(bsz, length, a.shape[1])
            on, s_p = gla_mixer(r3(q), r3(k), r3(v), r3(g), r3(gd), wgu, gla_b_gate[o], gla_norm[o],
                                None, gla_tb, GLA_CHUNK, None)
            mix_p = [(on.reshape(n_p, gv), w_out)]
            sp_l.append(s_p)

            q, k, v, g, gd = norm_matmul(xs, norm_mix[l], w_in, odd_widths, tm_s)
            tpad = _round_up(dseq, BF16_ROWS)
            p3 = lambda a: jnp.pad(a.reshape(db, dseq, a.shape[1]), ((0, 0), (0, tpad - dseq), (0, 0)))
            on, s_s = gla_mixer(p3(q), p3(k), p3(v), p3(g), p3(gd), wgu, gla_b_gate[o], gla_norm[o],
                                state_gla[o], tpad, tpad, dseq)
            mix_s = [(on[:, :dseq].reshape(n_s, gv), w_out)]
            ss_l.append(s_s)

        wg, wu, wd = ffn_w_gate[l].astype(BF16), ffn_w_up[l].astype(BF16), ffn_w_down[l].astype(BF16)
        xp = mix_ffn(xp, mix_p, norm_ffn[l], wg, wu, wd, tm_p, tf)
        xs = mix_ffn(xs, mix_s, norm_ffn[l], wg, wu, wd, tm_s, tf)

    y_prompt = final_norm(xp.reshape(bsz, length, d), norm_final, N_META)
    y_sample = final_norm(xs.reshape(1, n_s, d), norm_final, 0).reshape(db, dseq, d)
    return (y_prompt, y_sample,
            jnp.stack(kp_l), jnp.stack(vp_l), jnp.stack(kip_l),
            jnp.stack(ks_l), jnp.stack(vs_l), jnp.stack(kis_l),
            jnp.stack(bp_l), jnp.stack(bs_l),
            jnp.stack(sp_l), jnp.stack(ss_l))
```

```python
import functools

import jax
import jax.numpy as jnp
from jax import lax
from jax.experimental import pallas as pl
from jax.experimental.pallas import tpu as pltpu

F32 = jnp.float32
BF16 = jnp.bfloat16
I32 = jnp.int32

NORM_EPS = 1e-6
N_META = 16
A_HEADS = 8
A_KV_HEADS = 2
A_GROUP = A_HEADS // A_KV_HEADS
A_HEAD_DIM = 64
IDX_HEADS = 8
IDX_DIM = 64
TOPK_MAX = 256
PAGE_SIZE = 128
POOL_WINDOWS = (2, 4, 8, 16)
POOL_GROUP_DIM = 128
POOL_BUF = 15
GLA_HEADS = 4
GLA_DK = 128
GLA_DV = 256
GLA_GATE_RANK = 16
GLA_GATE_NORM = 16.0

LANES = 128
SUBLANES = 8
BF16_ROWS = 16
VMEM_LIMIT_BYTES = 56 * 1024 * 1024

DSA_KEY_CHUNK = 2 * LANES
GLA_CHUNK = 96

INT_MIN = -(2 ** 31)
MASK_BIAS = -1e30
HIGHEST = lax.Precision.HIGHEST
QK_SCALE_LOG2 = (A_HEAD_DIM ** -0.5) * 1.4426950408889634


def _round_up(n, m):
    return (n + m - 1) // m * m


def _cparams(*sem):
    return pltpu.CompilerParams(dimension_semantics=sem, vmem_limit_bytes=VMEM_LIMIT_BYTES)


def _rms(x, g):
    ms = jnp.mean(x * x, axis=-1, keepdims=True)
    return (x * lax.rsqrt(ms + NORM_EPS)) * g


def _dot_nt(a, b):
    return lax.dot_general(a, b, (((1,), (1,)), ((), ())), preferred_element_type=F32)


def _dot_tn(a, b):
    return lax.dot_general(a, b, (((0,), (0,)), ((), ())), preferred_element_type=F32)


def _norm_matmul_kernel(x_ref, g_ref, w_ref, *o_refs, widths):
    h = _rms(x_ref[...], g_ref[...]).astype(BF16)
    off = 0
    for o_ref, wd in zip(o_refs, widths):
        o_ref[...] = jnp.dot(h, w_ref[:, off:off + wd], preferred_element_type=F32)
        off += wd


def norm_matmul(x, g, w, widths, tm):
    n, d = x.shape
    assert n % tm == 0 and sum(widths) == w.shape[1]
    return pl.pallas_call(
        functools.partial(_norm_matmul_kernel, widths=widths),
        grid=(n // tm,),
        in_specs=[pl.BlockSpec((tm, d), lambda i: (i, 0)),
                  pl.BlockSpec((1, d), lambda i: (0, 0)),
                  pl.BlockSpec((d, w.shape[1]), lambda i: (0, 0))],
        out_specs=[pl.BlockSpec((tm, wd), lambda i: (i, 0)) for wd in widths],
        out_shape=[jax.ShapeDtypeStruct((n, wd), F32) for wd in widths],
        compiler_params=_cparams("parallel"),
        name="norm_matmul",
    )(x, g.reshape(1, d), w)


def _mix_ffn_kernel(*refs, n_mix):
    x_ref = refs[0]
    mix = refs[1:1 + 2 * n_mix]
    g_ref, wg_ref, wu_ref, wd_ref, o_ref, h_s = refs[1 + 2 * n_mix:]
    j = pl.program_id(1)

    @pl.when(j == 0)
    def _():
        xm = x_ref[...]
        for m in range(n_mix):
            a_ref, w_ref = mix[2 * m], mix[2 * m + 1]
            xm = xm + jnp.dot(a_ref[...].astype(BF16), w_ref[...], preferred_element_type=F32)
        o_ref[...] = xm
        h_s[...] = _rms(xm, g_ref[...]).astype(BF16)

    h = h_s[...]
    gate = jnp.dot(h, wg_ref[...], preferred_element_type=F32)
    up = jnp.dot(h, wu_ref[...], preferred_element_type=F32)
    act = (gate * jax.nn.sigmoid(gate) * up).astype(BF16)
    o_ref[...] += jnp.dot(act, wd_ref[...], preferred_element_type=F32)


def mix_ffn(x, mixes, g, wg, wu, wd, tm, tf):
    n, d = x.shape
    f = wg.shape[1]
    assert n % tm == 0 and f % tf == 0
    in_specs = [pl.BlockSpec((tm, d), lambda i, j: (i, 0))]
    args = [x]
    for a, w in mixes:
        in_specs += [pl.BlockSpec((tm, a.shape[1]), lambda i, j: (i, 0)),
                     pl.BlockSpec(w.shape, lambda i, j: (0, 0))]
        args += [a, w]
    in_specs += [pl.BlockSpec((1, d), lambda i, j: (0, 0)),
                 pl.BlockSpec((d, tf), lambda i, j: (0, j)),
                 pl.BlockSpec((d, tf), lambda i, j: (0, j)),
                 pl.BlockSpec((tf, d), lambda i, j: (j, 0))]
    args += [g.reshape(1, d), wg, wu, wd]
    return pl.pallas_call(
        functools.partial(_mix_ffn_kernel, n_mix=len(mixes)),
        grid=(n // tm, f // tf),
        in_specs=in_specs,
        out_specs=pl.BlockSpec((tm, d), lambda i, j: (i, 0)),
        out_shape=jax.ShapeDtypeStruct((n, d), F32),
        scratch_shapes=[pltpu.VMEM((tm, d), BF16)],
        compiler_params=_cparams("parallel", "arbitrary"),
        name="mix_ffn",
    )(*args)


def _final_norm_kernel(x_ref, g_ref, o_ref, *, skip, rows, chunk):
    for c in range(rows // chunk):
        x = x_ref[0, skip + c * chunk: skip + (c + 1) * chunk, :]
        o_ref[0, c * chunk:(c + 1) * chunk, :] = _rms(x, g_ref[...])


def final_norm(x, g, skip):
    b, t, d = x.shape
    rows = t - skip
    chunk = 256 if rows % 256 == 0 else rows
    return pl.pallas_call(
        functools.partial(_final_norm_kernel, skip=skip, rows=rows, chunk=chunk),
        grid=(b,),
        in_specs=[pl.BlockSpec((1, t, d), lambda i: (i, 0, 0)),
                  pl.BlockSpec((1, d), lambda i: (0, 0))],
        out_specs=pl.BlockSpec((1, rows, d), lambda i: (i, 0, 0)),
        out_shape=jax.ShapeDtypeStruct((b, rows, d), F32),
        compiler_params=_cparams("parallel"),
        name="final_norm",
    )(x, g.reshape(1, d))


def _pool_kernel(u_ref, pre_ref, pw_ref, sc_ref, ob_ref, buf_ref, ext_s, *, t, pos0, rc):
    head = POOL_BUF + 1
    tproc = _round_up(t, rc)
    ext_s[0:head, :] = pre_ref[0]
    ext_s[head:head + t, :] = u_ref[0]
    if tproc > t:
        ext_s[head + t:head + tproc, :] = jnp.zeros((tproc - t, ext_s.shape[1]), F32)
    buf_ref[0] = ext_s[t + 1:t + head, :]
    for c in range(tproc // rc):
        r0 = c * rc
        nst = min(rc, t - r0)
        pos = pos0 + r0 + lax.broadcasted_iota(I32, (rc, 1), 0)
        for gi, w in enumerate(POOL_WINDOWS):
            sl = slice(gi * POOL_GROUP_DIM, (gi + 1) * POOL_GROUP_DIM)
            xt = ext_s[head + r0:head + r0 + rc, sl]
            acc = xt
            for r in range(1, w):
                acc = acc + ext_s[head + r0 - r:head + r0 - r + rc, sl]
            cnt = jnp.minimum(w, pos + 1).astype(F32)
            dlt = acc / cnt - xt
            y = jnp.dot(dlt.astype(BF16), pw_ref[gi], preferred_element_type=F32) * sc_ref[:, sl]
            ob_ref[0, r0:r0 + nst, sl] = y[0:nst].astype(ob_ref.dtype)


def pool_mixer(u, prefix16, pw, scale, pos0, rc, out_dtype):
    b, t, c = u.shape
    head = POOL_BUF + 1
    return pl.pallas_call(
        functools.partial(_pool_kernel, t=t, pos0=pos0, rc=rc),
        grid=(b,),
        in_specs=[pl.BlockSpec((1, t, c), lambda i: (i, 0, 0)),
                  pl.BlockSpec((1, head, c), lambda i: (i, 0, 0)),
                  pl.BlockSpec(pw.shape, lambda i: (0, 0, 0)),
                  pl.BlockSpec((1, c), lambda i: (0, 0))],
        out_specs=[pl.BlockSpec((1, t, c), lambda i: (i, 0, 0)),
                   pl.BlockSpec((1, POOL_BUF, c), lambda i: (i, 0, 0))],
        out_shape=[jax.ShapeDtypeStruct((b, t, c), out_dtype),
                   jax.ShapeDtypeStruct((b, POOL_BUF, c), F32)],
        scratch_shapes=[pltpu.VMEM((head + _round_up(t, rc), c), F32)],
        compiler_params=_cparams("parallel"),
        name="pool_mixer",
    )(u, prefix16, pw, scale.reshape(1, c))


def _sortable_key(score):
    bits = lax.bitcast_convert_type(score, I32)
    return bits ^ ((bits >> 31) & 0x7FFFFFFF)


def _lane_tile(x, n):
    return x if n == 1 else jnp.concatenate([x] * n, axis=1)


def _row_total(x):
    return jnp.broadcast_to(jnp.sum(x, axis=-1, keepdims=True), x.shape)


def _dsa_prompt_kernel(q_ref, qi_ref, k_ref, v_ref, kiwi_ref, o_ref,
                       kb_s, vb_s, kib_s, qb_s, qib_s, wib_s, key_s, keyt_s, m_s, acc_s,
                       *, length, tq, kc, topk):
    nkc = kib_s.shape[0]
    nl = kc // LANES
    tqp = keyt_s.shape[2]
    tqb = qb_s.shape[1] // A_GROUP
    it = pl.program_id(1)
    r0 = it * tq
    nk = (r0 + tq + kc - 1) // kc
    kf = float(topk)

    @pl.when(it == 0)
    def _():
        ones_col = (lax.broadcasted_iota(I32, (kc, LANES - A_HEAD_DIM), 1) == 0).astype(BF16)
        for c in range(nkc):
            lo = c * kc
            valid = max(0, min(kc, length - lo))
            for kh in range(A_KV_HEADS):
                sl = slice(kh * A_HEAD_DIM, (kh + 1) * A_HEAD_DIM)
                if valid < kc:
                    kb_s[kh, c] = jnp.zeros((kc, A_HEAD_DIM), BF16)
                    vb_s[kh, c, :, 0:A_HEAD_DIM] = jnp.zeros((kc, A_HEAD_DIM), BF16)
                if valid:
                    kb_s[kh, c, 0:valid, :] = k_ref[0, lo:lo + valid, sl].astype(BF16)
                    vb_s[kh, c, 0:valid, 0:A_HEAD_DIM] = v_ref[0, lo:lo + valid, sl].astype(BF16)
                vb_s[kh, c, :, A_HEAD_DIM:LANES] = ones_col
            if valid < kc:
                kib_s[c] = jnp.zeros((kc, IDX_DIM), BF16)
            if valid:
                kib_s[c, 0:valid, :] = kiwi_ref[0, lo:lo + valid, 0:IDX_DIM].astype(BF16)

    def pad_rows(x, n):
        return x if x.shape[0] == n else jnp.concatenate([x, jnp.zeros((n - x.shape[0], x.shape[1]), x.dtype)], axis=0)

    kiwi_t = pad_rows(kiwi_ref[0, pl.ds(pl.multiple_of(r0, SUBLANES), tq), :], tqp)
    for h in range(IDX_HEADS):
        qih = pad_rows(qi_ref[0, :, h * IDX_DIM:(h + 1) * IDX_DIM], tqp)
        qib_s[h * tqp:(h + 1) * tqp, :] = qih.astype(BF16)
        wib_s[h] = jnp.broadcast_to(kiwi_t[:, IDX_DIM + h:IDX_DIM + h + 1], (tqp, LANES))
    for hd in range(A_HEADS):
        kh, g = divmod(hd, A_GROUP)
        qh = pad_rows(q_ref[0, :, hd * A_HEAD_DIM:(hd + 1) * A_HEAD_DIM] * QK_SCALE_LOG2, tqb)
        qb_s[kh, g * tqb:(g + 1) * tqb, :] = qh.astype(BF16)
        m_s[hd] = jnp.full((tq, LANES), MASK_BIAS, F32)
        acc_s[hd] = jnp.zeros((tq, LANES), F32)

    def score_chunk(c, carry):
        s_all = _dot_nt(qib_s[...], kib_s[c])
        score = jnp.zeros((tqp, kc), F32)
        for h in range(IDX_HEADS):
            score = score + jnp.maximum(s_all[h * tqp:(h + 1) * tqp], 0.0) * _lane_tile(wib_s[h], nl)
        col = c * kc + lax.broadcasted_iota(I32, (tqp, kc), 1)
        pos = r0 + lax.broadcasted_iota(I32, (tqp, kc), 0)
        key = jnp.where(col <= pos, _sortable_key(score), INT_MIN)
        key_s[c] = key
        keyt_s[c] = key.T
        return carry

    lax.fori_loop(0, nk, score_chunk, 0)

    def count(pred):
        def body(c, acc):
            hit = pred(keyt_s[c], c).astype(F32)
            return acc + jnp.sum(hit.reshape(kc // SUBLANES, SUBLANES, tqp), axis=0)
        acc = lax.fori_loop(0, nk, body, jnp.zeros((SUBLANES, tqp), F32))
        return jnp.sum(acc, axis=0, keepdims=True)

    t0 = jnp.where(count(lambda k, c: k >= 0) >= kf, 0, INT_MIN).astype(I32)

    def bit_step(b, t):
        cand = t + lax.shift_left(jnp.int32(1), 30 - b)
        return jnp.where(count(lambda k, c: k >= cand) >= kf, cand, t)

    thr_q = lax.fori_loop(0, 31, bit_step, t0)
    tie_q = (count(lambda k, c: k >= thr_q) > kf) & (thr_q > INT_MIN)

    ncols = nkc * kc
    nbits = max(1, (ncols - 1).bit_length())

    def break_ties():
        need = kf - count(lambda k, c: k > thr_q)
        kidx = lax.broadcasted_iota(I32, (kc, tqp), 0)

        def idx_step(b, jc):
            cand = jc + lax.shift_left(jnp.int32(1), nbits - 1 - b)
            n_lt = count(lambda k, c: (k == thr_q) & (c * kc + kidx < cand))
            return jnp.where(n_lt < need, cand, jc)

        jc = lax.fori_loop(0, nbits, idx_step, jnp.zeros((1, tqp), I32))
        return jnp.where(tie_q, jc, ncols)

    jcut_q = lax.cond(jnp.max(tie_q.astype(F32)) > 0.0, break_ties,
                      lambda: jnp.full((1, tqp), ncols, I32))

    def per_row(x_q):
        return jnp.broadcast_to(x_q, (LANES, tqp)).T[0:tq]

    thr_t = _lane_tile(per_row(thr_q), nl)
    jcut_t = _lane_tile(per_row(jcut_q), nl)

    def attend_chunk(c, carry):
        keys = key_s[c, 0:tq, :]
        col = c * kc + lax.broadcasted_iota(I32, (tq, kc), 1)
        pos = r0 + lax.broadcasted_iota(I32, (tq, kc), 0)
        sel = (col <= pos) & ((keys > thr_t) | ((keys == thr_t) & (col <= jcut_t)))
        bias = jnp.where(sel, 0.0, MASK_BIAS).astype(F32)
        for kh in range(A_KV_HEADS):
            l_all = _dot_nt(qb_s[kh], kb_s[kh, c])
            es, stats = [], []
            for g in range(A_GROUP):
                hd = kh * A_GROUP + g
                logits = l_all[g * tqb:g * tqb + tq] + bias
                m_old = m_s[hd]
                m_new = jnp.maximum(m_old, jnp.broadcast_to(jnp.max(logits, axis=-1, keepdims=True), (tq, LANES)))
                es.append(pad_rows(jnp.exp2(logits - _lane_tile(m_new, nl)), tqb).astype(BF16))
                stats.append((hd, m_old, m_new))
            pv_all = jnp.dot(jnp.concatenate(es, axis=0), vb_s[kh, c], preferred_element_type=F32)
            for g, (hd, m_old, m_new) in enumerate(stats):
                acc_s[hd] = jnp.exp2(m_old - m_new) * acc_s[hd] + pv_all[g * tqb:g * tqb + tq]
                m_s[hd] = m_new
        return carry

    lax.fori_loop(0, nk, attend_chunk, 0)

    for hd in range(A_HEADS):
        acc = acc_s[hd]
        o_ref[0, :, hd * A_HEAD_DIM:(hd + 1) * A_HEAD_DIM] = (
            acc[:, 0:A_HEAD_DIM] / acc[:, A_HEAD_DIM:A_HEAD_DIM + 1])


def dsa_prompt(q, qi, k, v, kiwi, tq, kc):
    b, length, _ = q.shape
    assert length % tq == 0 and kc % LANES == 0
    topk = min(TOPK_MAX, length // 4)
    nkc = -(-length // kc)
    tqp = _round_up(tq, LANES)
    tqb = _round_up(tq, BF16_ROWS)
    seq = lambda c: pl.BlockSpec((1, length, c), lambda i, j: (i, 0, 0))
    til = lambda c: pl.BlockSpec((1, tq, c), lambda i, j: (i, j, 0))
    return pl.pallas_call(
        functools.partial(_dsa_prompt_kernel, length=length, tq=tq, kc=kc, topk=topk),
        grid=(b, length // tq),
        in_specs=[til(q.shape[2]), til(qi.shape[2]), seq(k.shape[2]), seq(v.shape[2]), seq(kiwi.shape[2])],
        out_specs=til(q.shape[2]),
        out_shape=jax.ShapeDtypeStruct(q.shape, F32),
        scratch_shapes=[pltpu.VMEM((A_KV_HEADS, nkc, kc, A_HEAD_DIM), BF16),
                        pltpu.VMEM((A_KV_HEADS, nkc, kc, LANES), BF16),
                        pltpu.VMEM((nkc, kc, IDX_DIM), BF16),
                        pltpu.VMEM((A_KV_HEADS, A_GROUP * tqb, A_HEAD_DIM), BF16),
                        pltpu.VMEM((IDX_HEADS * tqp, IDX_DIM), BF16),
                        pltpu.VMEM((IDX_HEADS, tqp, LANES), F32),
                        pltpu.VMEM((nkc, tqp, kc), I32),
                        pltpu.VMEM((nkc, kc, tqp), I32),
                        pltpu.VMEM((A_HEADS, tq, LANES), F32),
                        pltpu.VMEM((A_HEADS, tq, LANES), F32)],
        compiler_params=_cparams("parallel", "arbitrary"),
        name="dsa_prompt",
    )(q, qi, k, v, kiwi)


def _dsa_sample_score_kernel(pt_ref, qi_ref, wi_ref, kinew_ref, *rest, pp, t):
    pages = rest[:pp]
    key_ref, kin_s = rest[pp:]
    j = pl.program_id(1)
    nsteps = pl.num_programs(1)
    cw = pp * PAGE_SIZE
    past = key_ref.shape[2] - LANES
    qi = qi_ref[0].astype(BF16)
    wi = wi_ref[0]

    def scores(s):
        s = jnp.maximum(s, 0.0) * wi
        return jnp.sum(s.reshape(t, IDX_HEADS, s.shape[1]), axis=1)

    ki_t = jnp.concatenate([p[0, 0] for p in pages], axis=1).astype(BF16)
    key_chunk = _sortable_key(scores(jnp.dot(qi, ki_t, preferred_element_type=F32)))
    for jj in range(past // cw):
        @pl.when(j == jj)
        def _(jj=jj):
            key_ref[0, :, jj * cw:(jj + 1) * cw] = key_chunk

    @pl.when(j == nsteps - 1)
    def _():
        kin_s[...] = jnp.zeros(kin_s.shape, F32)
        kin_s[0:t, :] = kinew_ref[0]
        snew = scores(_dot_nt(qi, kin_s[...].astype(BF16)))
        cn = lax.broadcasted_iota(I32, (t, LANES), 1)
        rn = lax.broadcasted_iota(I32, (t, LANES), 0)
        key_ref[0, :, past:past + LANES] = jnp.where(cn <= rn, _sortable_key(snew), INT_MIN)


def _dsa_select_kernel(key_ref, thr_ref, jcut_ref, *, topk):
    rows, ncols = key_ref.shape
    kf = float(topk)

    def count(pred):
        acc = jnp.zeros((rows, LANES), F32)
        for c0 in range(0, ncols, LANES):
            acc = acc + pred(key_ref[:, c0:c0 + LANES], c0).astype(F32)
        return _row_total(acc)

    t0 = jnp.where(count(lambda k, c0: k >= 0) >= kf, 0, INT_MIN).astype(I32)

    def bit_step(b, t):
        cand = t + lax.shift_left(jnp.int32(1), 30 - b)
        return jnp.where(count(lambda k, c0: k >= cand) >= kf, cand, t)

    thr = lax.fori_loop(0, 31, bit_step, t0)
    tie = (count(lambda k, c0: k >= thr) > kf) & (thr > INT_MIN)
    nbits = max(1, (ncols - 1).bit_length())

    def break_ties():
        need = kf - count(lambda k, c0: k > thr)
        lane = lax.broadcasted_iota(I32, (rows, LANES), 1)

        def idx_step(b, jc):
            cand = jc + lax.shift_left(jnp.int32(1), nbits - 1 - b)
            n_lt = count(lambda k, c0: (k == thr) & (c0 + lane < cand))
            return jnp.where(n_lt < need, cand, jc)

        jc = lax.fori_loop(0, nbits, idx_step, jnp.zeros((rows, LANES), I32))
        return jnp.where(tie, jc, ncols)

    thr_ref[...] = thr
    jcut_ref[...] = lax.cond(jnp.max(tie.astype(F32)) > 0.0, break_ties,
                             lambda: jnp.full((rows, LANES), ncols, I32))


def _dsa_sample_attn_kernel(pt_ref, q_ref, key_ref, keynew_ref, thr_ref, jcut_ref, knew_ref, vnew_ref,
                            *rest, pp, t):
    kpages = rest[:pp]
    vpages = rest[pp:2 * pp]
    o_ref, m_s, l_s, acc_s, new_s = rest[2 * pp:]
    j = pl.program_id(1)
    nsteps = pl.num_programs(1)
    cw = pp * PAGE_SIZE

    def bias_of(keys, c0):
        n = keys.shape[1] // LANES
        thr, jcut = _lane_tile(thr_ref[0], n), _lane_tile(jcut_ref[0], n)
        col = c0 + lax.broadcasted_iota(I32, keys.shape, 1)
        sel = (keys > INT_MIN) & ((keys > thr) | ((keys == thr) & (col <= jcut)))
        bias = jnp.where(sel, 0.0, MASK_BIAS).astype(F32)
        return jnp.concatenate([bias] * A_GROUP, axis=0)

    @pl.when(j == 0)
    def _():
        m_s[...] = jnp.full(m_s.shape, MASK_BIAS, F32)
        l_s[...] = jnp.zeros(l_s.shape, F32)
        acc_s[...] = jnp.zeros(acc_s.shape, F32)

    def update(kh, kb, vb, bias, key_major):
        qh = (q_ref[0, kh] * (A_HEAD_DIM ** -0.5)).astype(BF16)
        qk = _dot_nt(qh, kb) if key_major else jnp.dot(qh, kb, preferred_element_type=F32)
        logits = qk + bias
        m_old = m_s[kh]
        m_new = jnp.maximum(m_old, jnp.max(logits, axis=-1, keepdims=True))
        alpha = jnp.exp(m_old - m_new)
        e = jnp.exp(logits - m_new)
        p = e.astype(BF16)
        pv = jnp.dot(p, vb, preferred_element_type=F32) if key_major else _dot_nt(p, vb)
        l_s[kh] = alpha * l_s[kh] + jnp.sum(e, axis=-1, keepdims=True)
        acc_s[kh] = alpha * acc_s[kh] + pv
        m_s[kh] = m_new

    bias = bias_of(key_ref[0], j * cw)
    for kh in range(A_KV_HEADS):
        k_t = jnp.concatenate([p[0, 0, kh] for p in kpages], axis=1).astype(BF16)
        v_t = jnp.concatenate([p[0, 0, kh] for p in vpages], axis=1).astype(BF16)
        update(kh, k_t, v_t, bias, False)

    @pl.when(j == nsteps - 1)
    def _():
        bias_new = bias_of(keynew_ref[0], nsteps * cw)
        for kh in range(A_KV_HEADS):
            sl = slice(kh * A_HEAD_DIM, (kh + 1) * A_HEAD_DIM)
            new_s[...] = jnp.zeros(new_s.shape, F32)
            new_s[0, 0:t, :] = knew_ref[0, :, sl]
            new_s[1, 0:t, :] = vnew_ref[0, :, sl]
            update(kh, new_s[0].astype(BF16), new_s[1].astype(BF16), bias_new, True)
            o_ref[0, kh] = acc_s[kh] / l_s[kh]


def dsa_sample(q_g, qi_t, wi_t, ki_new, k_new, v_new, cache_k, cache_v, cache_ki, page_flat, layer, pp):
    db, _, gt, hd = q_g.shape
    t = gt // A_GROUP
    npg = page_flat.shape[0] // db
    assert npg % pp == 0
    nsteps = npg // pp
    past = npg * PAGE_SIZE
    ncols = past + LANES
    cw = pp * PAGE_SIZE
    kvw = k_new.shape[2]

    def page_spec(shape, i):
        nz = (0,) * len(shape)
        return pl.BlockSpec((1, 1) + shape, lambda b, j, pt, i=i: (layer, pt[b * npg + j * pp + i]) + nz)

    ki_page = (IDX_DIM, PAGE_SIZE)
    kv_page = (A_KV_HEADS, hd, PAGE_SIZE)

    keys = pl.pallas_call(
        functools.partial(_dsa_sample_score_kernel, pp=pp, t=t),
        grid_spec=pltpu.PrefetchScalarGridSpec(
            num_scalar_prefetch=1,
            grid=(db, nsteps),
            in_specs=[pl.BlockSpec((1, t * IDX_HEADS, IDX_DIM), lambda b, j, pt: (b, 0, 0)),
                      pl.BlockSpec((1, t * IDX_HEADS, 1), lambda b, j, pt: (b, 0, 0)),
                      pl.BlockSpec((1, t, IDX_DIM), lambda b, j, pt: (b, 0, 0))]
                     + [page_spec(ki_page, i) for i in range(pp)],
            out_specs=pl.BlockSpec((1, t, ncols), lambda b, j, pt: (b, 0, 0)),
            scratch_shapes=[pltpu.VMEM((LANES, IDX_DIM), F32)]),
        out_shape=jax.ShapeDtypeStruct((db, t, ncols), I32),
        compiler_params=_cparams("parallel", "arbitrary"),
        name="dsa_sample_score",
    )(page_flat, qi_t, wi_t, ki_new, *([cache_ki] * pp))

    stat = jax.ShapeDtypeStruct((db * t, LANES), I32)
    thr, jcut = pl.pallas_call(
        functools.partial(_dsa_select_kernel, topk=min(TOPK_MAX, (past + t) // 4)),
        out_shape=[stat, stat],
        compiler_params=pltpu.CompilerParams(vmem_limit_bytes=VMEM_LIMIT_BYTES),
        name="dsa_sample_select",
    )(keys.reshape(db * t, ncols))
    thr, jcut = thr.reshape(db, t, LANES), jcut.reshape(db, t, LANES)

    out = pl.pallas_call(
        functools.partial(_dsa_sample_attn_kernel, pp=pp, t=t),
        grid_spec=pltpu.PrefetchScalarGridSpec(
            num_scalar_prefetch=1,
            grid=(db, nsteps),
            in_specs=[pl.BlockSpec((1, A_KV_HEADS, gt, hd), lambda b, j, pt: (b, 0, 0, 0)),
                      pl.BlockSpec((1, t, cw), lambda b, j, pt: (b, 0, j)),
                      pl.BlockSpec((1, t, LANES), lambda b, j, pt: (b, 0, past // LANES)),
                      pl.BlockSpec((1, t, LANES), lambda b, j, pt: (b, 0, 0)),
                      pl.BlockSpec((1, t, LANES), lambda b, j, pt: (b, 0, 0)),
                      pl.BlockSpec((1, t, kvw), lambda b, j, pt: (b, 0, 0)),
                      pl.BlockSpec((1, t, kvw), lambda b, j, pt: (b, 0, 0))]
                     + [page_spec(kv_page, i) for i in range(pp)]
                     + [page_spec(kv_page, i) for i in range(pp)],
            out_specs=pl.BlockSpec((1, A_KV_HEADS, gt, hd), lambda b, j, pt: (b, 0, 0, 0)),
            scratch_shapes=[pltpu.VMEM((A_KV_HEADS, gt, 1), F32),
                            pltpu.VMEM((A_KV_HEADS, gt, 1), F32),
                            pltpu.VMEM((A_KV_HEADS, gt, hd), F32),
                            pltpu.VMEM((2, LANES, hd), F32)]),
        out_shape=jax.ShapeDtypeStruct((db, A_KV_HEADS, gt, hd), F32),
        compiler_params=_cparams("parallel", "arbitrary"),
        name="dsa_sample_attn",
    )(page_flat, q_g, keys, keys, thr, jcut, k_new, v_new, *([cache_k] * pp), *([cache_v] * pp))
    return out


def _gla_kernel(*refs, tb, tv, chunk, sub, has_s0):
    if has_s0:
        (q_ref, k_ref, v_ref, g_ref, gd_ref, wgu_ref, bg_ref, gn_ref, s0_ref,
         on_ref, sout_ref, b_s, dec_s, st_s) = refs
    else:
        q_ref, k_ref, v_ref, g_ref, gd_ref, wgu_ref, bg_ref, gn_ref, on_ref, sout_ref, b_s, dec_s, st_s = refs
    jb = pl.program_id(1)
    nb = pl.num_programs(1)
    nfull, rem = tb // chunk, tb % chunk
    spans = [(ic * chunk, chunk) for ic in range(nfull)] + ([(nfull * chunk, rem)] if rem else [])

    @pl.when(jb == 0)
    def _():
        for h in range(GLA_HEADS):
            st_s[h] = s0_ref[0, 0, h] if has_s0 else jnp.zeros((GLA_DK, GLA_DV), F32)

    x = jnp.dot(gd_ref[0], wgu_ref[...], precision=HIGHEST, preferred_element_type=F32) + bg_ref[...]
    la = (jnp.minimum(x, 0.0) - jnp.log1p(jnp.exp(-jnp.abs(x)))) * (1.0 / GLA_GATE_NORM)
    if tv is not None:
        row = jb * tb + lax.broadcasted_iota(I32, (tb, 1), 0)
        la = jnp.where(row < tv, la, 0.0)
    for ic, (r0, c) in enumerate(spans):
        lac = la[r0:r0 + c]
        tri = (lax.broadcasted_iota(I32, (c, c), 0) >= lax.broadcasted_iota(I32, (c, c), 1)).astype(F32)
        b_s[r0:r0 + c, :] = jnp.dot(tri, lac, precision=HIGHEST, preferred_element_type=F32)
        tot = lax.dot_general(lac, jnp.ones((c, LANES), F32), (((0,), (0,)), ((), ())),
                              precision=HIGHEST, preferred_element_type=F32)
        dec_s[ic] = jnp.exp(tot)

    def do_chunk(ic, r0, c):
        rows = pl.ds(r0, c)
        b = b_s[rows, :]
        qc = q_ref[0, rows, :] * (GLA_DK ** -0.5)
        kc = k_ref[0, rows, :]
        if tv is not None:
            rowc = jb * tb + r0 + lax.broadcasted_iota(I32, (c, 1), 0)
            kc = jnp.where(rowc < tv, kc, 0.0)
        vc = v_ref[0, rows, :].astype(BF16)
        gc = g_ref[0, rows, :]
        dec = dec_s[ic]
        causal = lax.broadcasted_iota(I32, (c, LANES), 1) <= lax.broadcasted_iota(I32, (c, LANES), 0)
        heads, states = [], []
        for h in range(GLA_HEADS):
            sk = slice(h * GLA_DK, (h + 1) * GLA_DK)
            sv = slice(h * GLA_DV, (h + 1) * GLA_DV)
            bh, qh, kh, vh = b[:, sk], qc[:, sk], kc[:, sk], vc[:, sv]
            st = st_s[h]
            att = []
            for i in range(c // sub):
                lo, hi = i * sub, (i + 1) * sub
                ref_b = bh[lo - 1:lo, :] if i > 0 else jnp.zeros((1, GLA_DK), F32)
                qt = (qh[lo:hi] * jnp.exp(bh[lo:hi] - ref_b)).astype(BF16)
                kt = (kh[0:hi] * jnp.exp(ref_b - bh[0:hi])).astype(BF16)
                kt = jnp.concatenate([kt, jnp.zeros((LANES - hi, GLA_DK), BF16)], axis=0)
                att.append(_dot_nt(qt, kt))
            att = jnp.concatenate(att, axis=0) if len(att) > 1 else att[0]
            att = jnp.where(causal, att, 0.0).astype(BF16)
            lhs = jnp.concatenate([(qh * jnp.exp(bh)).astype(BF16), att], axis=1)
            vpad = vh if c == LANES else jnp.concatenate([vh, jnp.zeros((LANES - c, GLA_DV), BF16)], axis=0)
            rhs = jnp.concatenate([st.astype(BF16), vpad], axis=0)
            o = jnp.dot(lhs, rhs, preferred_element_type=F32)
            bl = bh[c - 1:c, :]
            kd = (kh * jnp.exp(bl - bh)).astype(BF16)
            dh = dec[sk, :]
            states.append(jnp.concatenate([dh, dh], axis=1) * st + _dot_tn(kd, vh))
            gh = gc[:, sv]
            heads.append(_rms(o, gn_ref[...]) * (gh * jax.nn.sigmoid(gh)))
        on_ref[0, rows, :] = jnp.concatenate(heads, axis=1).astype(on_ref.dtype)
        for h in range(GLA_HEADS):
            st_s[h] = states[h]

    if nfull == 1:
        do_chunk(0, 0, chunk)
    elif nfull > 1:
        def body(ic, carry):
            do_chunk(ic, pl.multiple_of(ic * chunk, BF16_ROWS), chunk)
            return carry
        lax.fori_loop(0, nfull, body, 0)
    if rem:
        do_chunk(nfull, nfull * chunk, rem)

    @pl.when(jb == nb - 1)
    def _():
        for h in range(GLA_HEADS):
            sout_ref[0, h] = st_s[h]


def gla_mixer(q, k, v, g, gd, wgu, bg, gn, s0, s0_layer, tb, chunk, tv):
    b, t, _ = q.shape
    sub = BF16_ROWS
    assert t % tb == 0 and tb % sub == 0 and chunk % sub == 0 and chunk <= LANES and GLA_DV == 2 * LANES
    nspans = -(-tb // chunk)
    blk = lambda c: pl.BlockSpec((1, tb, c), lambda i, j: (i, j, 0))
    cst = lambda a: pl.BlockSpec(a.shape, lambda i, j: (0,) * a.ndim)
    st_spec = pl.BlockSpec((1, GLA_HEADS, GLA_DK, GLA_DV), lambda i, j: (i, 0, 0, 0))
    bg2, gn2 = bg.reshape(1, -1), gn.reshape(1, -1)
    args = [q, k, v, g, gd, wgu, bg2, gn2]
    in_specs = [blk(q.shape[2]), blk(k.shape[2]), blk(v.shape[2]), blk(g.shape[2]), blk(gd.shape[2]),
                cst(wgu), cst(bg2), cst(gn2)]
    if s0 is not None:
        args.append(s0)
        in_specs.append(pl.BlockSpec((1, 1, GLA_HEADS, GLA_DK, GLA_DV), lambda i, j: (s0_layer, i, 0, 0, 0)))
    return pl.pallas_call(
        functools.partial(_gla_kernel, tb=tb, tv=tv, chunk=chunk, sub=sub, has_s0=s0 is not None),
        grid=(b, t // tb),
        in_specs=in_specs,
        out_specs=[blk(v.shape[2]), st_spec],
        out_shape=[jax.ShapeDtypeStruct(v.shape, BF16),
                   jax.ShapeDtypeStruct((b, GLA_HEADS, GLA_DK, GLA_DV), F32)],
        scratch_shapes=[pltpu.VMEM((tb, GLA_HEADS * GLA_DK), F32),
                        pltpu.VMEM((nspans, GLA_HEADS * GLA_DK, LANES), F32),
                        pltpu.VMEM((GLA_HEADS, GLA_DK, GLA_DV), F32)],
        compiler_params=_cparams("parallel", "arbitrary"),
        name="gla_mixer",
    )(*args)


def _pick_tile(n, candidates):
    for c in candidates:
        if n % c == 0:
            return c
    return n


def kernel(x_prompt, x_sample, cache_k, cache_v, cache_kidx, state_pool, state_gla, page_table,
           meta_tokens, norm_mix, norm_ffn, norm_final, w_in_even, w_out_even, pool_w, pool_scale,
           w_in_odd, gla_w_gate_up, gla_b_gate, gla_norm, w_out_odd, ffn_w_gate, ffn_w_up, ffn_w_down):
    bsz, seq, d = x_prompt.shape
    db, dseq, _ = x_sample.shape
    depth = norm_mix.shape[0]
    length = seq + N_META
    n_even = cache_k.shape[0]
    n_pool = cache_k.shape[1]
    npg = page_table.shape[1]
    past = npg * PAGE_SIZE
    a_width = A_HEADS * A_HEAD_DIM
    kv_width = A_KV_HEADS * A_HEAD_DIM
    pool_width = pool_scale.shape[1]
    gk = GLA_HEADS * GLA_DK
    gv = GLA_HEADS * GLA_DV
    d_ff = ffn_w_gate.shape[2]

    meta = jnp.broadcast_to(meta_tokens[None], (bsz, N_META, d))
    xp = jnp.concatenate([meta, x_prompt], axis=1).reshape(bsz * length, d)
    xs = x_sample.reshape(db * dseq, d)
    n_p, n_s = xp.shape[0], xs.shape[0]

    tm_p = _pick_tile(n_p, (688, 512, 256, 128, 64, 32, 16, 8))
    tm_s = _pick_tile(n_s, (128, 64, 32, 16, 8))
    tf = _pick_tile(d_ff, (1408, 1024, 512, 256, 128))
    tq = _pick_tile(length, (344, 256, 128, 64, 32, 16, 8))
    pool_rc = _pick_tile(length, (48, 32, 16, 8))
    gla_tb = _pick_tile(length, (688, 512, 256, 128, 64, 32, 16))
    pp = _pick_tile(npg, (16, 8, 4, 2, 1))

    e_off = [0]
    for s in (a_width, kv_width, kv_width, IDX_HEADS * IDX_DIM, IDX_DIM, IDX_HEADS, pool_width):
        e_off.append(e_off[-1] + s)
    kiwi_pad = LANES - IDX_DIM - IDX_HEADS
    even_widths = (a_width, kv_width, kv_width, IDX_HEADS * IDX_DIM, pool_width, LANES)
    o_off = [0]
    for s in (gk, gk, gv, gv, GLA_GATE_RANK):
        o_off.append(o_off[-1] + s)
    odd_widths = (gk, gk, gv, gv, LANES)

    cache_kt = cache_k.transpose(0, 1, 3, 4, 2)
    cache_vt = cache_v.transpose(0, 1, 3, 4, 2)
    cache_kit = cache_kidx.transpose(0, 1, 3, 2)
    page_flat = page_table.reshape(-1).astype(I32)

    kp_l, vp_l, kip_l, bp_l, ks_l, vs_l, kis_l, bs_l, sp_l, ss_l = ([] for _ in range(10))
    for l in range(depth):
        if l % 2 == 0:
            e = l // 2
            w = w_in_even[e]
            w_in = jnp.concatenate(
                [w[:, e_off[0]:e_off[4]], w[:, e_off[6]:e_off[7]], w[:, e_off[4]:e_off[6]],
                 jnp.zeros((d, kiwi_pad), w.dtype)], axis=1).astype(BF16)
            w_out = w_out_even[e].astype(BF16)
            pw = pool_w[e].astype(BF16)

            q, k, v, qi, u, kiwi = norm_matmul(xp, norm_mix[l], w_in, even_widths, tm_p)
            r3 = lambda a: a.reshape(bsz, length, a.shape[1])
            oa = dsa_prompt(r3(q), r3(qi), r3(k), r3(v), r3(kiwi), tq, DSA_KEY_CHUNK)
            ob, buf = pool_mixer(r3(u), jnp.zeros((bsz, POOL_BUF + 1, pool_width), F32), pw,
                                 pool_scale[e], 0, pool_rc, BF16)
            mix_p = [(oa.reshape(n_p, a_width), w_out[:a_width]),
                     (ob.reshape(n_p, pool_width), w_out[a_width:])]
            kp_l.append(k.reshape(bsz, length, A_KV_HEADS, A_HEAD_DIM))
            vp_l.append(v.reshape(bsz, length, A_KV_HEADS, A_HEAD_DIM))
            kip_l.append(kiwi[:, :IDX_DIM].reshape(bsz, length, IDX_DIM))
            bp_l.append(buf)

            q, k, v, qi, u, kiwi = norm_matmul(xs, norm_mix[l], w_in, even_widths, tm_s)
            s3 = lambda a: a.reshape(db, dseq, a.shape[1])
            q_g = q.reshape(db, dseq, A_KV_HEADS, A_GROUP, A_HEAD_DIM).transpose(0, 2, 3, 1, 4)
            q_g = q_g.reshape(db, A_KV_HEADS, A_GROUP * dseq, A_HEAD_DIM)
            qi_t = qi.reshape(db, dseq * IDX_HEADS, IDX_DIM)
            wi_t = kiwi[:, IDX_DIM:IDX_DIM + IDX_HEADS].reshape(db, dseq * IDX_HEADS, 1)
            ki_new = kiwi[:, :IDX_DIM].reshape(db, dseq, IDX_DIM)
            oa = dsa_sample(q_g, qi_t, wi_t, ki_new, s3(k), s3(v), cache_kt, cache_vt, cache_kit,
                            page_flat, e, pp)
            oa = oa.reshape(db, A_KV_HEADS, A_GROUP, dseq, A_HEAD_DIM).transpose(0, 3, 1, 2, 4)
            prefix = jnp.pad(state_pool[e], ((0, 0), (1, 0), (0, 0)))
            ob, buf = pool_mixer(s3(u), prefix, pw, pool_scale[e], past, BF16_ROWS, F32)
            mix_s = [(oa.reshape(n_s, a_width), w_out[:a_width]),
                     (ob.reshape(n_s, pool_width), w_out[a_width:])]
            ks_l.append(k.reshape(db, dseq, A_KV_HEADS, A_HEAD_DIM))
            vs_l.append(v.reshape(db, dseq, A_KV_HEADS, A_HEAD_DIM))
            kis_l.append(ki_new)
            bs_l.append(buf)
        else:
            o = l // 2
            w = w_in_odd[o]
            w_in = jnp.concatenate([w, jnp.zeros((d, LANES - GLA_GATE_RANK), w.dtype)], axis=1).astype(BF16)
            w_out = w_out_odd[o].astype(BF16)
            wgu = jnp.pad(gla_w_gate_up[o], ((0, LANES - GLA_GATE_RANK), (0, 0)))

            q, k, v, g, gd = norm_matmul(xp, norm_mix[l], w_in, odd_widths, tm_p)
            r3 = lambda a: a.reshape(bsz, length, a.shape[1])
            on, s_p = gla_mixer(r3(q), r3(k), r3(v), r3(g), r3(gd), wgu, gla_b_gate[o], gla_norm[o],
                                None, None, gla_tb, GLA_CHUNK, None)
            mix_p = [(on.reshape(n_p, gv), w_out)]
            sp_l.append(s_p)

            q, k, v, g, gd = norm_matmul(xs, norm_mix[l], w_in, odd_widths, tm_s)
            tpad = _round_up(dseq, BF16_ROWS)
            p3 = lambda a: jnp.pad(a.reshape(db, dseq, a.shape[1]), ((0, 0), (0, tpad - dseq), (0, 0)))
            on, s_s = gla_mixer(p3(q), p3(k), p3(v), p3(g), p3(gd), wgu, gla_b_gate[o], gla_norm[o],
                                state_gla, o, tpad, tpad, dseq)
            mix_s = [(on[:, :dseq].reshape(n_s, gv), w_out)]
            ss_l.append(s_s)

        wg, wu, wd = ffn_w_gate[l].astype(BF16), ffn_w_up[l].astype(BF16), ffn_w_down[l].astype(BF16)
        xp = mix_ffn(xp, mix_p, norm_ffn[l], wg, wu, wd, tm_p, tf)
        xs = mix_ffn(xs, mix_s, norm_ffn[l], wg, wu, wd, tm_s, tf)

    y_prompt = final_norm(xp.reshape(bsz, length, d), norm_final, N_META)
    y_sample = final_norm(xs.reshape(1, n_s, d), norm_final, 0).reshape(db, dseq, d)
    return (y_prompt, y_sample,
            jnp.stack(kp_l), jnp.stack(vp_l), jnp.stack(kip_l),
            jnp.stack(ks_l), jnp.stack(vs_l), jnp.stack(kis_l),
            jnp.stack(bp_l), jnp.stack(bs_l),
            jnp.stack(sp_l), jnp.stack(ss_l))
```

```python
import functools

import jax
import jax.numpy as jnp
from jax import lax
from jax.experimental import pallas as pl
from jax.experimental.pallas import tpu as pltpu

F32 = jnp.float32
BF16 = jnp.bfloat16
I32 = jnp.int32

NORM_EPS = 1e-6
N_META = 16
A_HEADS = 8
A_KV_HEADS = 2
A_GROUP = A_HEADS // A_KV_HEADS
A_HEAD_DIM = 64
IDX_HEADS = 8
IDX_DIM = 64
TOPK_MAX = 256
PAGE_SIZE = 128
POOL_WINDOWS = (2, 4, 8, 16)
POOL_GROUP_DIM = 128
POOL_BUF = 15
GLA_HEADS = 4
GLA_DK = 128
GLA_DV = 256
GLA_GATE_RANK = 16
GLA_GATE_NORM = 16.0

LANES = 128
SUBLANES = 8
BF16_ROWS = 16
VMEM_LIMIT_BYTES = 56 * 1024 * 1024

DSA_KEY_CHUNK = 2 * LANES
GLA_CHUNK = 96

INT_MIN = -(2 ** 31)
MASK_BIAS = -1e30
HIGHEST = lax.Precision.HIGHEST
QK_SCALE_LOG2 = (A_HEAD_DIM ** -0.5) * 1.4426950408889634


def _round_up(n, m):
    return (n + m - 1) // m * m


def _cparams(*sem):
    return pltpu.CompilerParams(dimension_semantics=sem, vmem_limit_bytes=VMEM_LIMIT_BYTES)


def _rms(x, g):
    ms = jnp.mean(x * x, axis=-1, keepdims=True)
    return (x * lax.rsqrt(ms + NORM_EPS)) * g


def _dot_nt(a, b):
    return lax.dot_general(a, b, (((1,), (1,)), ((), ())), preferred_element_type=F32)


def _dot_tn(a, b):
    return lax.dot_general(a, b, (((0,), (0,)), ((), ())), preferred_element_type=F32)


def _norm_matmul_kernel(x_ref, g_ref, w_ref, *o_refs, widths):
    h = _rms(x_ref[...], g_ref[...]).astype(BF16)
    off = 0
    for o_ref, wd in zip(o_refs, widths):
        o_ref[...] = jnp.dot(h, w_ref[:, off:off + wd], preferred_element_type=F32)
        off += wd


def norm_matmul(x, g, w, widths, tm):
    n, d = x.shape
    assert n % tm == 0 and sum(widths) == w.shape[1]
    return pl.pallas_call(
        functools.partial(_norm_matmul_kernel, widths=widths),
        grid=(n // tm,),
        in_specs=[pl.BlockSpec((tm, d), lambda i: (i, 0)),
                  pl.BlockSpec((1, d), lambda i: (0, 0)),
                  pl.BlockSpec((d, w.shape[1]), lambda i: (0, 0))],
        out_specs=[pl.BlockSpec((tm, wd), lambda i: (i, 0)) for wd in widths],
        out_shape=[jax.ShapeDtypeStruct((n, wd), F32) for wd in widths],
        compiler_params=_cparams("parallel"),
        name="norm_matmul",
    )(x, g.reshape(1, d), w)


def _mix_ffn_kernel(*refs, n_mix):
    x_ref = refs[0]
    mix = refs[1:1 + 2 * n_mix]
    g_ref, wg_ref, wu_ref, wd_ref, o_ref, h_s = refs[1 + 2 * n_mix:]
    j = pl.program_id(1)

    @pl.when(j == 0)
    def _():
        xm = x_ref[...]
        for m in range(n_mix):
            a_ref, w_ref = mix[2 * m], mix[2 * m + 1]
            xm = xm + jnp.dot(a_ref[...].astype(BF16), w_ref[...], preferred_element_type=F32)
        o_ref[...] = xm
        h_s[...] = _rms(xm, g_ref[...]).astype(BF16)

    h = h_s[...]
    gate = jnp.dot(h, wg_ref[...], preferred_element_type=F32)
    up = jnp.dot(h, wu_ref[...], preferred_element_type=F32)
    act = (gate * jax.nn.sigmoid(gate) * up).astype(BF16)
    o_ref[...] += jnp.dot(act, wd_ref[...], preferred_element_type=F32)


def mix_ffn(x, mixes, g, wg, wu, wd, tm, tf):
    n, d = x.shape
    f = wg.shape[1]
    assert n % tm == 0 and f % tf == 0
    in_specs = [pl.BlockSpec((tm, d), lambda i, j: (i, 0))]
    args = [x]
    for a, w in mixes:
        in_specs += [pl.BlockSpec((tm, a.shape[1]), lambda i, j: (i, 0)),
                     pl.BlockSpec(w.shape, lambda i, j: (0, 0))]
        args += [a, w]
    in_specs += [pl.BlockSpec((1, d), lambda i, j: (0, 0)),
                 pl.BlockSpec((d, tf), lambda i, j: (0, j)),
                 pl.BlockSpec((d, tf), lambda i, j: (0, j)),
                 pl.BlockSpec((tf, d), lambda i, j: (j, 0))]
    args += [g.reshape(1, d), wg, wu, wd]
    return pl.pallas_call(
        functools.partial(_mix_ffn_kernel, n_mix=len(mixes)),
        grid=(n // tm, f // tf),
        in_specs=in_specs,
        out_specs=pl.BlockSpec((tm, d), lambda i, j: (i, 0)),
        out_shape=jax.ShapeDtypeStruct((n, d), F32),
        scratch_shapes=[pltpu.VMEM((tm, d), BF16)],
        compiler_params=_cparams("parallel", "arbitrary"),
        name="mix_ffn",
    )(*args)


def _final_norm_kernel(x_ref, g_ref, o_ref, *, skip, rows, chunk):
    for c in range(rows // chunk):
        x = x_ref[0, skip + c * chunk: skip + (c + 1) * chunk, :]
        o_ref[0, c * chunk:(c + 1) * chunk, :] = _rms(x, g_ref[...])


def final_norm(x, g, skip):
    b, t, d = x.shape
    rows = t - skip
    chunk = 256 if rows % 256 == 0 else rows
    return pl.pallas_call(
        functools.partial(_final_norm_kernel, skip=skip, rows=rows, chunk=chunk),
        grid=(b,),
        in_specs=[pl.BlockSpec((1, t, d), lambda i: (i, 0, 0)),
                  pl.BlockSpec((1, d), lambda i: (0, 0))],
        out_specs=pl.BlockSpec((1, rows, d), lambda i: (i, 0, 0)),
        out_shape=jax.ShapeDtypeStruct((b, rows, d), F32),
        compiler_params=_cparams("parallel"),
        name="final_norm",
    )(x, g.reshape(1, d))


def _pool_kernel(u_ref, pre_ref, pw_ref, sc_ref, ob_ref, buf_ref, ext_s, *, t, pos0, rc):
    head = POOL_BUF + 1
    tproc = _round_up(t, rc)
    ext_s[0:head, :] = pre_ref[0]
    ext_s[head:head + t, :] = u_ref[0]
    if tproc > t:
        ext_s[head + t:head + tproc, :] = jnp.zeros((tproc - t, ext_s.shape[1]), F32)
    buf_ref[0] = ext_s[t + 1:t + head, :]
    for c in range(tproc // rc):
        r0 = c * rc
        nst = min(rc, t - r0)
        pos = pos0 + r0 + lax.broadcasted_iota(I32, (rc, 1), 0)
        for gi, w in enumerate(POOL_WINDOWS):
            sl = slice(gi * POOL_GROUP_DIM, (gi + 1) * POOL_GROUP_DIM)
            xt = ext_s[head + r0:head + r0 + rc, sl]
            acc = xt
            for r in range(1, w):
                acc = acc + ext_s[head + r0 - r:head + r0 - r + rc, sl]
            cnt = jnp.minimum(w, pos + 1).astype(F32)
            dlt = acc / cnt - xt
            y = jnp.dot(dlt.astype(BF16), pw_ref[gi], preferred_element_type=F32) * sc_ref[:, sl]
            ob_ref[0, r0:r0 + nst, sl] = y[0:nst].astype(ob_ref.dtype)


def pool_mixer(u, prefix16, pw, scale, pos0, rc, out_dtype):
    b, t, c = u.shape
    head = POOL_BUF + 1
    return pl.pallas_call(
        functools.partial(_pool_kernel, t=t, pos0=pos0, rc=rc),
        grid=(b,),
        in_specs=[pl.BlockSpec((1, t, c), lambda i: (i, 0, 0)),
                  pl.BlockSpec((1, head, c), lambda i: (i, 0, 0)),
                  pl.BlockSpec(pw.shape, lambda i: (0, 0, 0)),
                  pl.BlockSpec((1, c), lambda i: (0, 0))],
        out_specs=[pl.BlockSpec((1, t, c), lambda i: (i, 0, 0)),
                   pl.BlockSpec((1, POOL_BUF, c), lambda i: (i, 0, 0))],
        out_shape=[jax.ShapeDtypeStruct((b, t, c), out_dtype),
                   jax.ShapeDtypeStruct((b, POOL_BUF, c), F32)],
        scratch_shapes=[pltpu.VMEM((head + _round_up(t, rc), c), F32)],
        compiler_params=_cparams("parallel"),
        name="pool_mixer",
    )(u, prefix16, pw, scale.reshape(1, c))


KEY_NEG_INF = INT_MIN + 0x7FFFFF


def _key_to_float(key):
    bits = key ^ ((key >> 31) & 0x7FFFFFFF)
    return jnp.where(key <= KEY_NEG_INF, -jnp.inf, lax.bitcast_convert_type(bits, F32))


def _lane_tile(x, n):
    return x if n == 1 else jnp.concatenate([x] * n, axis=1)


def _row_total(x):
    return jnp.broadcast_to(jnp.sum(x, axis=-1, keepdims=True), x.shape)


def _dsa_prompt_kernel(q_ref, qi_ref, k_ref, v_ref, kiwi_ref, o_ref,
                       kb_s, vb_s, kib_s, qb_s, qib_s, wib_s, sc_s, sct_s, m_s, acc_s,
                       *, length, tq, kc, topk):
    nkc = kib_s.shape[0]
    nl = kc // LANES
    tqp = sct_s.shape[2]
    tqb = qb_s.shape[1] // A_GROUP
    it = pl.program_id(1)
    r0 = it * tq
    nk = (r0 + tq + kc - 1) // kc
    kf = float(topk)

    def pad_rows(x, n):
        return x if x.shape[0] == n else jnp.concatenate([x, jnp.zeros((n - x.shape[0], x.shape[1]), x.dtype)], axis=0)

    @pl.when(it == 0)
    def _():
        ones_col = (lax.broadcasted_iota(I32, (kc, LANES - A_HEAD_DIM), 1) == 0).astype(BF16)
        for c in range(nkc):
            lo = c * kc
            valid = min(kc, length - lo)
            k_t = pad_rows(k_ref[0, lo:lo + valid, :], kc).T
            ki_t = pad_rows(kiwi_ref[0, lo:lo + valid, :], kc).T
            kib_s[c] = ki_t[0:IDX_DIM].astype(BF16)
            for kh in range(A_KV_HEADS):
                sl = slice(kh * A_HEAD_DIM, (kh + 1) * A_HEAD_DIM)
                kb_s[kh, c] = k_t[sl].astype(BF16)
                vb_s[kh, c, :, 0:A_HEAD_DIM] = pad_rows(v_ref[0, lo:lo + valid, sl], kc).astype(BF16)
                vb_s[kh, c, :, A_HEAD_DIM:LANES] = ones_col

    kiwi_t = pad_rows(kiwi_ref[0, pl.ds(pl.multiple_of(r0, SUBLANES), tq), :], tqp)
    for h in range(IDX_HEADS):
        qih = pad_rows(qi_ref[0, :, h * IDX_DIM:(h + 1) * IDX_DIM], tqp)
        qib_s[h * tqp:(h + 1) * tqp, :] = qih.astype(BF16)
        wib_s[h] = jnp.broadcast_to(kiwi_t[:, IDX_DIM + h:IDX_DIM + h + 1], (tqp, LANES))
    for hd in range(A_HEADS):
        kh, g = divmod(hd, A_GROUP)
        qh = pad_rows(q_ref[0, :, hd * A_HEAD_DIM:(hd + 1) * A_HEAD_DIM] * QK_SCALE_LOG2, tqb)
        qb_s[kh, g * tqb:(g + 1) * tqb, :] = qh.astype(BF16)
        m_s[hd] = jnp.full((tq, LANES), MASK_BIAS, F32)
        acc_s[hd] = jnp.zeros((tq, LANES), F32)

    def score_chunk(c, carry):
        s_all = jnp.dot(qib_s[...], kib_s[c], preferred_element_type=F32)
        score = jnp.zeros((tqp, kc), F32)
        for h in range(IDX_HEADS):
            score = score + jnp.maximum(s_all[h * tqp:(h + 1) * tqp], 0.0) * _lane_tile(wib_s[h], nl)
        col = c * kc + lax.broadcasted_iota(I32, (tqp, kc), 1)
        pos = r0 + lax.broadcasted_iota(I32, (tqp, kc), 0)
        vis_score = jnp.where(col <= pos, score, -jnp.inf)
        sc_s[c] = vis_score
        sct_s[c] = vis_score.T
        return carry

    lax.fori_loop(0, nk, score_chunk, 0)

    def count(pred):
        def body(c, acc):
            hit = pred(sct_s[c], c).astype(F32)
            return acc + jnp.sum(hit.reshape(kc // SUBLANES, SUBLANES, tqp), axis=0)
        acc = lax.fori_loop(0, nk, body, jnp.zeros((SUBLANES, tqp), F32))
        return jnp.sum(acc, axis=0, keepdims=True)

    t0 = jnp.where(count(lambda s, c: s >= 0.0) >= kf, 0, INT_MIN).astype(I32)

    def bit_step(b, t):
        cand = t + lax.shift_left(jnp.int32(1), 30 - b)
        cand_f = _key_to_float(cand)
        return jnp.where(count(lambda s, c: s >= cand_f) >= kf, cand, t)

    thr_key = lax.fori_loop(0, 31, bit_step, t0)
    thr_q = _key_to_float(thr_key)
    tie_q = (count(lambda s, c: s >= thr_q) > kf) & (thr_key > KEY_NEG_INF)

    ncols = nkc * kc
    nbits = max(1, (ncols - 1).bit_length())

    def break_ties():
        need = kf - count(lambda s, c: s > thr_q)
        kidx = lax.broadcasted_iota(I32, (kc, tqp), 0)

        def idx_step(b, jc):
            cand = jc + lax.shift_left(jnp.int32(1), nbits - 1 - b)
            n_lt = count(lambda s, c: (s == thr_q) & (c * kc + kidx < cand))
            return jnp.where(n_lt < need, cand, jc)

        jc = lax.fori_loop(0, nbits, idx_step, jnp.zeros((1, tqp), I32))
        return jnp.where(tie_q, jc, ncols)

    jcut_q = lax.cond(jnp.max(tie_q.astype(F32)) > 0.0, break_ties,
                      lambda: jnp.full((1, tqp), ncols, I32))

    def per_row(x_q):
        return jnp.broadcast_to(x_q, (LANES, tqp)).T[0:tq]

    thr_t = _lane_tile(per_row(thr_q), nl)
    jcut_t = _lane_tile(per_row(jcut_q), nl)

    def attend_chunk(c, carry):
        sc = sc_s[c, 0:tq, :]
        col = c * kc + lax.broadcasted_iota(I32, (tq, kc), 1)
        pos = r0 + lax.broadcasted_iota(I32, (tq, kc), 0)
        sel = (col <= pos) & ((sc > thr_t) | ((sc == thr_t) & (col <= jcut_t)))
        bias = jnp.where(sel, 0.0, MASK_BIAS).astype(F32)
        for kh in range(A_KV_HEADS):
            l_all = jnp.dot(qb_s[kh], kb_s[kh, c], preferred_element_type=F32)
            es, stats = [], []
            for g in range(A_GROUP):
                hd = kh * A_GROUP + g
                logits = l_all[g * tqb:g * tqb + tq] + bias
                m_old = m_s[hd]
                m_new = jnp.maximum(m_old, jnp.broadcast_to(jnp.max(logits, axis=-1, keepdims=True), (tq, LANES)))
                es.append(pad_rows(jnp.exp2(logits - _lane_tile(m_new, nl)), tqb).astype(BF16))
                stats.append((hd, m_old, m_new))
            pv_all = jnp.dot(jnp.concatenate(es, axis=0), vb_s[kh, c], preferred_element_type=F32)
            for g, (hd, m_old, m_new) in enumerate(stats):
                acc_s[hd] = jnp.exp2(m_old - m_new) * acc_s[hd] + pv_all[g * tqb:g * tqb + tq]
                m_s[hd] = m_new
        return carry

    lax.fori_loop(0, nk, attend_chunk, 0)

    for hd in range(A_HEADS):
        acc = acc_s[hd]
        o_ref[0, :, hd * A_HEAD_DIM:(hd + 1) * A_HEAD_DIM] = (
            acc[:, 0:A_HEAD_DIM] / acc[:, A_HEAD_DIM:A_HEAD_DIM + 1])


def dsa_prompt(q, qi, k, v, kiwi, tq, kc):
    b, length, _ = q.shape
    assert length % tq == 0 and kc % LANES == 0
    topk = min(TOPK_MAX, length // 4)
    nkc = -(-length // kc)
    tqp = _round_up(tq, LANES)
    tqb = _round_up(tq, BF16_ROWS)
    seq = lambda c: pl.BlockSpec((1, length, c), lambda i, j: (i, 0, 0))
    til = lambda c: pl.BlockSpec((1, tq, c), lambda i, j: (i, j, 0))
    return pl.pallas_call(
        functools.partial(_dsa_prompt_kernel, length=length, tq=tq, kc=kc, topk=topk),
        grid=(b, length // tq),
        in_specs=[til(q.shape[2]), til(qi.shape[2]), seq(k.shape[2]), seq(v.shape[2]), seq(kiwi.shape[2])],
        out_specs=til(q.shape[2]),
        out_shape=jax.ShapeDtypeStruct(q.shape, F32),
        scratch_shapes=[pltpu.VMEM((A_KV_HEADS, nkc, A_HEAD_DIM, kc), BF16),
                        pltpu.VMEM((A_KV_HEADS, nkc, kc, LANES), BF16),
                        pltpu.VMEM((nkc, IDX_DIM, kc), BF16),
                        pltpu.VMEM((A_KV_HEADS, A_GROUP * tqb, A_HEAD_DIM), BF16),
                        pltpu.VMEM((IDX_HEADS * tqp, IDX_DIM), BF16),
                        pltpu.VMEM((IDX_HEADS, tqp, LANES), F32),
                        pltpu.VMEM((nkc, tqp, kc), F32),
                        pltpu.VMEM((nkc, kc, tqp), F32),
                        pltpu.VMEM((A_HEADS, tq, LANES), F32),
                        pltpu.VMEM((A_HEADS, tq, LANES), F32)],
        compiler_params=_cparams("parallel", "arbitrary"),
        name="dsa_prompt",
    )(q, qi, k, v, kiwi)


def _dsa_sample_score_kernel(pt_ref, qi_ref, wi_ref, kinew_ref, *rest, pp, t):
    pages = rest[:pp]
    sc_ref, kin_s = rest[pp:]
    j = pl.program_id(1)
    nsteps = pl.num_programs(1)
    cw = pp * PAGE_SIZE
    past = sc_ref.shape[2] - LANES
    qi = qi_ref[0].astype(BF16)
    wi = wi_ref[0]

    def scores(s):
        s = jnp.maximum(s, 0.0) * wi
        return jnp.sum(s.reshape(t, IDX_HEADS, s.shape[1]), axis=1)

    ki_t = jnp.concatenate([p[0, 0] for p in pages], axis=1).astype(BF16)
    sc_chunk = scores(jnp.dot(qi, ki_t, preferred_element_type=F32))
    for jj in range(past // cw):
        @pl.when(j == jj)
        def _(jj=jj):
            sc_ref[0, :, jj * cw:(jj + 1) * cw] = sc_chunk

    @pl.when(j == nsteps - 1)
    def _():
        kin_s[...] = jnp.zeros(kin_s.shape, F32)
        kin_s[0:t, :] = kinew_ref[0]
        snew = scores(_dot_nt(qi, kin_s[...].astype(BF16)))
        cn = lax.broadcasted_iota(I32, (t, LANES), 1)
        rn = lax.broadcasted_iota(I32, (t, LANES), 0)
        sc_ref[0, :, past:past + LANES] = jnp.where(cn <= rn, snew, -jnp.inf)


def _dsa_select_kernel(sc_ref, thr_ref, jcut_ref, *, topk):
    rows, ncols = sc_ref.shape
    kf = float(topk)

    def count(pred):
        acc = jnp.zeros((rows, LANES), F32)
        for c0 in range(0, ncols, LANES):
            acc = acc + pred(sc_ref[:, c0:c0 + LANES], c0).astype(F32)
        return _row_total(acc)

    t0 = jnp.where(count(lambda s, c0: s >= 0.0) >= kf, 0, INT_MIN).astype(I32)

    def bit_step(b, t):
        cand = t + lax.shift_left(jnp.int32(1), 30 - b)
        cand_f = _key_to_float(cand)
        return jnp.where(count(lambda s, c0: s >= cand_f) >= kf, cand, t)

    thr_key = lax.fori_loop(0, 31, bit_step, t0)
    thr = _key_to_float(thr_key)
    tie = (count(lambda s, c0: s >= thr) > kf) & (thr_key > KEY_NEG_INF)
    nbits = max(1, (ncols - 1).bit_length())

    def break_ties():
        need = kf - count(lambda s, c0: s > thr)
        lane = lax.broadcasted_iota(I32, (rows, LANES), 1)

        def idx_step(b, jc):
            cand = jc + lax.shift_left(jnp.int32(1), nbits - 1 - b)
            n_lt = count(lambda s, c0: (s == thr) & (c0 + lane < cand))
            return jnp.where(n_lt < need, cand, jc)

        jc = lax.fori_loop(0, nbits, idx_step, jnp.zeros((rows, LANES), I32))
        return jnp.where(tie, jc, ncols)

    thr_ref[...] = thr
    jcut_ref[...] = lax.cond(jnp.max(tie.astype(F32)) > 0.0, break_ties,
                             lambda: jnp.full((rows, LANES), ncols, I32))


def _dsa_sample_attn_kernel(pt_ref, q_ref, sc_ref, scnew_ref, thr_ref, jcut_ref, knew_ref, vnew_ref,
                            *rest, pp, t):
    kpages = rest[:pp]
    vpages = rest[pp:2 * pp]
    o_ref, m_s, l_s, acc_s, new_s = rest[2 * pp:]
    j = pl.program_id(1)
    nsteps = pl.num_programs(1)
    cw = pp * PAGE_SIZE

    def bias_of(sc, c0):
        n = sc.shape[1] // LANES
        thr, jcut = _lane_tile(thr_ref[0], n), _lane_tile(jcut_ref[0], n)
        col = c0 + lax.broadcasted_iota(I32, sc.shape, 1)
        sel = (sc > -jnp.inf) & ((sc > thr) | ((sc == thr) & (col <= jcut)))
        bias = jnp.where(sel, 0.0, MASK_BIAS).astype(F32)
        return jnp.concatenate([bias] * A_GROUP, axis=0)

    @pl.when(j == 0)
    def _():
        m_s[...] = jnp.full(m_s.shape, MASK_BIAS, F32)
        l_s[...] = jnp.zeros(l_s.shape, F32)
        acc_s[...] = jnp.zeros(acc_s.shape, F32)

    def update(kh, kb, vb, bias, key_major):
        qh = (q_ref[0, kh] * (A_HEAD_DIM ** -0.5)).astype(BF16)
        qk = _dot_nt(qh, kb) if key_major else jnp.dot(qh, kb, preferred_element_type=F32)
        logits = qk + bias
        m_old = m_s[kh]
        m_new = jnp.maximum(m_old, jnp.max(logits, axis=-1, keepdims=True))
        alpha = jnp.exp(m_old - m_new)
        e = jnp.exp(logits - m_new)
        p = e.astype(BF16)
        pv = jnp.dot(p, vb, preferred_element_type=F32) if key_major else _dot_nt(p, vb)
        l_s[kh] = alpha * l_s[kh] + jnp.sum(e, axis=-1, keepdims=True)
        acc_s[kh] = alpha * acc_s[kh] + pv
        m_s[kh] = m_new

    bias = bias_of(sc_ref[0], j * cw)
    for kh in range(A_KV_HEADS):
        k_t = jnp.concatenate([p[0, 0, kh] for p in kpages], axis=1).astype(BF16)
        v_t = jnp.concatenate([p[0, 0, kh] for p in vpages], axis=1).astype(BF16)
        update(kh, k_t, v_t, bias, False)

    @pl.when(j == nsteps - 1)
    def _():
        bias_new = bias_of(scnew_ref[0], nsteps * cw)
        for kh in range(A_KV_HEADS):
            sl = slice(kh * A_HEAD_DIM, (kh + 1) * A_HEAD_DIM)
            new_s[...] = jnp.zeros(new_s.shape, F32)
            new_s[0, 0:t, :] = knew_ref[0, :, sl]
            new_s[1, 0:t, :] = vnew_ref[0, :, sl]
            update(kh, new_s[0].astype(BF16), new_s[1].astype(BF16), bias_new, True)
            o_ref[0, kh] = acc_s[kh] / l_s[kh]


def dsa_sample(q_g, qi_t, wi_t, ki_new, k_new, v_new, cache_k, cache_v, cache_ki, page_flat, layer, pp):
    db, _, gt, hd = q_g.shape
    t = gt // A_GROUP
    npg = page_flat.shape[0] // db
    assert npg % pp == 0
    nsteps = npg // pp
    past = npg * PAGE_SIZE
    ncols = past + LANES
    cw = pp * PAGE_SIZE
    kvw = k_new.shape[2]

    def page_spec(shape, i):
        nz = (0,) * len(shape)
        return pl.BlockSpec((1, 1) + shape, lambda b, j, pt, i=i: (layer, pt[b * npg + j * pp + i]) + nz)

    ki_page = (IDX_DIM, PAGE_SIZE)
    kv_page = (A_KV_HEADS, hd, PAGE_SIZE)

    sc = pl.pallas_call(
        functools.partial(_dsa_sample_score_kernel, pp=pp, t=t),
        grid_spec=pltpu.PrefetchScalarGridSpec(
            num_scalar_prefetch=1,
            grid=(db, nsteps),
            in_specs=[pl.BlockSpec((1, t * IDX_HEADS, IDX_DIM), lambda b, j, pt: (b, 0, 0)),
                      pl.BlockSpec((1, t * IDX_HEADS, 1), lambda b, j, pt: (b, 0, 0)),
                      pl.BlockSpec((1, t, IDX_DIM), lambda b, j, pt: (b, 0, 0))]
                     + [page_spec(ki_page, i) for i in range(pp)],
            out_specs=pl.BlockSpec((1, t, ncols), lambda b, j, pt: (b, 0, 0)),
            scratch_shapes=[pltpu.VMEM((LANES, IDX_DIM), F32)]),
        out_shape=jax.ShapeDtypeStruct((db, t, ncols), F32),
        compiler_params=_cparams("parallel", "arbitrary"),
        name="dsa_sample_score",
    )(page_flat, qi_t, wi_t, ki_new, *([cache_ki] * pp))

    thr, jcut = pl.pallas_call(
        functools.partial(_dsa_select_kernel, topk=min(TOPK_MAX, (past + t) // 4)),
        out_shape=[jax.ShapeDtypeStruct((db * t, LANES), F32), jax.ShapeDtypeStruct((db * t, LANES), I32)],
        compiler_params=pltpu.CompilerParams(vmem_limit_bytes=VMEM_LIMIT_BYTES),
        name="dsa_sample_select",
    )(sc.reshape(db * t, ncols))
    thr, jcut = thr.reshape(db, t, LANES), jcut.reshape(db, t, LANES)

    out = pl.pallas_call(
        functools.partial(_dsa_sample_attn_kernel, pp=pp, t=t),
        grid_spec=pltpu.PrefetchScalarGridSpec(
            num_scalar_prefetch=1,
            grid=(db, nsteps),
            in_specs=[pl.BlockSpec((1, A_KV_HEADS, gt, hd), lambda b, j, pt: (b, 0, 0, 0)),
                      pl.BlockSpec((1, t, cw), lambda b, j, pt: (b, 0, j)),
                      pl.BlockSpec((1, t, LANES), lambda b, j, pt: (b, 0, past // LANES)),
                      pl.BlockSpec((1, t, LANES), lambda b, j, pt: (b, 0, 0)),
                      pl.BlockSpec((1, t, LANES), lambda b, j, pt: (b, 0, 0)),
                      pl.BlockSpec((1, t, kvw), lambda b, j, pt: (b, 0, 0)),
                      pl.BlockSpec((1, t, kvw), lambda b, j, pt: (b, 0, 0))]
                     + [page_spec(kv_page, i) for i in range(pp)]
                     + [page_spec(kv_page, i) for i in range(pp)],
            out_specs=pl.BlockSpec((1, A_KV_HEADS, gt, hd), lambda b, j, pt: (b, 0, 0, 0)),
            scratch_shapes=[pltpu.VMEM((A_KV_HEADS, gt, 1), F32),
                            pltpu.VMEM((A_KV_HEADS, gt, 1), F32),
                            pltpu.VMEM((A_KV_HEADS, gt, hd), F32),
                            pltpu.VMEM((2, LANES, hd), F32)]),
        out_shape=jax.ShapeDtypeStruct((db, A_KV_HEADS, gt, hd), F32),
        compiler_params=_cparams("parallel", "arbitrary"),
        name="dsa_sample_attn",
    )(page_flat, q_g, sc, sc, thr, jcut, k_new, v_new, *([cache_k] * pp), *([cache_v] * pp))
    return out


def _gla_kernel(*refs, tb, tv, chunk, sub, has_s0):
    if has_s0:
        (q_ref, k_ref, v_ref, g_ref, gd_ref, wgu_ref, bg_ref, gn_ref, s0_ref,
         on_ref, sout_ref, b_s, dec_s, st_s) = refs
    else:
        q_ref, k_ref, v_ref, g_ref, gd_ref, wgu_ref, bg_ref, gn_ref, on_ref, sout_ref, b_s, dec_s, st_s = refs
    jb = pl.program_id(1)
    nb = pl.num_programs(1)
    nfull, rem = tb // chunk, tb % chunk
    spans = [(ic * chunk, chunk) for ic in range(nfull)] + ([(nfull * chunk, rem)] if rem else [])

    @pl.when(jb == 0)
    def _():
        for h in range(GLA_HEADS):
            st_s[h] = s0_ref[0, 0, h] if has_s0 else jnp.zeros((GLA_DK, GLA_DV), F32)

    x = jnp.dot(gd_ref[0], wgu_ref[...], precision=HIGHEST, preferred_element_type=F32) + bg_ref[...]
    la = (jnp.minimum(x, 0.0) - jnp.log1p(jnp.exp(-jnp.abs(x)))) * (1.0 / GLA_GATE_NORM)
    if tv is not None:
        row = jb * tb + lax.broadcasted_iota(I32, (tb, 1), 0)
        la = jnp.where(row < tv, la, 0.0)
    for ic, (r0, c) in enumerate(spans):
        lac = la[r0:r0 + c]
        tri = (lax.broadcasted_iota(I32, (c, c), 0) >= lax.broadcasted_iota(I32, (c, c), 1)).astype(BF16)
        hi = lac.astype(BF16)
        r1 = lac - hi.astype(F32)
        mid = r1.astype(BF16)
        lo = (r1 - mid.astype(F32)).astype(BF16)
        bc = (jnp.dot(tri, hi, preferred_element_type=F32) + jnp.dot(tri, mid, preferred_element_type=F32)
              + jnp.dot(tri, lo, preferred_element_type=F32))
        b_s[r0:r0 + c, :] = bc
        dec_s[ic] = jnp.exp(jnp.broadcast_to(bc[c - 1:c, :], (LANES, bc.shape[1])).T)

    def do_chunk(ic, r0, c):
        rows = pl.ds(r0, c)
        b = b_s[rows, :]
        qc = q_ref[0, rows, :] * (GLA_DK ** -0.5)
        kc = k_ref[0, rows, :]
        if tv is not None:
            rowc = jb * tb + r0 + lax.broadcasted_iota(I32, (c, 1), 0)
            kc = jnp.where(rowc < tv, kc, 0.0)
        vc = v_ref[0, rows, :].astype(BF16)
        gc = g_ref[0, rows, :]
        dec = dec_s[ic]
        causal = lax.broadcasted_iota(I32, (c, LANES), 1) <= lax.broadcasted_iota(I32, (c, LANES), 0)
        heads, states = [], []
        for h in range(GLA_HEADS):
            sk = slice(h * GLA_DK, (h + 1) * GLA_DK)
            sv = slice(h * GLA_DV, (h + 1) * GLA_DV)
            bh, qh, kh, vh = b[:, sk], qc[:, sk], kc[:, sk], vc[:, sv]
            st = st_s[h]
            att = []
            for i in range(c // sub):
                lo, hi = i * sub, (i + 1) * sub
                ref_b = bh[lo - 1:lo, :] if i > 0 else jnp.zeros((1, GLA_DK), F32)
                qt = (qh[lo:hi] * jnp.exp(bh[lo:hi] - ref_b)).astype(BF16)
                kt = (kh[0:hi] * jnp.exp(ref_b - bh[0:hi])).astype(BF16)
                kt = jnp.concatenate([kt, jnp.zeros((LANES - hi, GLA_DK), BF16)], axis=0)
                att.append(_dot_nt(qt, kt))
            att = jnp.concatenate(att, axis=0) if len(att) > 1 else att[0]
            att = jnp.where(causal, att, 0.0).astype(BF16)
            lhs = jnp.concatenate([(qh * jnp.exp(bh)).astype(BF16), att], axis=1)
            vpad = vh if c == LANES else jnp.concatenate([vh, jnp.zeros((LANES - c, GLA_DV), BF16)], axis=0)
            rhs = jnp.concatenate([st.astype(BF16), vpad], axis=0)
            o = jnp.dot(lhs, rhs, preferred_element_type=F32)
            bl = bh[c - 1:c, :]
            kd = (kh * jnp.exp(bl - bh)).astype(BF16)
            dh = dec[sk, :]
            states.append(jnp.concatenate([dh, dh], axis=1) * st + _dot_tn(kd, vh))
            gh = gc[:, sv]
            heads.append(_rms(o, gn_ref[...]) * (gh * jax.nn.sigmoid(gh)))
        on_ref[0, rows, :] = jnp.concatenate(heads, axis=1).astype(on_ref.dtype)
        for h in range(GLA_HEADS):
            st_s[h] = states[h]

    if nfull == 1:
        do_chunk(0, 0, chunk)
    elif nfull > 1:
        def body(ic, carry):
            do_chunk(ic, pl.multiple_of(ic * chunk, BF16_ROWS), chunk)
            return carry
        lax.fori_loop(0, nfull, body, 0)
    if rem:
        do_chunk(nfull, nfull * chunk, rem)

    @pl.when(jb == nb - 1)
    def _():
        for h in range(GLA_HEADS):
            sout_ref[0, h] = st_s[h]


def gla_mixer(q, k, v, g, gd, wgu, bg, gn, s0, s0_layer, tb, chunk, tv):
    b, t, _ = q.shape
    sub = BF16_ROWS
    assert t % tb == 0 and tb % sub == 0 and chunk % sub == 0 and chunk <= LANES and GLA_DV == 2 * LANES
    nspans = -(-tb // chunk)
    blk = lambda c: pl.BlockSpec((1, tb, c), lambda i, j: (i, j, 0))
    cst = lambda a: pl.BlockSpec(a.shape, lambda i, j: (0,) * a.ndim)
    st_spec = pl.BlockSpec((1, GLA_HEADS, GLA_DK, GLA_DV), lambda i, j: (i, 0, 0, 0))
    bg2, gn2 = bg.reshape(1, -1), gn.reshape(1, -1)
    args = [q, k, v, g, gd, wgu, bg2, gn2]
    in_specs = [blk(q.shape[2]), blk(k.shape[2]), blk(v.shape[2]), blk(g.shape[2]), blk(gd.shape[2]),
                cst(wgu), cst(bg2), cst(gn2)]
    if s0 is not None:
        args.append(s0)
        in_specs.append(pl.BlockSpec((1, 1, GLA_HEADS, GLA_DK, GLA_DV), lambda i, j: (s0_layer, i, 0, 0, 0)))
    return pl.pallas_call(
        functools.partial(_gla_kernel, tb=tb, tv=tv, chunk=chunk, sub=sub, has_s0=s0 is not None),
        grid=(b, t // tb),
        in_specs=in_specs,
        out_specs=[blk(v.shape[2]), st_spec],
        out_shape=[jax.ShapeDtypeStruct(v.shape, BF16),
                   jax.ShapeDtypeStruct((b, GLA_HEADS, GLA_DK, GLA_DV), F32)],
        scratch_shapes=[pltpu.VMEM((tb, GLA_HEADS * GLA_DK), F32),
                        pltpu.VMEM((nspans, GLA_HEADS * GLA_DK, LANES), F32),
                        pltpu.VMEM((GLA_HEADS, GLA_DK, GLA_DV), F32)],
        compiler_params=_cparams("parallel", "arbitrary"),
        name="gla_mixer",
    )(*args)


def _pick_tile(n, candidates):
    for c in candidates:
        if n % c == 0:
            return c
    return n


def kernel(x_prompt, x_sample, cache_k, cache_v, cache_kidx, state_pool, state_gla, page_table,
           meta_tokens, norm_mix, norm_ffn, norm_final, w_in_even, w_out_even, pool_w, pool_scale,
           w_in_odd, gla_w_gate_up, gla_b_gate, gla_norm, w_out_odd, ffn_w_gate, ffn_w_up, ffn_w_down):
    bsz, seq, d = x_prompt.shape
    db, dseq, _ = x_sample.shape
    depth = norm_mix.shape[0]
    length = seq + N_META
    npg = page_table.shape[1]
    past = npg * PAGE_SIZE
    a_width = A_HEADS * A_HEAD_DIM
    kv_width = A_KV_HEADS * A_HEAD_DIM
    pool_width = pool_scale.shape[1]
    gk = GLA_HEADS * GLA_DK
    gv = GLA_HEADS * GLA_DV
    d_ff = ffn_w_gate.shape[2]

    meta = jnp.broadcast_to(meta_tokens[None], (bsz, N_META, d))
    xp = jnp.concatenate([meta, x_prompt], axis=1).reshape(bsz * length, d)
    xs = x_sample.reshape(db * dseq, d)
    n_p, n_s = xp.shape[0], xs.shape[0]

    tm_p = _pick_tile(n_p, (688, 512, 256, 128, 64, 32, 16, 8))
    tm_s = _pick_tile(n_s, (128, 64, 32, 16, 8))
    tf = _pick_tile(d_ff, (1408, 1024, 512, 256, 128))
    tq = _pick_tile(length, (344, 256, 128, 64, 32, 16, 8))
    pool_rc = _pick_tile(length, (48, 32, 16, 8))
    gla_tb = _pick_tile(length, (688, 512, 256, 128, 64, 32, 16))
    pp = _pick_tile(npg, (16, 8, 4, 2, 1))

    e_off = [0]
    for s in (a_width, kv_width, kv_width, IDX_HEADS * IDX_DIM, IDX_DIM, IDX_HEADS, pool_width):
        e_off.append(e_off[-1] + s)
    kiwi_pad = LANES - IDX_DIM - IDX_HEADS
    even_widths = (a_width, kv_width, kv_width, IDX_HEADS * IDX_DIM, pool_width, LANES)
    odd_widths = (gk, gk, gv, gv, LANES)

    cache_kt = cache_k.transpose(0, 1, 3, 4, 2)
    cache_vt = cache_v.transpose(0, 1, 3, 4, 2)
    cache_kit = cache_kidx.transpose(0, 1, 3, 2)
    page_flat = page_table.reshape(-1).astype(I32)

    kp_l, vp_l, kip_l, bp_l, ks_l, vs_l, kis_l, bs_l, sp_l, ss_l = ([] for _ in range(10))
    for l in range(depth):
        if l % 2 == 0:
            e = l // 2
            w = w_in_even[e]
            w_in = jnp.concatenate(
                [w[:, e_off[0]:e_off[4]], w[:, e_off[6]:e_off[7]], w[:, e_off[4]:e_off[6]],
                 jnp.zeros((d, kiwi_pad), w.dtype)], axis=1).astype(BF16)
            w_out = w_out_even[e].astype(BF16)
            pw = pool_w[e].astype(BF16)

            q, k, v, qi, u, kiwi = norm_matmul(xp, norm_mix[l], w_in, even_widths, tm_p)
            r3 = lambda a: a.reshape(bsz, length, a.shape[1])
            oa = dsa_prompt(r3(q), r3(qi), r3(k), r3(v), r3(kiwi), tq, DSA_KEY_CHUNK)
            ob, buf = pool_mixer(r3(u), jnp.zeros((bsz, POOL_BUF + 1, pool_width), F32), pw,
                                 pool_scale[e], 0, pool_rc, BF16)
            mix_p = [(oa.reshape(n_p, a_width), w_out[:a_width]),
                     (ob.reshape(n_p, pool_width), w_out[a_width:])]
            kp_l.append(k.reshape(bsz, length, A_KV_HEADS, A_HEAD_DIM))
            vp_l.append(v.reshape(bsz, length, A_KV_HEADS, A_HEAD_DIM))
            kip_l.append(kiwi[:, :IDX_DIM].reshape(bsz, length, IDX_DIM))
            bp_l.append(buf)

            q, k, v, qi, u, kiwi = norm_matmul(xs, norm_mix[l], w_in, even_widths, tm_s)
            s3 = lambda a: a.reshape(db, dseq, a.shape[1])
            q_g = q.reshape(db, dseq, A_KV_HEADS, A_GROUP, A_HEAD_DIM).transpose(0, 2, 3, 1, 4)
            q_g = q_g.reshape(db, A_KV_HEADS, A_GROUP * dseq, A_HEAD_DIM)
            qi_t = qi.reshape(db, dseq * IDX_HEADS, IDX_DIM)
            wi_t = kiwi[:, IDX_DIM:IDX_DIM + IDX_HEADS].reshape(db, dseq * IDX_HEADS, 1)
            ki_new = kiwi[:, :IDX_DIM].reshape(db, dseq, IDX_DIM)
            oa = dsa_sample(q_g, qi_t, wi_t, ki_new, s3(k), s3(v), cache_kt, cache_vt, cache_kit,
                            page_flat, e, pp)
            oa = oa.reshape(db, A_KV_HEADS, A_GROUP, dseq, A_HEAD_DIM).transpose(0, 3, 1, 2, 4)
            prefix = jnp.pad(state_pool[e], ((0, 0), (1, 0), (0, 0)))
            ob, buf = pool_mixer(s3(u), prefix, pw, pool_scale[e], past, BF16_ROWS, F32)
            mix_s = [(oa.reshape(n_s, a_width), w_out[:a_width]),
                     (ob.reshape(n_s, pool_width), w_out[a_width:])]
            ks_l.append(k.reshape(db, dseq, A_KV_HEADS, A_HEAD_DIM))
            vs_l.append(v.reshape(db, dseq, A_KV_HEADS, A_HEAD_DIM))
            kis_l.append(ki_new)
            bs_l.append(buf)
        else:
            o = l // 2
            w = w_in_odd[o]
            w_in = jnp.concatenate([w, jnp.zeros((d, LANES - GLA_GATE_RANK), w.dtype)], axis=1).astype(BF16)
            w_out = w_out_odd[o].astype(BF16)
            wgu = jnp.pad(gla_w_gate_up[o], ((0, LANES - GLA_GATE_RANK), (0, 0)))

            q, k, v, g, gd = norm_matmul(xp, norm_mix[l], w_in, odd_widths, tm_p)
            r3 = lambda a: a.reshape(bsz, length, a.shape[1])
            on, s_p = gla_mixer(r3(q), r3(k), r3(v), r3(g), r3(gd), wgu, gla_b_gate[o], gla_norm[o],
                                None, None, gla_tb, GLA_CHUNK, None)
            mix_p = [(on.reshape(n_p, gv), w_out)]
            sp_l.append(s_p)

            q, k, v, g, gd = norm_matmul(xs, norm_mix[l], w_in, odd_widths, tm_s)
            tpad = _round_up(dseq, BF16_ROWS)
            p3 = lambda a: jnp.pad(a.reshape(db, dseq, a.shape[1]), ((0, 0), (0, tpad - dseq), (0, 0)))
            on, s_s = gla_mixer(p3(q), p3(k), p3(v), p3(g), p3(gd), wgu, gla_b_gate[o], gla_norm[o],
                                state_gla, o, tpad, tpad, dseq)
            mix_s = [(on[:, :dseq].reshape(n_s, gv), w_out)]
            ss_l.append(s_s)

        wg, wu, wd = ffn_w_gate[l].astype(BF16), ffn_w_up[l].astype(BF16), ffn_w_down[l].astype(BF16)
        xp = mix_ffn(xp, mix_p, norm_ffn[l], wg, wu, wd, tm_p, tf)
        xs = mix_ffn(xs, mix_s, norm_ffn[l], wg, wu, wd, tm_s, tf)

    y_prompt = final_norm(xp.reshape(bsz, length, d), norm_final, N_META)
    y_sample = final_norm(xs.reshape(1, n_s, d), norm_final, 0).reshape(db, dseq, d)
    return (y_prompt, y_sample,
            jnp.stack(kp_l), jnp.stack(vp_l), jnp.stack(kip_l),
            jnp.stack(ks_l), jnp.stack(vs_l), jnp.stack(kis_l),
            jnp.stack(bp_l), jnp.stack(bs_l),
            jnp.stack(sp_l), jnp.stack(ss_l))
```

```python
import functools

import jax
import jax.numpy as jnp
from jax import lax
from jax.experimental import pallas as pl
from jax.experimental.pallas import tpu as pltpu

F32 = jnp.float32
BF16 = jnp.bfloat16
I32 = jnp.int32

NORM_EPS = 1e-6
N_META = 16
A_HEADS = 8
A_KV_HEADS = 2
A_GROUP = A_HEADS // A_KV_HEADS
A_HEAD_DIM = 64
IDX_HEADS = 8
IDX_DIM = 64
TOPK_MAX = 256
PAGE_SIZE = 128
POOL_WINDOWS = (2, 4, 8, 16)
POOL_GROUP_DIM = 128
POOL_BUF = 15
GLA_HEADS = 4
GLA_DK = 128
GLA_DV = 256
GLA_GATE_RANK = 16
GLA_GATE_NORM = 16.0

LANES = 128
SUBLANES = 8
BF16_ROWS = 16
VMEM_LIMIT_BYTES = 56 * 1024 * 1024

DSA_KEY_CHUNK = 2 * LANES
GLA_CHUNK = 96

INT_MIN = -(2 ** 31)
MASK_BIAS = -1e30
HIGHEST = lax.Precision.HIGHEST
QK_SCALE_LOG2 = (A_HEAD_DIM ** -0.5) * 1.4426950408889634


def _round_up(n, m):
    return (n + m - 1) // m * m


def _cparams(*sem):
    return pltpu.CompilerParams(dimension_semantics=sem, vmem_limit_bytes=VMEM_LIMIT_BYTES)


def _rms(x, g):
    ms = jnp.mean(x * x, axis=-1, keepdims=True)
    return (x * lax.rsqrt(ms + NORM_EPS)) * g


def _dot_nt(a, b):
    return lax.dot_general(a, b, (((1,), (1,)), ((), ())), preferred_element_type=F32)


def _dot_tn(a, b):
    return lax.dot_general(a, b, (((0,), (0,)), ((), ())), preferred_element_type=F32)


def _norm_matmul_kernel(x_ref, g_ref, w_ref, *o_refs, widths):
    h = _rms(x_ref[...], g_ref[...]).astype(BF16)
    off = 0
    for o_ref, wd in zip(o_refs, widths):
        o_ref[...] = jnp.dot(h, w_ref[:, off:off + wd], preferred_element_type=F32)
        off += wd


def norm_matmul(x, g, w, widths, tm):
    n, d = x.shape
    assert n % tm == 0 and sum(widths) == w.shape[1]
    return pl.pallas_call(
        functools.partial(_norm_matmul_kernel, widths=widths),
        grid=(n // tm,),
        in_specs=[pl.BlockSpec((tm, d), lambda i: (i, 0)),
                  pl.BlockSpec((1, d), lambda i: (0, 0)),
                  pl.BlockSpec((d, w.shape[1]), lambda i: (0, 0))],
        out_specs=[pl.BlockSpec((tm, wd), lambda i: (i, 0)) for wd in widths],
        out_shape=[jax.ShapeDtypeStruct((n, wd), F32) for wd in widths],
        compiler_params=_cparams("parallel"),
        name="norm_matmul",
    )(x, g.reshape(1, d), w)


def _mix_ffn_kernel(*refs, n_mix):
    x_ref = refs[0]
    mix = refs[1:1 + 2 * n_mix]
    g_ref, wg_ref, wu_ref, wd_ref, o_ref, h_s = refs[1 + 2 * n_mix:]
    j = pl.program_id(1)

    @pl.when(j == 0)
    def _():
        xm = x_ref[...]
        for m in range(n_mix):
            a_ref, w_ref = mix[2 * m], mix[2 * m + 1]
            xm = xm + jnp.dot(a_ref[...].astype(BF16), w_ref[...], preferred_element_type=F32)
        o_ref[...] = xm
        h_s[...] = _rms(xm, g_ref[...]).astype(BF16)

    h = h_s[...]
    gate = jnp.dot(h, wg_ref[...], preferred_element_type=F32)
    up = jnp.dot(h, wu_ref[...], preferred_element_type=F32)
    act = (gate * jax.nn.sigmoid(gate) * up).astype(BF16)
    o_ref[...] += jnp.dot(act, wd_ref[...], preferred_element_type=F32)


def mix_ffn(x, mixes, g, wg, wu, wd, tm, tf):
    n, d = x.shape
    f = wg.shape[1]
    assert n % tm == 0 and f % tf == 0
    in_specs = [pl.BlockSpec((tm, d), lambda i, j: (i, 0))]
    args = [x]
    for a, w in mixes:
        in_specs += [pl.BlockSpec((tm, a.shape[1]), lambda i, j: (i, 0)),
                     pl.BlockSpec(w.shape, lambda i, j: (0, 0))]
        args += [a, w]
    in_specs += [pl.BlockSpec((1, d), lambda i, j: (0, 0)),
                 pl.BlockSpec((d, tf), lambda i, j: (0, j)),
                 pl.BlockSpec((d, tf), lambda i, j: (0, j)),
                 pl.BlockSpec((tf, d), lambda i, j: (j, 0))]
    args += [g.reshape(1, d), wg, wu, wd]
    return pl.pallas_call(
        functools.partial(_mix_ffn_kernel, n_mix=len(mixes)),
        grid=(n // tm, f // tf),
        in_specs=in_specs,
        out_specs=pl.BlockSpec((tm, d), lambda i, j: (i, 0)),
        out_shape=jax.ShapeDtypeStruct((n, d), F32),
        scratch_shapes=[pltpu.VMEM((tm, d), BF16)],
        compiler_params=_cparams("parallel", "arbitrary"),
        name="mix_ffn",
    )(*args)


def _final_norm_kernel(x_ref, g_ref, o_ref, *, skip, rows, chunk):
    for c in range(rows // chunk):
        x = x_ref[0, skip + c * chunk: skip + (c + 1) * chunk, :]
        o_ref[0, c * chunk:(c + 1) * chunk, :] = _rms(x, g_ref[...])


def final_norm(x, g, skip):
    b, t, d = x.shape
    rows = t - skip
    chunk = 256 if rows % 256 == 0 else rows
    return pl.pallas_call(
        functools.partial(_final_norm_kernel, skip=skip, rows=rows, chunk=chunk),
        grid=(b,),
        in_specs=[pl.BlockSpec((1, t, d), lambda i: (i, 0, 0)),
                  pl.BlockSpec((1, d), lambda i: (0, 0))],
        out_specs=pl.BlockSpec((1, rows, d), lambda i: (i, 0, 0)),
        out_shape=jax.ShapeDtypeStruct((b, rows, d), F32),
        compiler_params=_cparams("parallel"),
        name="final_norm",
    )(x, g.reshape(1, d))


def _pool_kernel(u_ref, pre_ref, pw_ref, sc_ref, ob_ref, buf_ref, ext_s, *, t, pos0, rc):
    head = POOL_BUF + 1
    tproc = _round_up(t, rc)
    ext_s[0:head, :] = pre_ref[0]
    ext_s[head:head + t, :] = u_ref[0]
    if tproc > t:
        ext_s[head + t:head + tproc, :] = jnp.zeros((tproc - t, ext_s.shape[1]), F32)
    buf_ref[0] = ext_s[t + 1:t + head, :]
    for c in range(tproc // rc):
        r0 = c * rc
        nst = min(rc, t - r0)
        pos = pos0 + r0 + lax.broadcasted_iota(I32, (rc, 1), 0)
        for gi, w in enumerate(POOL_WINDOWS):
            sl = slice(gi * POOL_GROUP_DIM, (gi + 1) * POOL_GROUP_DIM)
            xt = ext_s[head + r0:head + r0 + rc, sl]
            acc = xt
            for r in range(1, w):
                acc = acc + ext_s[head + r0 - r:head + r0 - r + rc, sl]
            cnt = jnp.minimum(w, pos + 1).astype(F32)
            dlt = acc / cnt - xt
            y = jnp.dot(dlt.astype(BF16), pw_ref[gi], preferred_element_type=F32) * sc_ref[:, sl]
            ob_ref[0, r0:r0 + nst, sl] = y[0:nst].astype(ob_ref.dtype)


def pool_mixer(u, prefix16, pw, scale, pos0, rc, out_dtype):
    b, t, c = u.shape
    head = POOL_BUF + 1
    return pl.pallas_call(
        functools.partial(_pool_kernel, t=t, pos0=pos0, rc=rc),
        grid=(b,),
        in_specs=[pl.BlockSpec((1, t, c), lambda i: (i, 0, 0)),
                  pl.BlockSpec((1, head, c), lambda i: (i, 0, 0)),
                  pl.BlockSpec(pw.shape, lambda i: (0, 0, 0)),
                  pl.BlockSpec((1, c), lambda i: (0, 0))],
        out_specs=[pl.BlockSpec((1, t, c), lambda i: (i, 0, 0)),
                   pl.BlockSpec((1, POOL_BUF, c), lambda i: (i, 0, 0))],
        out_shape=[jax.ShapeDtypeStruct((b, t, c), out_dtype),
                   jax.ShapeDtypeStruct((b, POOL_BUF, c), F32)],
        scratch_shapes=[pltpu.VMEM((head + _round_up(t, rc), c), F32)],
        compiler_params=_cparams("parallel"),
        name="pool_mixer",
    )(u, prefix16, pw, scale.reshape(1, c))


KEY_NEG_INF = INT_MIN + 0x7FFFFF


def _key_to_float(key):
    bits = key ^ ((key >> 31) & 0x7FFFFFFF)
    return jnp.where(key <= KEY_NEG_INF, -jnp.inf, lax.bitcast_convert_type(bits, F32))


def _lane_tile(x, n):
    return x if n == 1 else jnp.concatenate([x] * n, axis=1)


def _row_total(x):
    return jnp.broadcast_to(jnp.sum(x, axis=-1, keepdims=True), x.shape)


def _dsa_prompt_kernel(q_ref, qi_ref, k_ref, v_ref, kiwi_ref, o_ref,
                       kb_s, vb_s, kib_s, qb_s, qib_s, wib_s, sc_s, sct_s, m_s, acc_s,
                       *, length, tq, kc, topk):
    nkc = kib_s.shape[0]
    nl = kc // LANES
    tqp = sct_s.shape[2]
    tqb = qb_s.shape[1] // A_GROUP
    it = pl.program_id(1)
    r0 = it * tq
    nk = (r0 + tq + kc - 1) // kc
    kf = float(topk)

    def pad_rows(x, n):
        return x if x.shape[0] == n else jnp.concatenate([x, jnp.zeros((n - x.shape[0], x.shape[1]), x.dtype)], axis=0)

    @pl.when(it == 0)
    def _():
        ones_col = (lax.broadcasted_iota(I32, (kc, LANES - A_HEAD_DIM), 1) == 0).astype(BF16)
        for c in range(nkc):
            lo = c * kc
            valid = min(kc, length - lo)
            k_t = pad_rows(k_ref[0, lo:lo + valid, :], kc).T
            ki_t = pad_rows(kiwi_ref[0, lo:lo + valid, :], kc).T
            kib_s[c] = ki_t[0:IDX_DIM].astype(BF16)
            for kh in range(A_KV_HEADS):
                sl = slice(kh * A_HEAD_DIM, (kh + 1) * A_HEAD_DIM)
                kb_s[kh, c] = k_t[sl].astype(BF16)
                vb_s[kh, c, :, 0:A_HEAD_DIM] = pad_rows(v_ref[0, lo:lo + valid, sl], kc).astype(BF16)
                vb_s[kh, c, :, A_HEAD_DIM:LANES] = ones_col

    kiwi_t = pad_rows(kiwi_ref[0, pl.ds(pl.multiple_of(r0, SUBLANES), tq), :], tqp)
    for h in range(IDX_HEADS):
        qih = pad_rows(qi_ref[0, :, h * IDX_DIM:(h + 1) * IDX_DIM], tqp)
        qib_s[h * tqp:(h + 1) * tqp, :] = qih.astype(BF16)
        wib_s[h] = jnp.broadcast_to(kiwi_t[:, IDX_DIM + h:IDX_DIM + h + 1], (tqp, LANES))
    for hd in range(A_HEADS):
        kh, g = divmod(hd, A_GROUP)
        qh = pad_rows(q_ref[0, :, hd * A_HEAD_DIM:(hd + 1) * A_HEAD_DIM] * QK_SCALE_LOG2, tqb)
        qb_s[kh, g * tqb:(g + 1) * tqb, :] = qh.astype(BF16)
        m_s[hd] = jnp.full((tq, LANES), MASK_BIAS, F32)
        acc_s[hd] = jnp.zeros((tq, LANES), F32)

    def score_chunk(c, carry):
        s_all = jnp.dot(qib_s[...], kib_s[c], preferred_element_type=F32)
        score = jnp.zeros((tqp, kc), F32)
        for h in range(IDX_HEADS):
            score = score + jnp.maximum(s_all[h * tqp:(h + 1) * tqp], 0.0) * _lane_tile(wib_s[h], nl)
        col = c * kc + lax.broadcasted_iota(I32, (tqp, kc), 1)
        pos = r0 + lax.broadcasted_iota(I32, (tqp, kc), 0)
        vis_score = jnp.where(col <= pos, score, -jnp.inf)
        sc_s[c] = vis_score
        sct_s[c] = vis_score.T
        return carry

    lax.fori_loop(0, nk, score_chunk, 0)

    def count(pred):
        def body(c, acc):
            hit = pred(sct_s[c], c).astype(F32)
            return acc + jnp.sum(hit.reshape(kc // SUBLANES, SUBLANES, tqp), axis=0)
        acc = lax.fori_loop(0, nk, body, jnp.zeros((SUBLANES, tqp), F32))
        return jnp.sum(acc, axis=0, keepdims=True)

    t0 = jnp.where(count(lambda s, c: s >= 0.0) >= kf, 0, INT_MIN).astype(I32)

    def bit_step(b, t):
        cand = t + lax.shift_left(jnp.int32(1), 30 - b)
        cand_f = _key_to_float(cand)
        return jnp.where(count(lambda s, c: s >= cand_f) >= kf, cand, t)

    thr_key = lax.fori_loop(0, 31, bit_step, t0)
    thr_q = _key_to_float(thr_key)
    tie_q = (count(lambda s, c: s >= thr_q) > kf) & (thr_key > KEY_NEG_INF)
    tie_q = tie_q & (lax.broadcasted_iota(I32, (1, tqp), 1) < tq)

    ncols = nkc * kc
    nbits = max(1, (ncols - 1).bit_length())

    def break_ties():
        need = kf - count(lambda s, c: s > thr_q)
        kidx = lax.broadcasted_iota(I32, (kc, tqp), 0)

        def idx_step(b, jc):
            cand = jc + lax.shift_left(jnp.int32(1), nbits - 1 - b)
            n_lt = count(lambda s, c: (s == thr_q) & (c * kc + kidx < cand))
            return jnp.where(n_lt < need, cand, jc)

        jc = lax.fori_loop(0, nbits, idx_step, jnp.zeros((1, tqp), I32))
        return jnp.where(tie_q, jc, ncols)

    jcut_q = lax.cond(jnp.max(tie_q.astype(F32)) > 0.0, break_ties,
                      lambda: jnp.full((1, tqp), ncols, I32))

    def per_row(x_q):
        return jnp.broadcast_to(x_q, (LANES, tqp)).T[0:tq]

    thr_t = _lane_tile(per_row(thr_q), nl)
    jcut_t = _lane_tile(per_row(jcut_q), nl)

    def attend_chunk(c, carry):
        sc = sc_s[c, 0:tq, :]
        col = c * kc + lax.broadcasted_iota(I32, (tq, kc), 1)
        pos = r0 + lax.broadcasted_iota(I32, (tq, kc), 0)
        sel = (col <= pos) & ((sc > thr_t) | ((sc == thr_t) & (col <= jcut_t)))
        bias = jnp.where(sel, 0.0, MASK_BIAS).astype(F32)
        for kh in range(A_KV_HEADS):
            l_all = jnp.dot(qb_s[kh], kb_s[kh, c], preferred_element_type=F32)
            es, stats = [], []
            for g in range(A_GROUP):
                hd = kh * A_GROUP + g
                logits = l_all[g * tqb:g * tqb + tq] + bias
                m_old = m_s[hd]
                m_new = jnp.maximum(m_old, jnp.broadcast_to(jnp.max(logits, axis=-1, keepdims=True), (tq, LANES)))
                es.append(pad_rows(jnp.exp2(logits - _lane_tile(m_new, nl)), tqb).astype(BF16))
                stats.append((hd, m_old, m_new))
            pv_all = jnp.dot(jnp.concatenate(es, axis=0), vb_s[kh, c], preferred_element_type=F32)
            for g, (hd, m_old, m_new) in enumerate(stats):
                acc_s[hd] = jnp.exp2(m_old - m_new) * acc_s[hd] + pv_all[g * tqb:g * tqb + tq]
                m_s[hd] = m_new
        return carry

    lax.fori_loop(0, nk, attend_chunk, 0)

    for hd in range(A_HEADS):
        acc = acc_s[hd]
        o_ref[0, :, hd * A_HEAD_DIM:(hd + 1) * A_HEAD_DIM] = (
            acc[:, 0:A_HEAD_DIM] / acc[:, A_HEAD_DIM:A_HEAD_DIM + 1])


def dsa_prompt(q, qi, k, v, kiwi, tq, kc):
    b, length, _ = q.shape
    assert length % tq == 0 and kc % LANES == 0
    topk = min(TOPK_MAX, length // 4)
    nkc = -(-length // kc)
    tqp = _round_up(tq, LANES)
    tqb = _round_up(tq, BF16_ROWS)
    seq = lambda c: pl.BlockSpec((1, length, c), lambda i, j: (i, 0, 0))
    til = lambda c: pl.BlockSpec((1, tq, c), lambda i, j: (i, j, 0))
    return pl.pallas_call(
        functools.partial(_dsa_prompt_kernel, length=length, tq=tq, kc=kc, topk=topk),
        grid=(b, length // tq),
        in_specs=[til(q.shape[2]), til(qi.shape[2]), seq(k.shape[2]), seq(v.shape[2]), seq(kiwi.shape[2])],
        out_specs=til(q.shape[2]),
        out_shape=jax.ShapeDtypeStruct(q.shape, F32),
        scratch_shapes=[pltpu.VMEM((A_KV_HEADS, nkc, A_HEAD_DIM, kc), BF16),
                        pltpu.VMEM((A_KV_HEADS, nkc, kc, LANES), BF16),
                        pltpu.VMEM((nkc, IDX_DIM, kc), BF16),
                        pltpu.VMEM((A_KV_HEADS, A_GROUP * tqb, A_HEAD_DIM), BF16),
                        pltpu.VMEM((IDX_HEADS * tqp, IDX_DIM), BF16),
                        pltpu.VMEM((IDX_HEADS, tqp, LANES), F32),
                        pltpu.VMEM((nkc, tqp, kc), F32),
                        pltpu.VMEM((nkc, kc, tqp), F32),
                        pltpu.VMEM((A_HEADS, tq, LANES), F32),
                        pltpu.VMEM((A_HEADS, tq, LANES), F32)],
        compiler_params=_cparams("parallel", "arbitrary"),
        name="dsa_prompt",
    )(q, qi, k, v, kiwi)


def _dsa_sample_score_kernel(pt_ref, qi_ref, wi_ref, kinew_ref, *rest, pp, t):
    pages = rest[:pp]
    sc_ref, kin_s = rest[pp:]
    j = pl.program_id(1)
    nsteps = pl.num_programs(1)
    cw = pp * PAGE_SIZE
    past = sc_ref.shape[2] - LANES
    qi = qi_ref[0].astype(BF16)
    wi = wi_ref[0]

    def scores(s):
        s = jnp.maximum(s, 0.0) * wi
        return jnp.sum(s.reshape(t, IDX_HEADS, s.shape[1]), axis=1)

    ki_t = jnp.concatenate([p[0, 0] for p in pages], axis=1).astype(BF16)
    sc_chunk = scores(jnp.dot(qi, ki_t, preferred_element_type=F32))
    for jj in range(past // cw):
        @pl.when(j == jj)
        def _(jj=jj):
            sc_ref[0, :, jj * cw:(jj + 1) * cw] = sc_chunk

    @pl.when(j == nsteps - 1)
    def _():
        kin_s[...] = jnp.zeros(kin_s.shape, F32)
        kin_s[0:t, :] = kinew_ref[0]
        snew = scores(_dot_nt(qi, kin_s[...].astype(BF16)))
        cn = lax.broadcasted_iota(I32, (t, LANES), 1)
        rn = lax.broadcasted_iota(I32, (t, LANES), 0)
        sc_ref[0, :, past:past + LANES] = jnp.where(cn <= rn, snew, -jnp.inf)


def _dsa_select_kernel(sc_ref, thr_ref, jcut_ref, *, topk):
    rows, ncols = sc_ref.shape
    kf = float(topk)

    def count(pred):
        acc = jnp.zeros((rows, LANES), F32)
        for c0 in range(0, ncols, LANES):
            acc = acc + pred(sc_ref[:, c0:c0 + LANES], c0).astype(F32)
        return _row_total(acc)

    t0 = jnp.where(count(lambda s, c0: s >= 0.0) >= kf, 0, INT_MIN).astype(I32)

    def bit_step(b, t):
        cand = t + lax.shift_left(jnp.int32(1), 30 - b)
        cand_f = _key_to_float(cand)
        return jnp.where(count(lambda s, c0: s >= cand_f) >= kf, cand, t)

    thr_key = lax.fori_loop(0, 31, bit_step, t0)
    thr = _key_to_float(thr_key)
    tie = (count(lambda s, c0: s >= thr) > kf) & (thr_key > KEY_NEG_INF)
    nbits = max(1, (ncols - 1).bit_length())

    def break_ties():
        need = kf - count(lambda s, c0: s > thr)
        lane = lax.broadcasted_iota(I32, (rows, LANES), 1)

        def idx_step(b, jc):
            cand = jc + lax.shift_left(jnp.int32(1), nbits - 1 - b)
            n_lt = count(lambda s, c0: (s == thr) & (c0 + lane < cand))
            return jnp.where(n_lt < need, cand, jc)

        jc = lax.fori_loop(0, nbits, idx_step, jnp.zeros((rows, LANES), I32))
        return jnp.where(tie, jc, ncols)

    thr_ref[...] = thr
    jcut_ref[...] = lax.cond(jnp.max(tie.astype(F32)) > 0.0, break_ties,
                             lambda: jnp.full((rows, LANES), ncols, I32))


def _dsa_sample_attn_kernel(pt_ref, q_ref, sc_ref, scnew_ref, thr_ref, jcut_ref, knew_ref, vnew_ref,
                            *rest, pp, t):
    kpages = rest[:pp]
    vpages = rest[pp:2 * pp]
    o_ref, m_s, l_s, acc_s, new_s = rest[2 * pp:]
    j = pl.program_id(1)
    nsteps = pl.num_programs(1)
    cw = pp * PAGE_SIZE

    def bias_of(sc, c0):
        n = sc.shape[1] // LANES
        thr, jcut = _lane_tile(thr_ref[0], n), _lane_tile(jcut_ref[0], n)
        col = c0 + lax.broadcasted_iota(I32, sc.shape, 1)
        sel = (sc > -jnp.inf) & ((sc > thr) | ((sc == thr) & (col <= jcut)))
        bias = jnp.where(sel, 0.0, MASK_BIAS).astype(F32)
        return jnp.concatenate([bias] * A_GROUP, axis=0)

    @pl.when(j == 0)
    def _():
        m_s[...] = jnp.full(m_s.shape, MASK_BIAS, F32)
        l_s[...] = jnp.zeros(l_s.shape, F32)
        acc_s[...] = jnp.zeros(acc_s.shape, F32)

    def update(kh, kb, vb, bias, key_major):
        qh = (q_ref[0, kh] * (A_HEAD_DIM ** -0.5)).astype(BF16)
        qk = _dot_nt(qh, kb) if key_major else jnp.dot(qh, kb, preferred_element_type=F32)
        logits = qk + bias
        m_old = m_s[kh]
        m_new = jnp.maximum(m_old, jnp.max(logits, axis=-1, keepdims=True))
        alpha = jnp.exp(m_old - m_new)
        e = jnp.exp(logits - m_new)
        p = e.astype(BF16)
        pv = jnp.dot(p, vb, preferred_element_type=F32) if key_major else _dot_nt(p, vb)
        l_s[kh] = alpha * l_s[kh] + jnp.sum(e, axis=-1, keepdims=True)
        acc_s[kh] = alpha * acc_s[kh] + pv
        m_s[kh] = m_new

    bias = bias_of(sc_ref[0], j * cw)
    for kh in range(A_KV_HEADS):
        k_t = jnp.concatenate([p[0, 0, kh] for p in kpages], axis=1).astype(BF16)
        v_t = jnp.concatenate([p[0, 0, kh] for p in vpages], axis=1).astype(BF16)
        update(kh, k_t, v_t, bias, False)

    @pl.when(j == nsteps - 1)
    def _():
        bias_new = bias_of(scnew_ref[0], nsteps * cw)
        for kh in range(A_KV_HEADS):
            sl = slice(kh * A_HEAD_DIM, (kh + 1) * A_HEAD_DIM)
            new_s[...] = jnp.zeros(new_s.shape, F32)
            new_s[0, 0:t, :] = knew_ref[0, :, sl]
            new_s[1, 0:t, :] = vnew_ref[0, :, sl]
            update(kh, new_s[0].astype(BF16), new_s[1].astype(BF16), bias_new, True)
            o_ref[0, kh] = acc_s[kh] / l_s[kh]


def dsa_sample(q_g, qi_t, wi_t, ki_new, k_new, v_new, cache_k, cache_v, cache_ki, page_flat, layer, pp):
    db, _, gt, hd = q_g.shape
    t = gt // A_GROUP
    npg = page_flat.shape[0] // db
    assert npg % pp == 0
    nsteps = npg // pp
    past = npg * PAGE_SIZE
    ncols = past + LANES
    cw = pp * PAGE_SIZE
    kvw = k_new.shape[2]

    def page_spec(shape, i):
        nz = (0,) * len(shape)
        return pl.BlockSpec((1, 1) + shape, lambda b, j, pt, i=i: (layer, pt[b * npg + j * pp + i]) + nz)

    ki_page = (IDX_DIM, PAGE_SIZE)
    kv_page = (A_KV_HEADS, hd, PAGE_SIZE)

    sc = pl.pallas_call(
        functools.partial(_dsa_sample_score_kernel, pp=pp, t=t),
        grid_spec=pltpu.PrefetchScalarGridSpec(
            num_scalar_prefetch=1,
            grid=(db, nsteps),
            in_specs=[pl.BlockSpec((1, t * IDX_HEADS, IDX_DIM), lambda b, j, pt: (b, 0, 0)),
                      pl.BlockSpec((1, t * IDX_HEADS, 1), lambda b, j, pt: (b, 0, 0)),
                      pl.BlockSpec((1, t, IDX_DIM), lambda b, j, pt: (b, 0, 0))]
                     + [page_spec(ki_page, i) for i in range(pp)],
            out_specs=pl.BlockSpec((1, t, ncols), lambda b, j, pt: (b, 0, 0)),
            scratch_shapes=[pltpu.VMEM((LANES, IDX_DIM), F32)]),
        out_shape=jax.ShapeDtypeStruct((db, t, ncols), F32),
        compiler_params=_cparams("parallel", "arbitrary"),
        name="dsa_sample_score",
    )(page_flat, qi_t, wi_t, ki_new, *([cache_ki] * pp))

    thr, jcut = pl.pallas_call(
        functools.partial(_dsa_select_kernel, topk=min(TOPK_MAX, (past + t) // 4)),
        out_shape=[jax.ShapeDtypeStruct((db * t, LANES), F32), jax.ShapeDtypeStruct((db * t, LANES), I32)],
        compiler_params=pltpu.CompilerParams(vmem_limit_bytes=VMEM_LIMIT_BYTES),
        name="dsa_sample_select",
    )(sc.reshape(db * t, ncols))
    thr, jcut = thr.reshape(db, t, LANES), jcut.reshape(db, t, LANES)

    out = pl.pallas_call(
        functools.partial(_dsa_sample_attn_kernel, pp=pp, t=t),
        grid_spec=pltpu.PrefetchScalarGridSpec(
            num_scalar_prefetch=1,
            grid=(db, nsteps),
            in_specs=[pl.BlockSpec((1, A_KV_HEADS, gt, hd), lambda b, j, pt: (b, 0, 0, 0)),
                      pl.BlockSpec((1, t, cw), lambda b, j, pt: (b, 0, j)),
                      pl.BlockSpec((1, t, LANES), lambda b, j, pt: (b, 0, past // LANES)),
                      pl.BlockSpec((1, t, LANES), lambda b, j, pt: (b, 0, 0)),
                      pl.BlockSpec((1, t, LANES), lambda b, j, pt: (b, 0, 0)),
                      pl.BlockSpec((1, t, kvw), lambda b, j, pt: (b, 0, 0)),
                      pl.BlockSpec((1, t, kvw), lambda b, j, pt: (b, 0, 0))]
                     + [page_spec(kv_page, i) for i in range(pp)]
                     + [page_spec(kv_page, i) for i in range(pp)],
            out_specs=pl.BlockSpec((1, A_KV_HEADS, gt, hd), lambda b, j, pt: (b, 0, 0, 0)),
            scratch_shapes=[pltpu.VMEM((A_KV_HEADS, gt, 1), F32),
                            pltpu.VMEM((A_KV_HEADS, gt, 1), F32),
                            pltpu.VMEM((A_KV_HEADS, gt, hd), F32),
                            pltpu.VMEM((2, LANES, hd), F32)]),
        out_shape=jax.ShapeDtypeStruct((db, A_KV_HEADS, gt, hd), F32),
        compiler_params=_cparams("parallel", "arbitrary"),
        name="dsa_sample_attn",
    )(page_flat, q_g, sc, sc, thr, jcut, k_new, v_new, *([cache_k] * pp), *([cache_v] * pp))
    return out


def _gla_kernel(*refs, tb, tv, chunk, sub, has_s0):
    if has_s0:
        (q_ref, k_ref, v_ref, g_ref, gd_ref, wgu_ref, bg_ref, gn_ref, s0_ref,
         on_ref, sout_ref, b_s, dec_s, st_s) = refs
    else:
        q_ref, k_ref, v_ref, g_ref, gd_ref, wgu_ref, bg_ref, gn_ref, on_ref, sout_ref, b_s, dec_s, st_s = refs
    jb = pl.program_id(1)
    nb = pl.num_programs(1)
    nfull, rem = tb // chunk, tb % chunk
    spans = [(ic * chunk, chunk) for ic in range(nfull)] + ([(nfull * chunk, rem)] if rem else [])

    @pl.when(jb == 0)
    def _():
        for h in range(GLA_HEADS):
            st_s[h] = s0_ref[0, 0, h] if has_s0 else jnp.zeros((GLA_DK, GLA_DV), F32)

    x = jnp.dot(gd_ref[0], wgu_ref[...], precision=HIGHEST, preferred_element_type=F32) + bg_ref[...]
    la = (jnp.minimum(x, 0.0) - jnp.log1p(jnp.exp(-jnp.abs(x)))) * (1.0 / GLA_GATE_NORM)
    if tv is not None:
        row = jb * tb + lax.broadcasted_iota(I32, (tb, 1), 0)
        la = jnp.where(row < tv, la, 0.0)
    for ic, (r0, c) in enumerate(spans):
        lac = la[r0:r0 + c]
        tri = (lax.broadcasted_iota(I32, (c, c), 0) >= lax.broadcasted_iota(I32, (c, c), 1)).astype(BF16)
        hi = lac.astype(BF16)
        r1 = lac - hi.astype(F32)
        mid = r1.astype(BF16)
        lo = (r1 - mid.astype(F32)).astype(BF16)
        bc = (jnp.dot(tri, hi, preferred_element_type=F32) + jnp.dot(tri, mid, preferred_element_type=F32)
              + jnp.dot(tri, lo, preferred_element_type=F32))
        b_s[r0:r0 + c, :] = bc
        dec_s[ic] = jnp.exp(jnp.broadcast_to(bc[c - 1:c, :], (LANES, bc.shape[1])).T)

    def do_chunk(ic, r0, c):
        rows = pl.ds(r0, c)
        b = b_s[rows, :]
        qc = q_ref[0, rows, :] * (GLA_DK ** -0.5)
        kc = k_ref[0, rows, :]
        if tv is not None:
            rowc = jb * tb + r0 + lax.broadcasted_iota(I32, (c, 1), 0)
            kc = jnp.where(rowc < tv, kc, 0.0)
        vc = v_ref[0, rows, :].astype(BF16)
        gc = g_ref[0, rows, :]
        dec = dec_s[ic]
        causal = lax.broadcasted_iota(I32, (c, LANES), 1) <= lax.broadcasted_iota(I32, (c, LANES), 0)
        heads, states = [], []
        for h in range(GLA_HEADS):
            sk = slice(h * GLA_DK, (h + 1) * GLA_DK)
            sv = slice(h * GLA_DV, (h + 1) * GLA_DV)
            bh, qh, kh, vh = b[:, sk], qc[:, sk], kc[:, sk], vc[:, sv]
            st = st_s[h]
            att = []
            for i in range(c // sub):
                lo, hi = i * sub, (i + 1) * sub
                ref_b = bh[lo - 1:lo, :] if i > 0 else jnp.zeros((1, GLA_DK), F32)
                qt = (qh[lo:hi] * jnp.exp(bh[lo:hi] - ref_b)).astype(BF16)
                kt = (kh[0:hi] * jnp.exp(ref_b - bh[0:hi])).astype(BF16)
                kt = jnp.concatenate([kt, jnp.zeros((LANES - hi, GLA_DK), BF16)], axis=0)
                att.append(_dot_nt(qt, kt))
            att = jnp.concatenate(att, axis=0) if len(att) > 1 else att[0]
            att = jnp.where(causal, att, 0.0).astype(BF16)
            lhs = jnp.concatenate([(qh * jnp.exp(bh)).astype(BF16), att], axis=1)
            vpad = vh if c == LANES else jnp.concatenate([vh, jnp.zeros((LANES - c, GLA_DV), BF16)], axis=0)
            rhs = jnp.concatenate([st.astype(BF16), vpad], axis=0)
            o = jnp.dot(lhs, rhs, preferred_element_type=F32)
            bl = bh[c - 1:c, :]
            kd = (kh * jnp.exp(bl - bh)).astype(BF16)
            dh = dec[sk, :]
            states.append(jnp.concatenate([dh, dh], axis=1) * st + _dot_tn(kd, vh))
            gh = gc[:, sv]
            heads.append(_rms(o, gn_ref[...]) * (gh * jax.nn.sigmoid(gh)))
        on_ref[0, rows, :] = jnp.concatenate(heads, axis=1).astype(on_ref.dtype)
        for h in range(GLA_HEADS):
            st_s[h] = states[h]

    if nfull == 1:
        do_chunk(0, 0, chunk)
    elif nfull > 1:
        def body(ic, carry):
            do_chunk(ic, pl.multiple_of(ic * chunk, BF16_ROWS), chunk)
            return carry
        lax.fori_loop(0, nfull, body, 0)
    if rem:
        do_chunk(nfull, nfull * chunk, rem)

    @pl.when(jb == nb - 1)
    def _():
        for h in range(GLA_HEADS):
            sout_ref[0, h] = st_s[h]


def gla_mixer(q, k, v, g, gd, wgu, bg, gn, s0, s0_layer, tb, chunk, tv):
    b, t, _ = q.shape
    sub = BF16_ROWS
    assert t % tb == 0 and tb % sub == 0 and chunk % sub == 0 and chunk <= LANES and GLA_DV == 2 * LANES
    nspans = -(-tb // chunk)
    blk = lambda c: pl.BlockSpec((1, tb, c), lambda i, j: (i, j, 0))
    cst = lambda a: pl.BlockSpec(a.shape, lambda i, j: (0,) * a.ndim)
    st_spec = pl.BlockSpec((1, GLA_HEADS, GLA_DK, GLA_DV), lambda i, j: (i, 0, 0, 0))
    bg2, gn2 = bg.reshape(1, -1), gn.reshape(1, -1)
    args = [q, k, v, g, gd, wgu, bg2, gn2]
    in_specs = [blk(q.shape[2]), blk(k.shape[2]), blk(v.shape[2]), blk(g.shape[2]), blk(gd.shape[2]),
                cst(wgu), cst(bg2), cst(gn2)]
    if s0 is not None:
        args.append(s0)
        in_specs.append(pl.BlockSpec((1, 1, GLA_HEADS, GLA_DK, GLA_DV), lambda i, j: (s0_layer, i, 0, 0, 0)))
    return pl.pallas_call(
        functools.partial(_gla_kernel, tb=tb, tv=tv, chunk=chunk, sub=sub, has_s0=s0 is not None),
        grid=(b, t // tb),
        in_specs=in_specs,
        out_specs=[blk(v.shape[2]), st_spec],
        out_shape=[jax.ShapeDtypeStruct(v.shape, BF16),
                   jax.ShapeDtypeStruct((b, GLA_HEADS, GLA_DK, GLA_DV), F32)],
        scratch_shapes=[pltpu.VMEM((tb, GLA_HEADS * GLA_DK), F32),
                        pltpu.VMEM((nspans, GLA_HEADS * GLA_DK, LANES), F32),
                        pltpu.VMEM((GLA_HEADS, GLA_DK, GLA_DV), F32)],
        compiler_params=_cparams("parallel", "arbitrary"),
        name="gla_mixer",
    )(*args)


def _pick_tile(n, candidates):
    for c in candidates:
        if n % c == 0:
            return c
    return n


def kernel(x_prompt, x_sample, cache_k, cache_v, cache_kidx, state_pool, state_gla, page_table,
           meta_tokens, norm_mix, norm_ffn, norm_final, w_in_even, w_out_even, pool_w, pool_scale,
           w_in_odd, gla_w_gate_up, gla_b_gate, gla_norm, w_out_odd, ffn_w_gate, ffn_w_up, ffn_w_down):
    bsz, seq, d = x_prompt.shape
    db, dseq, _ = x_sample.shape
    depth = norm_mix.shape[0]
    length = seq + N_META
    npg = page_table.shape[1]
    past = npg * PAGE_SIZE
    a_width = A_HEADS * A_HEAD_DIM
    kv_width = A_KV_HEADS * A_HEAD_DIM
    pool_width = pool_scale.shape[1]
    gk = GLA_HEADS * GLA_DK
    gv = GLA_HEADS * GLA_DV
    d_ff = ffn_w_gate.shape[2]

    meta = jnp.broadcast_to(meta_tokens[None], (bsz, N_META, d))
    xp = jnp.concatenate([meta, x_prompt], axis=1).reshape(bsz * length, d)
    xs = x_sample.reshape(db * dseq, d)
    n_p, n_s = xp.shape[0], xs.shape[0]

    tm_p = _pick_tile(n_p, (688, 512, 256, 128, 64, 32, 16, 8))
    tm_s = _pick_tile(n_s, (128, 64, 32, 16, 8))
    tf = _pick_tile(d_ff, (1408, 1024, 512, 256, 128))
    tq = _pick_tile(length, (344, 256, 128, 64, 32, 16, 8))
    pool_rc = _pick_tile(length, (48, 32, 16, 8))
    gla_tb = _pick_tile(length, (688, 512, 256, 128, 64, 32, 16))
    pp = _pick_tile(npg, (16, 8, 4, 2, 1))

    e_off = [0]
    for s in (a_width, kv_width, kv_width, IDX_HEADS * IDX_DIM, IDX_DIM, IDX_HEADS, pool_width):
        e_off.append(e_off[-1] + s)
    kiwi_pad = LANES - IDX_DIM - IDX_HEADS
    even_widths = (a_width, kv_width, kv_width, IDX_HEADS * IDX_DIM, pool_width, LANES)
    odd_widths = (gk, gk, gv, gv, LANES)

    cache_kt = cache_k.transpose(0, 1, 3, 4, 2)
    cache_vt = cache_v.transpose(0, 1, 3, 4, 2)
    cache_kit = cache_kidx.transpose(0, 1, 3, 2)
    page_flat = page_table.reshape(-1).astype(I32)

    kp_l, vp_l, kip_l, bp_l, ks_l, vs_l, kis_l, bs_l, sp_l, ss_l = ([] for _ in range(10))
    for l in range(depth):
        if l % 2 == 0:
            e = l // 2
            w = w_in_even[e]
            w_in = jnp.concatenate(
                [w[:, e_off[0]:e_off[4]], w[:, e_off[6]:e_off[7]], w[:, e_off[4]:e_off[6]],
                 jnp.zeros((d, kiwi_pad), w.dtype)], axis=1).astype(BF16)
            w_out = w_out_even[e].astype(BF16)
            pw = pool_w[e].astype(BF16)

            q, k, v, qi, u, kiwi = norm_matmul(xp, norm_mix[l], w_in, even_widths, tm_p)
            r3 = lambda a: a.reshape(bsz, length, a.shape[1])
            oa = dsa_prompt(r3(q), r3(qi), r3(k), r3(v), r3(kiwi), tq, DSA_KEY_CHUNK)
            ob, buf = pool_mixer(r3(u), jnp.zeros((bsz, POOL_BUF + 1, pool_width), F32), pw,
                                 pool_scale[e], 0, pool_rc, BF16)
            mix_p = [(oa.reshape(n_p, a_width), w_out[:a_width]),
                     (ob.reshape(n_p, pool_width), w_out[a_width:])]
            kp_l.append(k.reshape(bsz, length, A_KV_HEADS, A_HEAD_DIM))
            vp_l.append(v.reshape(bsz, length, A_KV_HEADS, A_HEAD_DIM))
            kip_l.append(kiwi[:, :IDX_DIM].reshape(bsz, length, IDX_DIM))
            bp_l.append(buf)

            q, k, v, qi, u, kiwi = norm_matmul(xs, norm_mix[l], w_in, even_widths, tm_s)
            s3 = lambda a: a.reshape(db, dseq, a.shape[1])
            q_g = q.reshape(db, dseq, A_KV_HEADS, A_GROUP, A_HEAD_DIM).transpose(0, 2, 3, 1, 4)
            q_g = q_g.reshape(db, A_KV_HEADS, A_GROUP * dseq, A_HEAD_DIM)
            qi_t = qi.reshape(db, dseq * IDX_HEADS, IDX_DIM)
            wi_t = kiwi[:, IDX_DIM:IDX_DIM + IDX_HEADS].reshape(db, dseq * IDX_HEADS, 1)
            ki_new = kiwi[:, :IDX_DIM].reshape(db, dseq, IDX_DIM)
            oa = dsa_sample(q_g, qi_t, wi_t, ki_new, s3(k), s3(v), cache_kt, cache_vt, cache_kit,
                            page_flat, e, pp)
            oa = oa.reshape(db, A_KV_HEADS, A_GROUP, dseq, A_HEAD_DIM).transpose(0, 3, 1, 2, 4)
            prefix = jnp.pad(state_pool[e], ((0, 0), (1, 0), (0, 0)))
            ob, buf = pool_mixer(s3(u), prefix, pw, pool_scale[e], past, BF16_ROWS, F32)
            mix_s = [(oa.reshape(n_s, a_width), w_out[:a_width]),
                     (ob.reshape(n_s, pool_width), w_out[a_width:])]
            ks_l.append(k.reshape(db, dseq, A_KV_HEADS, A_HEAD_DIM))
            vs_l.append(v.reshape(db, dseq, A_KV_HEADS, A_HEAD_DIM))
            kis_l.append(ki_new)
            bs_l.append(buf)
        else:
            o = l // 2
            w = w_in_odd[o]
            w_in = jnp.concatenate([w, jnp.zeros((d, LANES - GLA_GATE_RANK), w.dtype)], axis=1).astype(BF16)
            w_out = w_out_odd[o].astype(BF16)
            wgu = jnp.pad(gla_w_gate_up[o], ((0, LANES - GLA_GATE_RANK), (0, 0)))

            q, k, v, g, gd = norm_matmul(xp, norm_mix[l], w_in, odd_widths, tm_p)
            r3 = lambda a: a.reshape(bsz, length, a.shape[1])
            on, s_p = gla_mixer(r3(q), r3(k), r3(v), r3(g), r3(gd), wgu, gla_b_gate[o], gla_norm[o],
                                None, None, gla_tb, GLA_CHUNK, None)
            mix_p = [(on.reshape(n_p, gv), w_out)]
            sp_l.append(s_p)

            q, k, v, g, gd = norm_matmul(xs, norm_mix[l], w_in, odd_widths, tm_s)
            tpad = _round_up(dseq, BF16_ROWS)
            p3 = lambda a: jnp.pad(a.reshape(db, dseq, a.shape[1]), ((0, 0), (0, tpad - dseq), (0, 0)))
            on, s_s = gla_mixer(p3(q), p3(k), p3(v), p3(g), p3(gd), wgu, gla_b_gate[o], gla_norm[o],
                                state_gla, o, tpad, tpad, dseq)
            mix_s = [(on[:, :dseq].reshape(n_s, gv), w_out)]
            ss_l.append(s_s)

        wg, wu, wd = ffn_w_gate[l].astype(BF16), ffn_w_up[l].astype(BF16), ffn_w_down[l].astype(BF16)
        xp = mix_ffn(xp, mix_p, norm_ffn[l], wg, wu, wd, tm_p, tf)
        xs = mix_ffn(xs, mix_s, norm_ffn[l], wg, wu, wd, tm_s, tf)

    y_prompt = final_norm(xp.reshape(bsz, length, d), norm_final, N_META)
    y_sample = final_norm(xs.reshape(1, n_s, d), norm_final, 0).reshape(db, dseq, d)
    return (y_prompt, y_sample,
            jnp.stack(kp_l), jnp.stack(vp_l), jnp.stack(kip_l),
            jnp.stack(ks_l), jnp.stack(vs_l), jnp.stack(kis_l),
            jnp.stack(bp_l), jnp.stack(bs_l),
            jnp.stack(sp_l), jnp.stack(ss_l))
```

```python
import functools

import jax
import jax.numpy as jnp
from jax import lax
from jax.experimental import pallas as pl
from jax.experimental.pallas import tpu as pltpu

F32 = jnp.float32
BF16 = jnp.bfloat16
I32 = jnp.int32

NORM_EPS = 1e-6
N_META = 16
A_HEADS = 8
A_KV_HEADS = 2
A_GROUP = A_HEADS // A_KV_HEADS
A_HEAD_DIM = 64
IDX_HEADS = 8
IDX_DIM = 64
TOPK_MAX = 256
PAGE_SIZE = 128
POOL_WINDOWS = (2, 4, 8, 16)
POOL_GROUP_DIM = 128
POOL_BUF = 15
GLA_HEADS = 4
GLA_DK = 128
GLA_DV = 256
GLA_GATE_RANK = 16
GLA_GATE_NORM = 16.0

LANES = 128
SUBLANES = 8
BF16_ROWS = 16
VMEM_LIMIT_BYTES = 56 * 1024 * 1024

DSA_KEY_CHUNK = 2 * LANES
GLA_CHUNK = 96

INT_MIN = -(2 ** 31)
MASK_BIAS = -1e30
HIGHEST = lax.Precision.HIGHEST
QK_SCALE_LOG2 = (A_HEAD_DIM ** -0.5) * 1.4426950408889634


def _round_up(n, m):
    return (n + m - 1) // m * m


def _cparams(*sem):
    return pltpu.CompilerParams(dimension_semantics=sem, vmem_limit_bytes=VMEM_LIMIT_BYTES)


def _rms(x, g):
    ms = jnp.mean(x * x, axis=-1, keepdims=True)
    return (x * lax.rsqrt(ms + NORM_EPS)) * g


def _dot_nt(a, b):
    return lax.dot_general(a, b, (((1,), (1,)), ((), ())), preferred_element_type=F32)


def _dot_tn(a, b):
    return lax.dot_general(a, b, (((0,), (0,)), ((), ())), preferred_element_type=F32)


def _norm_matmul_kernel(x_ref, g_ref, w_ref, *o_refs, widths):
    h = _rms(x_ref[...], g_ref[...]).astype(BF16)
    off = 0
    for o_ref, wd in zip(o_refs, widths):
        o_ref[...] = jnp.dot(h, w_ref[:, off:off + wd], preferred_element_type=F32)
        off += wd


def norm_matmul(x, g, w, widths, tm):
    n, d = x.shape
    assert n % tm == 0 and sum(widths) == w.shape[1]
    return pl.pallas_call(
        functools.partial(_norm_matmul_kernel, widths=widths),
        grid=(n // tm,),
        in_specs=[pl.BlockSpec((tm, d), lambda i: (i, 0)),
                  pl.BlockSpec((1, d), lambda i: (0, 0)),
                  pl.BlockSpec((d, w.shape[1]), lambda i: (0, 0))],
        out_specs=[pl.BlockSpec((tm, wd), lambda i: (i, 0)) for wd in widths],
        out_shape=[jax.ShapeDtypeStruct((n, wd), F32) for wd in widths],
        compiler_params=_cparams("parallel"),
        name="norm_matmul",
    )(x, g.reshape(1, d), w)


def _mix_ffn_kernel(*refs, n_mix):
    x_ref = refs[0]
    mix = refs[1:1 + 2 * n_mix]
    g_ref, wg_ref, wu_ref, wd_ref, o_ref, h_s = refs[1 + 2 * n_mix:]
    j = pl.program_id(1)

    @pl.when(j == 0)
    def _():
        xm = x_ref[...]
        for m in range(n_mix):
            a_ref, w_ref = mix[2 * m], mix[2 * m + 1]
            xm = xm + jnp.dot(a_ref[...].astype(BF16), w_ref[...], preferred_element_type=F32)
        o_ref[...] = xm
        h_s[...] = _rms(xm, g_ref[...]).astype(BF16)

    h = h_s[...]
    gate = jnp.dot(h, wg_ref[...], preferred_element_type=F32)
    up = jnp.dot(h, wu_ref[...], preferred_element_type=F32)
    act = (gate * jax.nn.sigmoid(gate) * up).astype(BF16)
    o_ref[...] += jnp.dot(act, wd_ref[...], preferred_element_type=F32)


def mix_ffn(x, mixes, g, wg, wu, wd, tm, tf):
    n, d = x.shape
    f = wg.shape[1]
    assert n % tm == 0 and f % tf == 0
    in_specs = [pl.BlockSpec((tm, d), lambda i, j: (i, 0))]
    args = [x]
    for a, w in mixes:
        in_specs += [pl.BlockSpec((tm, a.shape[1]), lambda i, j: (i, 0)),
                     pl.BlockSpec(w.shape, lambda i, j: (0, 0))]
        args += [a, w]
    in_specs += [pl.BlockSpec((1, d), lambda i, j: (0, 0)),
                 pl.BlockSpec((d, tf), lambda i, j: (0, j)),
                 pl.BlockSpec((d, tf), lambda i, j: (0, j)),
                 pl.BlockSpec((tf, d), lambda i, j: (j, 0))]
    args += [g.reshape(1, d), wg, wu, wd]
    return pl.pallas_call(
        functools.partial(_mix_ffn_kernel, n_mix=len(mixes)),
        grid=(n // tm, f // tf),
        in_specs=in_specs,
        out_specs=pl.BlockSpec((tm, d), lambda i, j: (i, 0)),
        out_shape=jax.ShapeDtypeStruct((n, d), F32),
        scratch_shapes=[pltpu.VMEM((tm, d), BF16)],
        compiler_params=_cparams("parallel", "arbitrary"),
        name="mix_ffn",
    )(*args)


def _final_norm_kernel(x_ref, g_ref, o_ref, *, skip, rows, chunk):
    for c in range(rows // chunk):
        x = x_ref[0, skip + c * chunk: skip + (c + 1) * chunk, :]
        o_ref[0, c * chunk:(c + 1) * chunk, :] = _rms(x, g_ref[...])


def final_norm(x, g, skip):
    b, t, d = x.shape
    rows = t - skip
    chunk = 256 if rows % 256 == 0 else rows
    return pl.pallas_call(
        functools.partial(_final_norm_kernel, skip=skip, rows=rows, chunk=chunk),
        grid=(b,),
        in_specs=[pl.BlockSpec((1, t, d), lambda i: (i, 0, 0)),
                  pl.BlockSpec((1, d), lambda i: (0, 0))],
        out_specs=pl.BlockSpec((1, rows, d), lambda i: (i, 0, 0)),
        out_shape=jax.ShapeDtypeStruct((b, rows, d), F32),
        compiler_params=_cparams("parallel"),
        name="final_norm",
    )(x, g.reshape(1, d))


def _pool_kernel(u_ref, pre_ref, pw_ref, sc_ref, ob_ref, buf_ref, ext_s, *, t, pos0, rc):
    head = POOL_BUF + 1
    tproc = _round_up(t, rc)
    ext_s[0:head, :] = pre_ref[0]
    ext_s[head:head + t, :] = u_ref[0]
    if tproc > t:
        ext_s[head + t:head + tproc, :] = jnp.zeros((tproc - t, ext_s.shape[1]), F32)
    buf_ref[0] = ext_s[t + 1:t + head, :]
    for c in range(tproc // rc):
        r0 = c * rc
        nst = min(rc, t - r0)
        pos = pos0 + r0 + lax.broadcasted_iota(I32, (rc, 1), 0)
        for gi, w in enumerate(POOL_WINDOWS):
            sl = slice(gi * POOL_GROUP_DIM, (gi + 1) * POOL_GROUP_DIM)
            xt = ext_s[head + r0:head + r0 + rc, sl]
            acc = xt
            for r in range(1, w):
                acc = acc + ext_s[head + r0 - r:head + r0 - r + rc, sl]
            cnt = jnp.minimum(w, pos + 1).astype(F32)
            dlt = acc / cnt - xt
            y = jnp.dot(dlt.astype(BF16), pw_ref[gi], preferred_element_type=F32) * sc_ref[:, sl]
            ob_ref[0, r0:r0 + nst, sl] = y[0:nst].astype(ob_ref.dtype)


def pool_mixer(u, prefix16, pw, scale, pos0, rc, out_dtype):
    b, t, c = u.shape
    head = POOL_BUF + 1
    return pl.pallas_call(
        functools.partial(_pool_kernel, t=t, pos0=pos0, rc=rc),
        grid=(b,),
        in_specs=[pl.BlockSpec((1, t, c), lambda i: (i, 0, 0)),
                  pl.BlockSpec((1, head, c), lambda i: (i, 0, 0)),
                  pl.BlockSpec(pw.shape, lambda i: (0, 0, 0)),
                  pl.BlockSpec((1, c), lambda i: (0, 0))],
        out_specs=[pl.BlockSpec((1, t, c), lambda i: (i, 0, 0)),
                   pl.BlockSpec((1, POOL_BUF, c), lambda i: (i, 0, 0))],
        out_shape=[jax.ShapeDtypeStruct((b, t, c), out_dtype),
                   jax.ShapeDtypeStruct((b, POOL_BUF, c), F32)],
        scratch_shapes=[pltpu.VMEM((head + _round_up(t, rc), c), F32)],
        compiler_params=_cparams("parallel"),
        name="pool_mixer",
    )(u, prefix16, pw, scale.reshape(1, c))


KEY_NEG_INF = INT_MIN + 0x7FFFFF


def _key_to_float(key):
    bits = key ^ ((key >> 31) & 0x7FFFFFFF)
    return jnp.where(key <= KEY_NEG_INF, -jnp.inf, lax.bitcast_convert_type(bits, F32))


def _lane_tile(x, n):
    return x if n == 1 else jnp.concatenate([x] * n, axis=1)


def _row_total(x):
    return jnp.broadcast_to(jnp.sum(x, axis=-1, keepdims=True), x.shape)


def _dsa_prompt_kernel(q_ref, qi_ref, k_ref, v_ref, kiwi_ref, o_ref,
                       kb_s, vb_s, kib_s, qb_s, qib_s, wib_s, sc_s, sct_s, m_s, acc_s,
                       *, length, tq, kc, topk):
    nkc = kib_s.shape[0]
    nl = kc // LANES
    tqp = sct_s.shape[2]
    tqb = qb_s.shape[1] // A_GROUP
    it = pl.program_id(1)
    r0 = it * tq
    nk = (r0 + tq + kc - 1) // kc
    kf = float(topk)

    def pad_rows(x, n):
        return x if x.shape[0] == n else jnp.concatenate([x, jnp.zeros((n - x.shape[0], x.shape[1]), x.dtype)], axis=0)

    @pl.when(it == 0)
    def _():
        ones_col = (lax.broadcasted_iota(I32, (kc, LANES - A_HEAD_DIM), 1) == 0).astype(BF16)
        for c in range(nkc):
            lo = c * kc
            valid = min(kc, length - lo)
            k_t = pad_rows(k_ref[0, lo:lo + valid, :], kc).T
            ki_t = pad_rows(kiwi_ref[0, lo:lo + valid, :], kc).T
            kib_s[c] = ki_t[0:IDX_DIM].astype(BF16)
            for kh in range(A_KV_HEADS):
                sl = slice(kh * A_HEAD_DIM, (kh + 1) * A_HEAD_DIM)
                kb_s[kh, c] = k_t[sl].astype(BF16)
                vb_s[kh, c, :, 0:A_HEAD_DIM] = pad_rows(v_ref[0, lo:lo + valid, sl], kc).astype(BF16)
                vb_s[kh, c, :, A_HEAD_DIM:LANES] = ones_col

    kiwi_t = pad_rows(kiwi_ref[0, pl.ds(pl.multiple_of(r0, SUBLANES), tq), :], tqp)
    for h in range(IDX_HEADS):
        qih = pad_rows(qi_ref[0, :, h * IDX_DIM:(h + 1) * IDX_DIM], tqp)
        qib_s[h * tqp:(h + 1) * tqp, :] = qih.astype(BF16)
        wib_s[h] = jnp.broadcast_to(kiwi_t[:, IDX_DIM + h:IDX_DIM + h + 1], (tqp, LANES))
    for hd in range(A_HEADS):
        kh, g = divmod(hd, A_GROUP)
        qh = pad_rows(q_ref[0, :, hd * A_HEAD_DIM:(hd + 1) * A_HEAD_DIM] * QK_SCALE_LOG2, tqb)
        qb_s[kh, g * tqb:(g + 1) * tqb, :] = qh.astype(BF16)
        m_s[hd] = jnp.full((tq, LANES), MASK_BIAS, F32)
        acc_s[hd] = jnp.zeros((tq, LANES), F32)

    def score_chunk(c, carry):
        s_all = jnp.dot(qib_s[...], kib_s[c], preferred_element_type=F32)
        score = jnp.zeros((tqp, kc), F32)
        for h in range(IDX_HEADS):
            score = score + jnp.maximum(s_all[h * tqp:(h + 1) * tqp], 0.0) * _lane_tile(wib_s[h], nl)
        col = c * kc + lax.broadcasted_iota(I32, (tqp, kc), 1)
        pos = r0 + lax.broadcasted_iota(I32, (tqp, kc), 0)
        vis_score = jnp.where(col <= pos, score, -jnp.inf)
        sc_s[c] = vis_score
        sct_s[c] = vis_score.T
        return carry

    lax.fori_loop(0, nk, score_chunk, 0)

    def count(pred):
        def body(c, acc):
            hit = pred(sct_s[c], c).astype(F32)
            return acc + jnp.sum(hit.reshape(kc // SUBLANES, SUBLANES, tqp), axis=0)
        acc = lax.fori_loop(0, nk, body, jnp.zeros((SUBLANES, tqp), F32))
        return jnp.sum(acc, axis=0, keepdims=True)

    t0 = jnp.where(count(lambda s, c: s >= 0.0) >= kf, 0, INT_MIN).astype(I32)

    def bit_step(b, t):
        cand = t + lax.shift_left(jnp.int32(1), 30 - b)
        cand_f = _key_to_float(cand)
        return jnp.where(count(lambda s, c: s >= cand_f) >= kf, cand, t)

    thr_key = lax.fori_loop(0, 31, bit_step, t0)
    thr_q = _key_to_float(thr_key)
    tie_q = (count(lambda s, c: s >= thr_q) > kf) & (thr_key > KEY_NEG_INF)
    tie_q = tie_q & (lax.broadcasted_iota(I32, (1, tqp), 1) < tq)

    ncols = nkc * kc

    def break_ties():
        need = kf - count(lambda s, c: s > thr_q)
        kidx = lax.broadcasted_iota(I32, (kc, tqp), 0).astype(F32)
        ltri = (lax.broadcasted_iota(I32, (kc, kc), 0) >= lax.broadcasted_iota(I32, (kc, kc), 1)).astype(BF16)

        def body(c, carry):
            seen, jc = carry
            tied = sct_s[c] == thr_q
            rank = seen + jnp.dot(ltri, tied.astype(BF16), preferred_element_type=F32)
            take = tied & (rank <= need)
            last = jnp.max(jnp.where(take, kidx + (c * kc).astype(F32), -1.0), axis=0, keepdims=True)
            return rank[kc - 1:kc, :], jnp.maximum(jc, last)

        _, jc = lax.fori_loop(0, nk, body, (jnp.zeros((1, tqp), F32), jnp.full((1, tqp), -1.0, F32)))
        return jnp.where(tie_q, jc.astype(I32), ncols)

    jcut_q = lax.cond(jnp.max(tie_q.astype(F32)) > 0.0, break_ties,
                      lambda: jnp.full((1, tqp), ncols, I32))

    def per_row(x_q):
        return jnp.broadcast_to(x_q, (LANES, tqp)).T[0:tq]

    thr_t = _lane_tile(per_row(thr_q), nl)
    jcut_t = _lane_tile(per_row(jcut_q), nl)

    def attend_chunk(c, carry):
        sc = sc_s[c, 0:tq, :]
        col = c * kc + lax.broadcasted_iota(I32, (tq, kc), 1)
        pos = r0 + lax.broadcasted_iota(I32, (tq, kc), 0)
        sel = (col <= pos) & ((sc > thr_t) | ((sc == thr_t) & (col <= jcut_t)))
        bias = jnp.where(sel, 0.0, MASK_BIAS).astype(F32)
        for kh in range(A_KV_HEADS):
            l_all = jnp.dot(qb_s[kh], kb_s[kh, c], preferred_element_type=F32)
            es, stats = [], []
            for g in range(A_GROUP):
                hd = kh * A_GROUP + g
                logits = l_all[g * tqb:g * tqb + tq] + bias
                m_old = m_s[hd]
                m_new = jnp.maximum(m_old, jnp.broadcast_to(jnp.max(logits, axis=-1, keepdims=True), (tq, LANES)))
                es.append(pad_rows(jnp.exp2(logits - _lane_tile(m_new, nl)), tqb).astype(BF16))
                stats.append((hd, m_old, m_new))
            pv_all = jnp.dot(jnp.concatenate(es, axis=0), vb_s[kh, c], preferred_element_type=F32)
            for g, (hd, m_old, m_new) in enumerate(stats):
                acc_s[hd] = jnp.exp2(m_old - m_new) * acc_s[hd] + pv_all[g * tqb:g * tqb + tq]
                m_s[hd] = m_new
        return carry

    lax.fori_loop(0, nk, attend_chunk, 0)

    for hd in range(A_HEADS):
        acc = acc_s[hd]
        o_ref[0, :, hd * A_HEAD_DIM:(hd + 1) * A_HEAD_DIM] = (
            acc[:, 0:A_HEAD_DIM] / acc[:, A_HEAD_DIM:A_HEAD_DIM + 1])


def dsa_prompt(q, qi, k, v, kiwi, tq, kc):
    b, length, _ = q.shape
    assert length % tq == 0 and kc % LANES == 0
    topk = min(TOPK_MAX, length // 4)
    nkc = -(-length // kc)
    tqp = _round_up(tq, LANES)
    tqb = _round_up(tq, BF16_ROWS)
    seq = lambda c: pl.BlockSpec((1, length, c), lambda i, j: (i, 0, 0))
    til = lambda c: pl.BlockSpec((1, tq, c), lambda i, j: (i, j, 0))
    return pl.pallas_call(
        functools.partial(_dsa_prompt_kernel, length=length, tq=tq, kc=kc, topk=topk),
        grid=(b, length // tq),
        in_specs=[til(q.shape[2]), til(qi.shape[2]), seq(k.shape[2]), seq(v.shape[2]), seq(kiwi.shape[2])],
        out_specs=til(q.shape[2]),
        out_shape=jax.ShapeDtypeStruct(q.shape, F32),
        scratch_shapes=[pltpu.VMEM((A_KV_HEADS, nkc, A_HEAD_DIM, kc), BF16),
                        pltpu.VMEM((A_KV_HEADS, nkc, kc, LANES), BF16),
                        pltpu.VMEM((nkc, IDX_DIM, kc), BF16),
                        pltpu.VMEM((A_KV_HEADS, A_GROUP * tqb, A_HEAD_DIM), BF16),
                        pltpu.VMEM((IDX_HEADS * tqp, IDX_DIM), BF16),
                        pltpu.VMEM((IDX_HEADS, tqp, LANES), F32),
                        pltpu.VMEM((nkc, tqp, kc), F32),
                        pltpu.VMEM((nkc, kc, tqp), F32),
                        pltpu.VMEM((A_HEADS, tq, LANES), F32),
                        pltpu.VMEM((A_HEADS, tq, LANES), F32)],
        compiler_params=_cparams("parallel", "arbitrary"),
        name="dsa_prompt",
    )(q, qi, k, v, kiwi)


def _dsa_sample_score_kernel(pt_ref, qi_ref, wi_ref, kinew_ref, *rest, pp, t):
    pages = rest[:pp]
    sc_ref, kin_s = rest[pp:]
    j = pl.program_id(1)
    nsteps = pl.num_programs(1)
    cw = pp * PAGE_SIZE
    past = sc_ref.shape[2] - LANES
    qi = qi_ref[0].astype(BF16)
    wi = wi_ref[0]

    def scores(s):
        s = jnp.maximum(s, 0.0) * wi
        return jnp.sum(s.reshape(t, IDX_HEADS, s.shape[1]), axis=1)

    ki_t = jnp.concatenate([p[0, 0] for p in pages], axis=1).astype(BF16)
    sc_chunk = scores(jnp.dot(qi, ki_t, preferred_element_type=F32))
    for jj in range(past // cw):
        @pl.when(j == jj)
        def _(jj=jj):
            sc_ref[0, :, jj * cw:(jj + 1) * cw] = sc_chunk

    @pl.when(j == nsteps - 1)
    def _():
        kin_s[...] = jnp.zeros(kin_s.shape, F32)
        kin_s[0:t, :] = kinew_ref[0]
        snew = scores(_dot_nt(qi, kin_s[...].astype(BF16)))
        cn = lax.broadcasted_iota(I32, (t, LANES), 1)
        rn = lax.broadcasted_iota(I32, (t, LANES), 0)
        sc_ref[0, :, past:past + LANES] = jnp.where(cn <= rn, snew, -jnp.inf)


def _dsa_select_kernel(sc_ref, thr_ref, jcut_ref, *, topk):
    rows, ncols = sc_ref.shape
    kf = float(topk)

    def count(pred):
        acc = jnp.zeros((rows, LANES), F32)
        for c0 in range(0, ncols, LANES):
            acc = acc + pred(sc_ref[:, c0:c0 + LANES], c0).astype(F32)
        return _row_total(acc)

    t0 = jnp.where(count(lambda s, c0: s >= 0.0) >= kf, 0, INT_MIN).astype(I32)

    def bit_step(b, t):
        cand = t + lax.shift_left(jnp.int32(1), 30 - b)
        cand_f = _key_to_float(cand)
        return jnp.where(count(lambda s, c0: s >= cand_f) >= kf, cand, t)

    thr_key = lax.fori_loop(0, 31, bit_step, t0)
    thr = _key_to_float(thr_key)
    tie = (count(lambda s, c0: s >= thr) > kf) & (thr_key > KEY_NEG_INF)
    nbits = max(1, (ncols - 1).bit_length())

    def break_ties():
        need = kf - count(lambda s, c0: s > thr)
        lane = lax.broadcasted_iota(I32, (rows, LANES), 1)

        def idx_step(b, jc):
            cand = jc + lax.shift_left(jnp.int32(1), nbits - 1 - b)
            n_lt = count(lambda s, c0: (s == thr) & (c0 + lane < cand))
            return jnp.where(n_lt < need, cand, jc)

        jc = lax.fori_loop(0, nbits, idx_step, jnp.zeros((rows, LANES), I32))
        return jnp.where(tie, jc, ncols)

    thr_ref[...] = thr
    jcut_ref[...] = lax.cond(jnp.max(tie.astype(F32)) > 0.0, break_ties,
                             lambda: jnp.full((rows, LANES), ncols, I32))


def _dsa_sample_attn_kernel(pt_ref, q_ref, sc_ref, scnew_ref, thr_ref, jcut_ref, knew_ref, vnew_ref,
                            *rest, pp, t):
    kpages = rest[:pp]
    vpages = rest[pp:2 * pp]
    o_ref, m_s, l_s, acc_s, new_s = rest[2 * pp:]
    j = pl.program_id(1)
    nsteps = pl.num_programs(1)
    cw = pp * PAGE_SIZE

    def bias_of(sc, c0):
        n = sc.shape[1] // LANES
        thr, jcut = _lane_tile(thr_ref[0], n), _lane_tile(jcut_ref[0], n)
        col = c0 + lax.broadcasted_iota(I32, sc.shape, 1)
        sel = (sc > -jnp.inf) & ((sc > thr) | ((sc == thr) & (col <= jcut)))
        bias = jnp.where(sel, 0.0, MASK_BIAS).astype(F32)
        return jnp.concatenate([bias] * A_GROUP, axis=0)

    @pl.when(j == 0)
    def _():
        m_s[...] = jnp.full(m_s.shape, MASK_BIAS, F32)
        l_s[...] = jnp.zeros(l_s.shape, F32)
        acc_s[...] = jnp.zeros(acc_s.shape, F32)

    def update(kh, kb, vb, bias, key_major):
        qh = (q_ref[0, kh] * (A_HEAD_DIM ** -0.5)).astype(BF16)
        qk = _dot_nt(qh, kb) if key_major else jnp.dot(qh, kb, preferred_element_type=F32)
        logits = qk + bias
        m_old = m_s[kh]
        m_new = jnp.maximum(m_old, jnp.max(logits, axis=-1, keepdims=True))
        alpha = jnp.exp(m_old - m_new)
        e = jnp.exp(logits - m_new)
        p = e.astype(BF16)
        pv = jnp.dot(p, vb, preferred_element_type=F32) if key_major else _dot_nt(p, vb)
        l_s[kh] = alpha * l_s[kh] + jnp.sum(e, axis=-1, keepdims=True)
        acc_s[kh] = alpha * acc_s[kh] + pv
        m_s[kh] = m_new

    bias = bias_of(sc_ref[0], j * cw)
    for kh in range(A_KV_HEADS):
        k_t = jnp.concatenate([p[0, 0, kh] for p in kpages], axis=1).astype(BF16)
        v_t = jnp.concatenate([p[0, 0, kh] for p in vpages], axis=1).astype(BF16)
        update(kh, k_t, v_t, bias, False)

    @pl.when(j == nsteps - 1)
    def _():
        bias_new = bias_of(scnew_ref[0], nsteps * cw)
        for kh in range(A_KV_HEADS):
            sl = slice(kh * A_HEAD_DIM, (kh + 1) * A_HEAD_DIM)
            new_s[...] = jnp.zeros(new_s.shape, F32)
            new_s[0, 0:t, :] = knew_ref[0, :, sl]
            new_s[1, 0:t, :] = vnew_ref[0, :, sl]
            update(kh, new_s[0].astype(BF16), new_s[1].astype(BF16), bias_new, True)
            o_ref[0, kh] = acc_s[kh] / l_s[kh]


def dsa_sample(q_g, qi_t, wi_t, ki_new, k_new, v_new, cache_k, cache_v, cache_ki, page_flat, layer, pp):
    db, _, gt, hd = q_g.shape
    t = gt // A_GROUP
    npg = page_flat.shape[0] // db
    assert npg % pp == 0
    nsteps = npg // pp
    past = npg * PAGE_SIZE
    ncols = past + LANES
    cw = pp * PAGE_SIZE
    kvw = k_new.shape[2]

    def page_spec(shape, i):
        nz = (0,) * len(shape)
        return pl.BlockSpec((1, 1) + shape, lambda b, j, pt, i=i: (layer, pt[b * npg + j * pp + i]) + nz)

    ki_page = (IDX_DIM, PAGE_SIZE)
    kv_page = (A_KV_HEADS, hd, PAGE_SIZE)

    sc = pl.pallas_call(
        functools.partial(_dsa_sample_score_kernel, pp=pp, t=t),
        grid_spec=pltpu.PrefetchScalarGridSpec(
            num_scalar_prefetch=1,
            grid=(db, nsteps),
            in_specs=[pl.BlockSpec((1, t * IDX_HEADS, IDX_DIM), lambda b, j, pt: (b, 0, 0)),
                      pl.BlockSpec((1, t * IDX_HEADS, 1), lambda b, j, pt: (b, 0, 0)),
                      pl.BlockSpec((1, t, IDX_DIM), lambda b, j, pt: (b, 0, 0))]
                     + [page_spec(ki_page, i) for i in range(pp)],
            out_specs=pl.BlockSpec((1, t, ncols), lambda b, j, pt: (b, 0, 0)),
            scratch_shapes=[pltpu.VMEM((LANES, IDX_DIM), F32)]),
        out_shape=jax.ShapeDtypeStruct((db, t, ncols), F32),
        compiler_params=_cparams("parallel", "arbitrary"),
        name="dsa_sample_score",
    )(page_flat, qi_t, wi_t, ki_new, *([cache_ki] * pp))

    thr, jcut = pl.pallas_call(
        functools.partial(_dsa_select_kernel, topk=min(TOPK_MAX, (past + t) // 4)),
        out_shape=[jax.ShapeDtypeStruct((db * t, LANES), F32), jax.ShapeDtypeStruct((db * t, LANES), I32)],
        compiler_params=pltpu.CompilerParams(vmem_limit_bytes=VMEM_LIMIT_BYTES),
        name="dsa_sample_select",
    )(sc.reshape(db * t, ncols))
    thr, jcut = thr.reshape(db, t, LANES), jcut.reshape(db, t, LANES)

    out = pl.pallas_call(
        functools.partial(_dsa_sample_attn_kernel, pp=pp, t=t),
        grid_spec=pltpu.PrefetchScalarGridSpec(
            num_scalar_prefetch=1,
            grid=(db, nsteps),
            in_specs=[pl.BlockSpec((1, A_KV_HEADS, gt, hd), lambda b, j, pt: (b, 0, 0, 0)),
                      pl.BlockSpec((1, t, cw), lambda b, j, pt: (b, 0, j)),
                      pl.BlockSpec((1, t, LANES), lambda b, j, pt: (b, 0, past // LANES)),
                      pl.BlockSpec((1, t, LANES), lambda b, j, pt: (b, 0, 0)),
                      pl.BlockSpec((1, t, LANES), lambda b, j, pt: (b, 0, 0)),
                      pl.BlockSpec((1, t, kvw), lambda b, j, pt: (b, 0, 0)),
                      pl.BlockSpec((1, t, kvw), lambda b, j, pt: (b, 0, 0))]
                     + [page_spec(kv_page, i) for i in range(pp)]
                     + [page_spec(kv_page, i) for i in range(pp)],
            out_specs=pl.BlockSpec((1, A_KV_HEADS, gt, hd), lambda b, j, pt: (b, 0, 0, 0)),
            scratch_shapes=[pltpu.VMEM((A_KV_HEADS, gt, 1), F32),
                            pltpu.VMEM((A_KV_HEADS, gt, 1), F32),
                            pltpu.VMEM((A_KV_HEADS, gt, hd), F32),
                            pltpu.VMEM((2, LANES, hd), F32)]),
        out_shape=jax.ShapeDtypeStruct((db, A_KV_HEADS, gt, hd), F32),
        compiler_params=_cparams("parallel", "arbitrary"),
        name="dsa_sample_attn",
    )(page_flat, q_g, sc, sc, thr, jcut, k_new, v_new, *([cache_k] * pp), *([cache_v] * pp))
    return out


def _gla_kernel(*refs, tb, tv, chunk, sub, has_s0):
    if has_s0:
        (q_ref, k_ref, v_ref, g_ref, gd_ref, wgu_ref, bg_ref, gn_ref, s0_ref,
         on_ref, sout_ref, b_s, dec_s, st_s) = refs
    else:
        q_ref, k_ref, v_ref, g_ref, gd_ref, wgu_ref, bg_ref, gn_ref, on_ref, sout_ref, b_s, dec_s, st_s = refs
    jb = pl.program_id(1)
    nb = pl.num_programs(1)
    nfull, rem = tb // chunk, tb % chunk
    spans = [(ic * chunk, chunk) for ic in range(nfull)] + ([(nfull * chunk, rem)] if rem else [])

    @pl.when(jb == 0)
    def _():
        for h in range(GLA_HEADS):
            st_s[h] = s0_ref[0, 0, h] if has_s0 else jnp.zeros((GLA_DK, GLA_DV), F32)

    x = jnp.dot(gd_ref[0], wgu_ref[...], precision=HIGHEST, preferred_element_type=F32) + bg_ref[...]
    la = (jnp.minimum(x, 0.0) - jnp.log1p(jnp.exp(-jnp.abs(x)))) * (1.0 / GLA_GATE_NORM)
    if tv is not None:
        row = jb * tb + lax.broadcasted_iota(I32, (tb, 1), 0)
        la = jnp.where(row < tv, la, 0.0)
    for ic, (r0, c) in enumerate(spans):
        lac = la[r0:r0 + c]
        tri = (lax.broadcasted_iota(I32, (c, c), 0) >= lax.broadcasted_iota(I32, (c, c), 1)).astype(BF16)
        hi = lac.astype(BF16)
        r1 = lac - hi.astype(F32)
        mid = r1.astype(BF16)
        lo = (r1 - mid.astype(F32)).astype(BF16)
        bc = (jnp.dot(tri, hi, preferred_element_type=F32) + jnp.dot(tri, mid, preferred_element_type=F32)
              + jnp.dot(tri, lo, preferred_element_type=F32))
        b_s[r0:r0 + c, :] = bc
        dec_s[ic] = jnp.exp(jnp.broadcast_to(bc[c - 1:c, :], (LANES, bc.shape[1])).T)

    def do_chunk(ic, r0, c):
        rows = pl.ds(r0, c)
        b = b_s[rows, :]
        qc = q_ref[0, rows, :] * (GLA_DK ** -0.5)
        kc = k_ref[0, rows, :]
        if tv is not None:
            rowc = jb * tb + r0 + lax.broadcasted_iota(I32, (c, 1), 0)
            kc = jnp.where(rowc < tv, kc, 0.0)
        vc = v_ref[0, rows, :].astype(BF16)
        gc = g_ref[0, rows, :]
        dec = dec_s[ic]
        causal = lax.broadcasted_iota(I32, (c, LANES), 1) <= lax.broadcasted_iota(I32, (c, LANES), 0)
        heads, states = [], []
        for h in range(GLA_HEADS):
            sk = slice(h * GLA_DK, (h + 1) * GLA_DK)
            sv = slice(h * GLA_DV, (h + 1) * GLA_DV)
            bh, qh, kh, vh = b[:, sk], qc[:, sk], kc[:, sk], vc[:, sv]
            st = st_s[h]
            att = []
            for i in range(c // sub):
                lo, hi = i * sub, (i + 1) * sub
                ref_b = bh[lo - 1:lo, :] if i > 0 else jnp.zeros((1, GLA_DK), F32)
                qt = (qh[lo:hi] * jnp.exp(bh[lo:hi] - ref_b)).astype(BF16)
                kt = (kh[0:hi] * jnp.exp(ref_b - bh[0:hi])).astype(BF16)
                kt = jnp.concatenate([kt, jnp.zeros((LANES - hi, GLA_DK), BF16)], axis=0)
                att.append(_dot_nt(qt, kt))
            att = jnp.concatenate(att, axis=0) if len(att) > 1 else att[0]
            att = jnp.where(causal, att, 0.0).astype(BF16)
            lhs = jnp.concatenate([(qh * jnp.exp(bh)).astype(BF16), att], axis=1)
            vpad = vh if c == LANES else jnp.concatenate([vh, jnp.zeros((LANES - c, GLA_DV), BF16)], axis=0)
            rhs = jnp.concatenate([st.astype(BF16), vpad], axis=0)
            o = jnp.dot(lhs, rhs, preferred_element_type=F32)
            bl = bh[c - 1:c, :]
            kd = (kh * jnp.exp(bl - bh)).astype(BF16)
            dh = dec[sk, :]
            states.append(jnp.concatenate([dh, dh], axis=1) * st + _dot_tn(kd, vh))
            gh = gc[:, sv]
            heads.append(_rms(o, gn_ref[...]) * (gh * jax.nn.sigmoid(gh)))
        on_ref[0, rows, :] = jnp.concatenate(heads, axis=1).astype(on_ref.dtype)
        for h in range(GLA_HEADS):
            st_s[h] = states[h]

    if nfull == 1:
        do_chunk(0, 0, chunk)
    elif nfull > 1:
        def body(ic, carry):
            do_chunk(ic, pl.multiple_of(ic * chunk, BF16_ROWS), chunk)
            return carry
        lax.fori_loop(0, nfull, body, 0)
    if rem:
        do_chunk(nfull, nfull * chunk, rem)

    @pl.when(jb == nb - 1)
    def _():
        for h in range(GLA_HEADS):
            sout_ref[0, h] = st_s[h]


def gla_mixer(q, k, v, g, gd, wgu, bg, gn, s0, s0_layer, tb, chunk, tv):
    b, t, _ = q.shape
    sub = BF16_ROWS
    assert t % tb == 0 and tb % sub == 0 and chunk % sub == 0 and chunk <= LANES and GLA_DV == 2 * LANES
    nspans = -(-tb // chunk)
    blk = lambda c: pl.BlockSpec((1, tb, c), lambda i, j: (i, j, 0))
    cst = lambda a: pl.BlockSpec(a.shape, lambda i, j: (0,) * a.ndim)
    st_spec = pl.BlockSpec((1, GLA_HEADS, GLA_DK, GLA_DV), lambda i, j: (i, 0, 0, 0))
    bg2, gn2 = bg.reshape(1, -1), gn.reshape(1, -1)
    args = [q, k, v, g, gd, wgu, bg2, gn2]
    in_specs = [blk(q.shape[2]), blk(k.shape[2]), blk(v.shape[2]), blk(g.shape[2]), blk(gd.shape[2]),
                cst(wgu), cst(bg2), cst(gn2)]
    if s0 is not None:
        args.append(s0)
        in_specs.append(pl.BlockSpec((1, 1, GLA_HEADS, GLA_DK, GLA_DV), lambda i, j: (s0_layer, i, 0, 0, 0)))
    return pl.pallas_call(
        functools.partial(_gla_kernel, tb=tb, tv=tv, chunk=chunk, sub=sub, has_s0=s0 is not None),
        grid=(b, t // tb),
        in_specs=in_specs,
        out_specs=[blk(v.shape[2]), st_spec],
        out_shape=[jax.ShapeDtypeStruct(v.shape, BF16),
                   jax.ShapeDtypeStruct((b, GLA_HEADS, GLA_DK, GLA_DV), F32)],
        scratch_shapes=[pltpu.VMEM((tb, GLA_HEADS * GLA_DK), F32),
                        pltpu.VMEM((nspans, GLA_HEADS * GLA_DK, LANES), F32),
                        pltpu.VMEM((GLA_HEADS, GLA_DK, GLA_DV), F32)],
        compiler_params=_cparams("parallel", "arbitrary"),
        name="gla_mixer",
    )(*args)


def _pick_tile(n, candidates):
    for c in candidates:
        if n % c == 0:
            return c
    return n


def kernel(x_prompt, x_sample, cache_k, cache_v, cache_kidx, state_pool, state_gla, page_table,
           meta_tokens, norm_mix, norm_ffn, norm_final, w_in_even, w_out_even, pool_w, pool_scale,
           w_in_odd, gla_w_gate_up, gla_b_gate, gla_norm, w_out_odd, ffn_w_gate, ffn_w_up, ffn_w_down):
    bsz, seq, d = x_prompt.shape
    db, dseq, _ = x_sample.shape
    depth = norm_mix.shape[0]
    length = seq + N_META
    npg = page_table.shape[1]
    past = npg * PAGE_SIZE
    a_width = A_HEADS * A_HEAD_DIM
    kv_width = A_KV_HEADS * A_HEAD_DIM
    pool_width = pool_scale.shape[1]
    gk = GLA_HEADS * GLA_DK
    gv = GLA_HEADS * GLA_DV
    d_ff = ffn_w_gate.shape[2]

    meta = jnp.broadcast_to(meta_tokens[None], (bsz, N_META, d))
    xp = jnp.concatenate([meta, x_prompt], axis=1).reshape(bsz * length, d)
    xs = x_sample.reshape(db * dseq, d)
    n_p, n_s = xp.shape[0], xs.shape[0]

    tm_p = _pick_tile(n_p, (688, 512, 256, 128, 64, 32, 16, 8))
    tm_s = _pick_tile(n_s, (128, 64, 32, 16, 8))
    tf = _pick_tile(d_ff, (1408, 1024, 512, 256, 128))
    tq = _pick_tile(length, (344, 256, 128, 64, 32, 16, 8))
    pool_rc = _pick_tile(length, (48, 32, 16, 8))
    gla_tb = _pick_tile(length, (688, 512, 256, 128, 64, 32, 16))
    pp = _pick_tile(npg, (16, 8, 4, 2, 1))

    e_off = [0]
    for s in (a_width, kv_width, kv_width, IDX_HEADS * IDX_DIM, IDX_DIM, IDX_HEADS, pool_width):
        e_off.append(e_off[-1] + s)
    kiwi_pad = LANES - IDX_DIM - IDX_HEADS
    even_widths = (a_width, kv_width, kv_width, IDX_HEADS * IDX_DIM, pool_width, LANES)
    odd_widths = (gk, gk, gv, gv, LANES)

    cache_kt = cache_k.transpose(0, 1, 3, 4, 2)
    cache_vt = cache_v.transpose(0, 1, 3, 4, 2)
    cache_kit = cache_kidx.transpose(0, 1, 3, 2)
    page_flat = page_table.reshape(-1).astype(I32)

    kp_l, vp_l, kip_l, bp_l, ks_l, vs_l, kis_l, bs_l, sp_l, ss_l = ([] for _ in range(10))
    for l in range(depth):
        if l % 2 == 0:
            e = l // 2
            w = w_in_even[e]
            w_in = jnp.concatenate(
                [w[:, e_off[0]:e_off[4]], w[:, e_off[6]:e_off[7]], w[:, e_off[4]:e_off[6]],
                 jnp.zeros((d, kiwi_pad), w.dtype)], axis=1).astype(BF16)
            w_out = w_out_even[e].astype(BF16)
            pw = pool_w[e].astype(BF16)

            q, k, v, qi, u, kiwi = norm_matmul(xp, norm_mix[l], w_in, even_widths, tm_p)
            r3 = lambda a: a.reshape(bsz, length, a.shape[1])
            oa = dsa_prompt(r3(q), r3(qi), r3(k), r3(v), r3(kiwi), tq, DSA_KEY_CHUNK)
            ob, buf = pool_mixer(r3(u), jnp.zeros((bsz, POOL_BUF + 1, pool_width), F32), pw,
                                 pool_scale[e], 0, pool_rc, BF16)
            mix_p = [(oa.reshape(n_p, a_width), w_out[:a_width]),
                     (ob.reshape(n_p, pool_width), w_out[a_width:])]
            kp_l.append(k.reshape(bsz, length, A_KV_HEADS, A_HEAD_DIM))
            vp_l.append(v.reshape(bsz, length, A_KV_HEADS, A_HEAD_DIM))
            kip_l.append(kiwi[:, :IDX_DIM].reshape(bsz, length, IDX_DIM))
            bp_l.append(buf)

            q, k, v, qi, u, kiwi = norm_matmul(xs, norm_mix[l], w_in, even_widths, tm_s)
            s3 = lambda a: a.reshape(db, dseq, a.shape[1])
            q_g = q.reshape(db, dseq, A_KV_HEADS, A_GROUP, A_HEAD_DIM).transpose(0, 2, 3, 1, 4)
            q_g = q_g.reshape(db, A_KV_HEADS, A_GROUP * dseq, A_HEAD_DIM)
            qi_t = qi.reshape(db, dseq * IDX_HEADS, IDX_DIM)
            wi_t = kiwi[:, IDX_DIM:IDX_DIM + IDX_HEADS].reshape(db, dseq * IDX_HEADS, 1)
            ki_new = kiwi[:, :IDX_DIM].reshape(db, dseq, IDX_DIM)
            oa = dsa_sample(q_g, qi_t, wi_t, ki_new, s3(k), s3(v), cache_kt, cache_vt, cache_kit,
                            page_flat, e, pp)
            oa = oa.reshape(db, A_KV_HEADS, A_GROUP, dseq, A_HEAD_DIM).transpose(0, 3, 1, 2, 4)
            prefix = jnp.pad(state_pool[e], ((0, 0), (1, 0), (0, 0)))
            ob, buf = pool_mixer(s3(u), prefix, pw, pool_scale[e], past, BF16_ROWS, F32)
            mix_s = [(oa.reshape(n_s, a_width), w_out[:a_width]),
                     (ob.reshape(n_s, pool_width), w_out[a_width:])]
            ks_l.append(k.reshape(db, dseq, A_KV_HEADS, A_HEAD_DIM))
            vs_l.append(v.reshape(db, dseq, A_KV_HEADS, A_HEAD_DIM))
            kis_l.append(ki_new)
            bs_l.append(buf)
        else:
            o = l // 2
            w = w_in_odd[o]
            w_in = jnp.concatenate([w, jnp.zeros((d, LANES - GLA_GATE_RANK), w.dtype)], axis=1).astype(BF16)
            w_out = w_out_odd[o].astype(BF16)
            wgu = jnp.pad(gla_w_gate_up[o], ((0, LANES - GLA_GATE_RANK), (0, 0)))

            q, k, v, g, gd = norm_matmul(xp, norm_mix[l], w_in, odd_widths, tm_p)
            r3 = lambda a: a.reshape(bsz, length, a.shape[1])
            on, s_p = gla_mixer(r3(q), r3(k), r3(v), r3(g), r3(gd), wgu, gla_b_gate[o], gla_norm[o],
                                None, None, gla_tb, GLA_CHUNK, None)
            mix_p = [(on.reshape(n_p, gv), w_out)]
            sp_l.append(s_p)

            q, k, v, g, gd = norm_matmul(xs, norm_mix[l], w_in, odd_widths, tm_s)
            tpad = _round_up(dseq, BF16_ROWS)
            p3 = lambda a: jnp.pad(a.reshape(db, dseq, a.shape[1]), ((0, 0), (0, tpad - dseq), (0, 0)))
            on, s_s = gla_mixer(p3(q), p3(k), p3(v), p3(g), p3(gd), wgu, gla_b_gate[o], gla_norm[o],
                                state_gla, o, tpad, tpad, dseq)
            mix_s = [(on[:, :dseq].reshape(n_s, gv), w_out)]
            ss_l.append(s_s)

        wg, wu, wd = ffn_w_gate[l].astype(BF16), ffn_w_up[l].astype(BF16), ffn_w_down[l].astype(BF16)
        xp = mix_ffn(xp, mix_p, norm_ffn[l], wg, wu, wd, tm_p, tf)
        xs = mix_ffn(xs, mix_s, norm_ffn[l], wg, wu, wd, tm_s, tf)

    y_prompt = final_norm(xp.reshape(bsz, length, d), norm_final, N_META)
    y_sample = final_norm(xs.reshape(1, n_s, d), norm_final, 0).reshape(db, dseq, d)
    return (y_prompt, y_sample,
            jnp.stack(kp_l), jnp.stack(vp_l), jnp.stack(kip_l),
            jnp.stack(ks_l), jnp.stack(vs_l), jnp.stack(kis_l),
            jnp.stack(bp_l), jnp.stack(bs_l),
            jnp.stack(sp_l), jnp.stack(ss_l))
```

```python
import functools

import jax
import jax.numpy as jnp
from jax import lax
from jax.experimental import pallas as pl
from jax.experimental.pallas import tpu as pltpu

F32 = jnp.float32
BF16 = jnp.bfloat16
I32 = jnp.int32

NORM_EPS = 1e-6
N_META = 16
A_HEADS = 8
A_KV_HEADS = 2
A_GROUP = A_HEADS // A_KV_HEADS
A_HEAD_DIM = 64
IDX_HEADS = 8
IDX_DIM = 64
TOPK_MAX = 256
PAGE_SIZE = 128
POOL_WINDOWS = (2, 4, 8, 16)
POOL_GROUP_DIM = 128
POOL_BUF = 15
GLA_HEADS = 4
GLA_DK = 128
GLA_DV = 256
GLA_GATE_RANK = 16
GLA_GATE_NORM = 16.0

LANES = 128
SUBLANES = 8
BF16_ROWS = 16
VMEM_LIMIT_BYTES = 56 * 1024 * 1024

DSA_KEY_CHUNK = 2 * LANES
GLA_CHUNK = 96

INT_MIN = -(2 ** 31)
MASK_BIAS = -1e30
HIGHEST = lax.Precision.HIGHEST
QK_SCALE_LOG2 = (A_HEAD_DIM ** -0.5) * 1.4426950408889634


def _round_up(n, m):
    return (n + m - 1) // m * m


def _cparams(*sem):
    return pltpu.CompilerParams(dimension_semantics=sem, vmem_limit_bytes=VMEM_LIMIT_BYTES)


def _rms(x, g):
    ms = jnp.mean(x * x, axis=-1, keepdims=True)
    return (x * lax.rsqrt(ms + NORM_EPS)) * g


def _dot_nt(a, b):
    return lax.dot_general(a, b, (((1,), (1,)), ((), ())), preferred_element_type=F32)


def _dot_tn(a, b):
    return lax.dot_general(a, b, (((0,), (0,)), ((), ())), preferred_element_type=F32)


def _norm_matmul_kernel(x_ref, g_ref, w_ref, *o_refs, widths):
    h = _rms(x_ref[...], g_ref[...]).astype(BF16)
    off = 0
    for o_ref, wd in zip(o_refs, widths):
        o_ref[...] = jnp.dot(h, w_ref[:, off:off + wd], preferred_element_type=F32)
        off += wd


def norm_matmul(x, g, w, widths, tm):
    n, d = x.shape
    assert n % tm == 0 and sum(widths) == w.shape[1]
    return pl.pallas_call(
        functools.partial(_norm_matmul_kernel, widths=widths),
        grid=(n // tm,),
        in_specs=[pl.BlockSpec((tm, d), lambda i: (i, 0)),
                  pl.BlockSpec((1, d), lambda i: (0, 0)),
                  pl.BlockSpec((d, w.shape[1]), lambda i: (0, 0))],
        out_specs=[pl.BlockSpec((tm, wd), lambda i: (i, 0)) for wd in widths],
        out_shape=[jax.ShapeDtypeStruct((n, wd), F32) for wd in widths],
        compiler_params=_cparams("parallel"),
        name="norm_matmul",
    )(x, g.reshape(1, d), w)


def _mix_ffn_kernel(*refs, n_mix):
    x_ref = refs[0]
    mix = refs[1:1 + 2 * n_mix]
    g_ref, wg_ref, wu_ref, wd_ref, o_ref, h_s = refs[1 + 2 * n_mix:]
    j = pl.program_id(1)

    @pl.when(j == 0)
    def _():
        xm = x_ref[...]
        for m in range(n_mix):
            a_ref, w_ref = mix[2 * m], mix[2 * m + 1]
            xm = xm + jnp.dot(a_ref[...].astype(BF16), w_ref[...], preferred_element_type=F32)
        o_ref[...] = xm
        h_s[...] = _rms(xm, g_ref[...]).astype(BF16)

    h = h_s[...]
    gate = jnp.dot(h, wg_ref[...], preferred_element_type=F32)
    up = jnp.dot(h, wu_ref[...], preferred_element_type=F32)
    act = (gate * jax.nn.sigmoid(gate) * up).astype(BF16)
    o_ref[...] += jnp.dot(act, wd_ref[...], preferred_element_type=F32)


def mix_ffn(x, mixes, g, wg, wu, wd, tm, tf):
    n, d = x.shape
    f = wg.shape[1]
    assert n % tm == 0 and f % tf == 0
    in_specs = [pl.BlockSpec((tm, d), lambda i, j: (i, 0))]
    args = [x]
    for a, w in mixes:
        in_specs += [pl.BlockSpec((tm, a.shape[1]), lambda i, j: (i, 0)),
                     pl.BlockSpec(w.shape, lambda i, j: (0, 0))]
        args += [a, w]
    in_specs += [pl.BlockSpec((1, d), lambda i, j: (0, 0)),
                 pl.BlockSpec((d, tf), lambda i, j: (0, j)),
                 pl.BlockSpec((d, tf), lambda i, j: (0, j)),
                 pl.BlockSpec((tf, d), lambda i, j: (j, 0))]
    args += [g.reshape(1, d), wg, wu, wd]
    return pl.pallas_call(
        functools.partial(_mix_ffn_kernel, n_mix=len(mixes)),
        grid=(n // tm, f // tf),
        in_specs=in_specs,
        out_specs=pl.BlockSpec((tm, d), lambda i, j: (i, 0)),
        out_shape=jax.ShapeDtypeStruct((n, d), F32),
        scratch_shapes=[pltpu.VMEM((tm, d), BF16)],
        compiler_params=_cparams("parallel", "arbitrary"),
        name="mix_ffn",
    )(*args)


def _final_norm_kernel(x_ref, g_ref, o_ref, *, skip, rows, chunk):
    for c in range(rows // chunk):
        x = x_ref[0, skip + c * chunk: skip + (c + 1) * chunk, :]
        o_ref[0, c * chunk:(c + 1) * chunk, :] = _rms(x, g_ref[...])


def final_norm(x, g, skip):
    b, t, d = x.shape
    rows = t - skip
    chunk = 256 if rows % 256 == 0 else rows
    return pl.pallas_call(
        functools.partial(_final_norm_kernel, skip=skip, rows=rows, chunk=chunk),
        grid=(b,),
        in_specs=[pl.BlockSpec((1, t, d), lambda i: (i, 0, 0)),
                  pl.BlockSpec((1, d), lambda i: (0, 0))],
        out_specs=pl.BlockSpec((1, rows, d), lambda i: (i, 0, 0)),
        out_shape=jax.ShapeDtypeStruct((b, rows, d), F32),
        compiler_params=_cparams("parallel"),
        name="final_norm",
    )(x, g.reshape(1, d))


def _pool_kernel(u_ref, pre_ref, pw_ref, sc_ref, ob_ref, buf_ref, ext_s, *, t, pos0, rc):
    head = POOL_BUF + 1
    tproc = _round_up(t, rc)
    ext_s[0:head, :] = pre_ref[0]
    ext_s[head:head + t, :] = u_ref[0]
    if tproc > t:
        ext_s[head + t:head + tproc, :] = jnp.zeros((tproc - t, ext_s.shape[1]), F32)
    buf_ref[0] = ext_s[t + 1:t + head, :]
    for c in range(tproc // rc):
        r0 = c * rc
        nst = min(rc, t - r0)
        pos = pos0 + r0 + lax.broadcasted_iota(I32, (rc, 1), 0)
        for gi, w in enumerate(POOL_WINDOWS):
            sl = slice(gi * POOL_GROUP_DIM, (gi + 1) * POOL_GROUP_DIM)
            xt = ext_s[head + r0:head + r0 + rc, sl]
            acc = xt
            for r in range(1, w):
                acc = acc + ext_s[head + r0 - r:head + r0 - r + rc, sl]
            cnt = jnp.minimum(w, pos + 1).astype(F32)
            dlt = acc / cnt - xt
            y = jnp.dot(dlt.astype(BF16), pw_ref[gi], preferred_element_type=F32) * sc_ref[:, sl]
            ob_ref[0, r0:r0 + nst, sl] = y[0:nst].astype(ob_ref.dtype)


def pool_mixer(u, prefix16, pw, scale, pos0, rc, out_dtype):
    b, t, c = u.shape
    head = POOL_BUF + 1
    return pl.pallas_call(
        functools.partial(_pool_kernel, t=t, pos0=pos0, rc=rc),
        grid=(b,),
        in_specs=[pl.BlockSpec((1, t, c), lambda i: (i, 0, 0)),
                  pl.BlockSpec((1, head, c), lambda i: (i, 0, 0)),
                  pl.BlockSpec(pw.shape, lambda i: (0, 0, 0)),
                  pl.BlockSpec((1, c), lambda i: (0, 0))],
        out_specs=[pl.BlockSpec((1, t, c), lambda i: (i, 0, 0)),
                   pl.BlockSpec((1, POOL_BUF, c), lambda i: (i, 0, 0))],
        out_shape=[jax.ShapeDtypeStruct((b, t, c), out_dtype),
                   jax.ShapeDtypeStruct((b, POOL_BUF, c), F32)],
        scratch_shapes=[pltpu.VMEM((head + _round_up(t, rc), c), F32)],
        compiler_params=_cparams("parallel"),
        name="pool_mixer",
    )(u, prefix16, pw, scale.reshape(1, c))


KEY_NEG_INF = INT_MIN + 0x7FFFFF


def _key_to_float(key):
    bits = key ^ ((key >> 31) & 0x7FFFFFFF)
    return jnp.where(key <= KEY_NEG_INF, -jnp.inf, lax.bitcast_convert_type(bits, F32))


def _lane_tile(x, n):
    return x if n == 1 else jnp.concatenate([x] * n, axis=1)


def _row_total(x):
    return jnp.broadcast_to(jnp.sum(x, axis=-1, keepdims=True), x.shape)


def _dsa_prompt_kernel(q_ref, qi_ref, k_ref, v_ref, kiwi_ref, o_ref,
                       kb_s, vb_s, kib_s, qb_s, qib_s, wib_s, sc_s, sct_s, m_s, acc_s,
                       *, length, tq, kc, topk):
    nkc = kib_s.shape[0]
    nl = kc // LANES
    tqp = sct_s.shape[2]
    tqb = qb_s.shape[1] // A_GROUP
    it = pl.program_id(1)
    r0 = it * tq
    nk = (r0 + tq + kc - 1) // kc
    kf = float(topk)

    def pad_rows(x, n):
        return x if x.shape[0] == n else jnp.concatenate([x, jnp.zeros((n - x.shape[0], x.shape[1]), x.dtype)], axis=0)

    @pl.when(it == 0)
    def _():
        ones_col = (lax.broadcasted_iota(I32, (kc, LANES - A_HEAD_DIM), 1) == 0).astype(BF16)
        for c in range(nkc):
            lo = c * kc
            valid = min(kc, length - lo)
            k_t = pad_rows(k_ref[0, lo:lo + valid, :], kc).T
            ki_t = pad_rows(kiwi_ref[0, lo:lo + valid, :], kc).T
            kib_s[c] = ki_t[0:IDX_DIM].astype(BF16)
            for kh in range(A_KV_HEADS):
                sl = slice(kh * A_HEAD_DIM, (kh + 1) * A_HEAD_DIM)
                kb_s[kh, c] = k_t[sl].astype(BF16)
                vb_s[kh, c, :, 0:A_HEAD_DIM] = pad_rows(v_ref[0, lo:lo + valid, sl], kc).astype(BF16)
                vb_s[kh, c, :, A_HEAD_DIM:LANES] = ones_col

    kiwi_t = pad_rows(kiwi_ref[0, pl.ds(pl.multiple_of(r0, SUBLANES), tq), :], tqp)
    for h in range(IDX_HEADS):
        qih = pad_rows(qi_ref[0, :, h * IDX_DIM:(h + 1) * IDX_DIM], tqp)
        qib_s[h * tqp:(h + 1) * tqp, :] = qih.astype(BF16)
        wib_s[h] = jnp.broadcast_to(kiwi_t[:, IDX_DIM + h:IDX_DIM + h + 1], (tqp, LANES))
    for hd in range(A_HEADS):
        kh, g = divmod(hd, A_GROUP)
        qh = pad_rows(q_ref[0, :, hd * A_HEAD_DIM:(hd + 1) * A_HEAD_DIM] * QK_SCALE_LOG2, tqb)
        qb_s[kh, g * tqb:(g + 1) * tqb, :] = qh.astype(BF16)
        m_s[hd] = jnp.full((tq, LANES), MASK_BIAS, F32)
        acc_s[hd] = jnp.zeros((tq, LANES), F32)

    def score_chunk(c, carry):
        s_all = jnp.dot(qib_s[...], kib_s[c], preferred_element_type=F32)
        score = jnp.zeros((tqp, kc), F32)
        for h in range(IDX_HEADS):
            score = score + jnp.maximum(s_all[h * tqp:(h + 1) * tqp], 0.0) * _lane_tile(wib_s[h], nl)
        col = c * kc + lax.broadcasted_iota(I32, (tqp, kc), 1)
        pos = r0 + lax.broadcasted_iota(I32, (tqp, kc), 0)
        vis_score = jnp.where(col <= pos, score, -jnp.inf)
        sc_s[c] = vis_score
        sct_s[c] = vis_score.T
        return carry

    lax.fori_loop(0, nk, score_chunk, 0)

    def count(pred):
        def body(c, acc):
            hit = pred(sct_s[c], c).astype(F32)
            return acc + jnp.sum(hit.reshape(kc // SUBLANES, SUBLANES, tqp), axis=0)
        acc = lax.fori_loop(0, nk, body, jnp.zeros((SUBLANES, tqp), F32))
        return jnp.sum(acc, axis=0, keepdims=True)

    t0 = jnp.where(count(lambda s, c: s >= 0.0) >= kf, 0, INT_MIN).astype(I32)

    def bit_step(b, t):
        cand = t + lax.shift_left(jnp.int32(1), 30 - b)
        cand_f = _key_to_float(cand)
        return jnp.where(count(lambda s, c: s >= cand_f) >= kf, cand, t)

    thr_key = lax.fori_loop(0, 31, bit_step, t0)
    thr_q = _key_to_float(thr_key)
    tie_q = (count(lambda s, c: s >= thr_q) > kf) & (thr_key > KEY_NEG_INF)
    tie_q = tie_q & (lax.broadcasted_iota(I32, (1, tqp), 1) < tq)

    ncols = nkc * kc

    def break_ties():
        need = kf - count(lambda s, c: s > thr_q)
        kidx = lax.broadcasted_iota(I32, (kc, tqp), 0).astype(F32)
        ltri = (lax.broadcasted_iota(I32, (kc, kc), 0) >= lax.broadcasted_iota(I32, (kc, kc), 1)).astype(BF16)

        def body(c, carry):
            seen, jc = carry
            tied = sct_s[c] == thr_q
            rank = seen + jnp.dot(ltri, tied.astype(BF16), preferred_element_type=F32)
            take = tied & (rank <= need)
            last = jnp.max(jnp.where(take, kidx + (c * kc).astype(F32), -1.0), axis=0, keepdims=True)
            return rank[kc - 1:kc, :], jnp.maximum(jc, last)

        _, jc = lax.fori_loop(0, nk, body, (jnp.zeros((1, tqp), F32), jnp.full((1, tqp), -1.0, F32)))
        return jnp.where(tie_q, jc.astype(I32), ncols)

    jcut_q = lax.cond(jnp.max(tie_q.astype(F32)) > 0.0, break_ties,
                      lambda: jnp.full((1, tqp), ncols, I32))

    def per_row(x_q):
        return jnp.broadcast_to(x_q, (LANES, tqp)).T[0:tq]

    thr_t = _lane_tile(per_row(thr_q), nl)
    jcut_t = _lane_tile(per_row(jcut_q), nl)

    def attend_chunk(c, carry):
        sc = sc_s[c, 0:tq, :]
        col = c * kc + lax.broadcasted_iota(I32, (tq, kc), 1)
        pos = r0 + lax.broadcasted_iota(I32, (tq, kc), 0)
        sel = (col <= pos) & ((sc > thr_t) | ((sc == thr_t) & (col <= jcut_t)))
        bias = jnp.where(sel, 0.0, MASK_BIAS).astype(F32)
        for kh in range(A_KV_HEADS):
            l_all = jnp.dot(qb_s[kh], kb_s[kh, c], preferred_element_type=F32)
            es, stats = [], []
            for g in range(A_GROUP):
                hd = kh * A_GROUP + g
                logits = l_all[g * tqb:g * tqb + tq] + bias
                m_old = m_s[hd]
                m_new = jnp.maximum(m_old, jnp.broadcast_to(jnp.max(logits, axis=-1, keepdims=True), (tq, LANES)))
                es.append(pad_rows(jnp.exp2(logits - _lane_tile(m_new, nl)), tqb).astype(BF16))
                stats.append((hd, m_old, m_new))
            pv_all = jnp.dot(jnp.concatenate(es, axis=0), vb_s[kh, c], preferred_element_type=F32)
            for g, (hd, m_old, m_new) in enumerate(stats):
                acc_s[hd] = jnp.exp2(m_old - m_new) * acc_s[hd] + pv_all[g * tqb:g * tqb + tq]
                m_s[hd] = m_new
        return carry

    lax.fori_loop(0, nk, attend_chunk, 0)

    for hd in range(A_HEADS):
        acc = acc_s[hd]
        o_ref[0, :, hd * A_HEAD_DIM:(hd + 1) * A_HEAD_DIM] = (
            acc[:, 0:A_HEAD_DIM] / acc[:, A_HEAD_DIM:A_HEAD_DIM + 1])


def dsa_prompt(q, qi, k, v, kiwi, tq, kc):
    b, length, _ = q.shape
    assert length % tq == 0 and kc % LANES == 0
    topk = min(TOPK_MAX, length // 4)
    nkc = -(-length // kc)
    tqp = _round_up(tq, LANES)
    tqb = _round_up(tq, BF16_ROWS)
    seq = lambda c: pl.BlockSpec((1, length, c), lambda i, j: (i, 0, 0))
    til = lambda c: pl.BlockSpec((1, tq, c), lambda i, j: (i, j, 0))
    return pl.pallas_call(
        functools.partial(_dsa_prompt_kernel, length=length, tq=tq, kc=kc, topk=topk),
        grid=(b, length // tq),
        in_specs=[til(q.shape[2]), til(qi.shape[2]), seq(k.shape[2]), seq(v.shape[2]), seq(kiwi.shape[2])],
        out_specs=til(q.shape[2]),
        out_shape=jax.ShapeDtypeStruct(q.shape, F32),
        scratch_shapes=[pltpu.VMEM((A_KV_HEADS, nkc, A_HEAD_DIM, kc), BF16),
                        pltpu.VMEM((A_KV_HEADS, nkc, kc, LANES), BF16),
                        pltpu.VMEM((nkc, IDX_DIM, kc), BF16),
                        pltpu.VMEM((A_KV_HEADS, A_GROUP * tqb, A_HEAD_DIM), BF16),
                        pltpu.VMEM((IDX_HEADS * tqp, IDX_DIM), BF16),
                        pltpu.VMEM((IDX_HEADS, tqp, LANES), F32),
                        pltpu.VMEM((nkc, tqp, kc), F32),
                        pltpu.VMEM((nkc, kc, tqp), F32),
                        pltpu.VMEM((A_HEADS, tq, LANES), F32),
                        pltpu.VMEM((A_HEADS, tq, LANES), F32)],
        compiler_params=_cparams("parallel", "arbitrary"),
        name="dsa_prompt",
    )(q, qi, k, v, kiwi)


def _dsa_sample_score_kernel(pt_ref, qi_ref, wi_ref, kinew_ref, *rest, pp, t):
    pages = rest[:pp]
    sc_ref, kin_s = rest[pp:]
    j = pl.program_id(1)
    nsteps = pl.num_programs(1)
    cw = pp * PAGE_SIZE
    past = sc_ref.shape[2] - LANES
    qi = qi_ref[0].astype(BF16)
    wi = wi_ref[0]

    def scores(s):
        s = jnp.maximum(s, 0.0) * wi
        return jnp.sum(s.reshape(t, IDX_HEADS, s.shape[1]), axis=1)

    ki_t = jnp.concatenate([p[0, 0] for p in pages], axis=1).astype(BF16)
    sc_chunk = scores(jnp.dot(qi, ki_t, preferred_element_type=F32))
    for jj in range(past // cw):
        @pl.when(j == jj)
        def _(jj=jj):
            sc_ref[0, :, jj * cw:(jj + 1) * cw] = sc_chunk

    @pl.when(j == nsteps - 1)
    def _():
        kin_s[...] = jnp.zeros(kin_s.shape, F32)
        kin_s[0:t, :] = kinew_ref[0]
        snew = scores(_dot_nt(qi, kin_s[...].astype(BF16)))
        cn = lax.broadcasted_iota(I32, (t, LANES), 1)
        rn = lax.broadcasted_iota(I32, (t, LANES), 0)
        sc_ref[0, :, past:past + LANES] = jnp.where(cn <= rn, snew, -jnp.inf)


def _dsa_select_kernel(sc_ref, thr_ref, jcut_ref, *, topk):
    rows, ncols = sc_ref.shape
    kf = float(topk)

    def count(pred):
        acc = jnp.zeros((rows, LANES), F32)
        for c0 in range(0, ncols, LANES):
            acc = acc + pred(sc_ref[:, c0:c0 + LANES], c0).astype(F32)
        return _row_total(acc)

    t0 = jnp.where(count(lambda s, c0: s >= 0.0) >= kf, 0, INT_MIN).astype(I32)

    def bit_step(b, t):
        cand = t + lax.shift_left(jnp.int32(1), 30 - b)
        cand_f = _key_to_float(cand)
        return jnp.where(count(lambda s, c0: s >= cand_f) >= kf, cand, t)

    thr_key = lax.fori_loop(0, 31, bit_step, t0)
    thr = _key_to_float(thr_key)
    tie = (count(lambda s, c0: s >= thr) > kf) & (thr_key > KEY_NEG_INF)
    nbits = max(1, (ncols - 1).bit_length())

    def break_ties():
        need = kf - count(lambda s, c0: s > thr)
        lane = lax.broadcasted_iota(I32, (rows, LANES), 1)

        def idx_step(b, jc):
            cand = jc + lax.shift_left(jnp.int32(1), nbits - 1 - b)
            n_lt = count(lambda s, c0: (s == thr) & (c0 + lane < cand))
            return jnp.where(n_lt < need, cand, jc)

        jc = lax.fori_loop(0, nbits, idx_step, jnp.zeros((rows, LANES), I32))
        return jnp.where(tie, jc, ncols)

    thr_ref[...] = thr
    jcut_ref[...] = lax.cond(jnp.max(tie.astype(F32)) > 0.0, break_ties,
                             lambda: jnp.full((rows, LANES), ncols, I32))


def _dsa_sample_attn_kernel(pt_ref, q_ref, sc_ref, scnew_ref, thr_ref, jcut_ref, knew_ref, vnew_ref,
                            *rest, pp, t):
    kpages = rest[:pp]
    vpages = rest[pp:2 * pp]
    o_ref, m_s, l_s, acc_s, new_s = rest[2 * pp:]
    j = pl.program_id(1)
    nsteps = pl.num_programs(1)
    cw = pp * PAGE_SIZE

    def bias_of(sc, c0):
        n = sc.shape[1] // LANES
        thr, jcut = _lane_tile(thr_ref[0], n), _lane_tile(jcut_ref[0], n)
        col = c0 + lax.broadcasted_iota(I32, sc.shape, 1)
        sel = (sc > -jnp.inf) & ((sc > thr) | ((sc == thr) & (col <= jcut)))
        bias = jnp.where(sel, 0.0, MASK_BIAS).astype(F32)
        return jnp.concatenate([bias] * A_GROUP, axis=0)

    @pl.when(j == 0)
    def _():
        m_s[...] = jnp.full(m_s.shape, MASK_BIAS, F32)
        l_s[...] = jnp.zeros(l_s.shape, F32)
        acc_s[...] = jnp.zeros(acc_s.shape, F32)

    def update(kh, kb, vb, bias, key_major):
        qh = (q_ref[0, kh] * (A_HEAD_DIM ** -0.5)).astype(BF16)
        qk = _dot_nt(qh, kb) if key_major else jnp.dot(qh, kb, preferred_element_type=F32)
        logits = qk + bias
        m_old = m_s[kh]
        m_new = jnp.maximum(m_old, jnp.max(logits, axis=-1, keepdims=True))
        alpha = jnp.exp(m_old - m_new)
        e = jnp.exp(logits - m_new)
        p = e.astype(BF16)
        pv = jnp.dot(p, vb, preferred_element_type=F32) if key_major else _dot_nt(p, vb)
        l_s[kh] = alpha * l_s[kh] + jnp.sum(e, axis=-1, keepdims=True)
        acc_s[kh] = alpha * acc_s[kh] + pv
        m_s[kh] = m_new

    bias = bias_of(sc_ref[0], j * cw)
    for kh in range(A_KV_HEADS):
        k_t = jnp.concatenate([p[0, 0, kh] for p in kpages], axis=1).astype(BF16)
        v_t = jnp.concatenate([p[0, 0, kh] for p in vpages], axis=1).astype(BF16)
        update(kh, k_t, v_t, bias, False)

    @pl.when(j == nsteps - 1)
    def _():
        bias_new = bias_of(scnew_ref[0], nsteps * cw)
        for kh in range(A_KV_HEADS):
            sl = slice(kh * A_HEAD_DIM, (kh + 1) * A_HEAD_DIM)
            new_s[...] = jnp.zeros(new_s.shape, F32)
            new_s[0, 0:t, :] = knew_ref[0, :, sl]
            new_s[1, 0:t, :] = vnew_ref[0, :, sl]
            update(kh, new_s[0].astype(BF16), new_s[1].astype(BF16), bias_new, True)
            o_ref[0, kh] = acc_s[kh] / l_s[kh]


def dsa_sample(q_g, qi_t, wi_t, ki_new, k_new, v_new, cache_k, cache_v, cache_ki, page_flat, layer, pp):
    db, _, gt, hd = q_g.shape
    t = gt // A_GROUP
    npg = page_flat.shape[0] // db
    assert npg % pp == 0
    nsteps = npg // pp
    past = npg * PAGE_SIZE
    ncols = past + LANES
    cw = pp * PAGE_SIZE
    kvw = k_new.shape[2]

    def page_spec(shape, i):
        nz = (0,) * len(shape)
        return pl.BlockSpec((1, 1) + shape, lambda b, j, pt, i=i: (layer, pt[b * npg + j * pp + i]) + nz)

    ki_page = (IDX_DIM, PAGE_SIZE)
    kv_page = (A_KV_HEADS, hd, PAGE_SIZE)

    sc = pl.pallas_call(
        functools.partial(_dsa_sample_score_kernel, pp=pp, t=t),
        grid_spec=pltpu.PrefetchScalarGridSpec(
            num_scalar_prefetch=1,
            grid=(db, nsteps),
            in_specs=[pl.BlockSpec((1, t * IDX_HEADS, IDX_DIM), lambda b, j, pt: (b, 0, 0)),
                      pl.BlockSpec((1, t * IDX_HEADS, 1), lambda b, j, pt: (b, 0, 0)),
                      pl.BlockSpec((1, t, IDX_DIM), lambda b, j, pt: (b, 0, 0))]
                     + [page_spec(ki_page, i) for i in range(pp)],
            out_specs=pl.BlockSpec((1, t, ncols), lambda b, j, pt: (b, 0, 0)),
            scratch_shapes=[pltpu.VMEM((LANES, IDX_DIM), F32)]),
        out_shape=jax.ShapeDtypeStruct((db, t, ncols), F32),
        compiler_params=_cparams("parallel", "arbitrary"),
        name="dsa_sample_score",
    )(page_flat, qi_t, wi_t, ki_new, *([cache_ki] * pp))

    thr, jcut = pl.pallas_call(
        functools.partial(_dsa_select_kernel, topk=min(TOPK_MAX, (past + t) // 4)),
        out_shape=[jax.ShapeDtypeStruct((db * t, LANES), F32), jax.ShapeDtypeStruct((db * t, LANES), I32)],
        compiler_params=pltpu.CompilerParams(vmem_limit_bytes=VMEM_LIMIT_BYTES),
        name="dsa_sample_select",
    )(sc.reshape(db * t, ncols))
    thr, jcut = thr.reshape(db, t, LANES), jcut.reshape(db, t, LANES)

    out = pl.pallas_call(
        functools.partial(_dsa_sample_attn_kernel, pp=pp, t=t),
        grid_spec=pltpu.PrefetchScalarGridSpec(
            num_scalar_prefetch=1,
            grid=(db, nsteps),
            in_specs=[pl.BlockSpec((1, A_KV_HEADS, gt, hd), lambda b, j, pt: (b, 0, 0, 0)),
                      pl.BlockSpec((1, t, cw), lambda b, j, pt: (b, 0, j)),
                      pl.BlockSpec((1, t, LANES), lambda b, j, pt: (b, 0, past // LANES)),
                      pl.BlockSpec((1, t, LANES), lambda b, j, pt: (b, 0, 0)),
                      pl.BlockSpec((1, t, LANES), lambda b, j, pt: (b, 0, 0)),
                      pl.BlockSpec((1, t, kvw), lambda b, j, pt: (b, 0, 0)),
                      pl.BlockSpec((1, t, kvw), lambda b, j, pt: (b, 0, 0))]
                     + [page_spec(kv_page, i) for i in range(pp)]
                     + [page_spec(kv_page, i) for i in range(pp)],
            out_specs=pl.BlockSpec((1, A_KV_HEADS, gt, hd), lambda b, j, pt: (b, 0, 0, 0)),
            scratch_shapes=[pltpu.VMEM((A_KV_HEADS, gt, 1), F32),
                            pltpu.VMEM((A_KV_HEADS, gt, 1), F32),
                            pltpu.VMEM((A_KV_HEADS, gt, hd), F32),
                            pltpu.VMEM((2, LANES, hd), F32)]),
        out_shape=jax.ShapeDtypeStruct((db, A_KV_HEADS, gt, hd), F32),
        compiler_params=_cparams("parallel", "arbitrary"),
        name="dsa_sample_attn",
    )(page_flat, q_g, sc, sc, thr, jcut, k_new, v_new, *([cache_k] * pp), *([cache_v] * pp))
    return out


def _gla_kernel(*refs, tb, tv, chunk, sub, has_s0):
    if has_s0:
        (q_ref, k_ref, v_ref, g_ref, gd_ref, wgu_ref, bg_ref, gn_ref, s0_ref,
         on_ref, sout_ref, b_s, dec_s, st_s) = refs
    else:
        q_ref, k_ref, v_ref, g_ref, gd_ref, wgu_ref, bg_ref, gn_ref, on_ref, sout_ref, b_s, dec_s, st_s = refs
    jb = pl.program_id(1)
    nb = pl.num_programs(1)
    nfull, rem = tb // chunk, tb % chunk
    spans = [(ic * chunk, chunk) for ic in range(nfull)] + ([(nfull * chunk, rem)] if rem else [])

    @pl.when(jb == 0)
    def _():
        for h in range(GLA_HEADS):
            st_s[h] = s0_ref[0, 0, h] if has_s0 else jnp.zeros((GLA_DK, GLA_DV), F32)

    x = jnp.dot(gd_ref[0], wgu_ref[...], precision=HIGHEST, preferred_element_type=F32) + bg_ref[...]
    la = (jnp.minimum(x, 0.0) - jnp.log1p(jnp.exp(-jnp.abs(x)))) * (1.0 / GLA_GATE_NORM)
    if tv is not None:
        row = jb * tb + lax.broadcasted_iota(I32, (tb, 1), 0)
        la = jnp.where(row < tv, la, 0.0)
    for ic, (r0, c) in enumerate(spans):
        lac = la[r0:r0 + c]
        tri = (lax.broadcasted_iota(I32, (c, c), 0) >= lax.broadcasted_iota(I32, (c, c), 1)).astype(BF16)
        hi = lac.astype(BF16)
        r1 = lac - hi.astype(F32)
        mid = r1.astype(BF16)
        lo = (r1 - mid.astype(F32)).astype(BF16)
        bc = (jnp.dot(tri, hi, preferred_element_type=F32) + jnp.dot(tri, mid, preferred_element_type=F32)
              + jnp.dot(tri, lo, preferred_element_type=F32))
        b_s[r0:r0 + c, :] = bc
        dec_s[ic] = jnp.exp(jnp.broadcast_to(bc[c - 1:c, :], (LANES, bc.shape[1])).T)

    def do_chunk(ic, r0, c):
        rows = pl.ds(r0, c)
        b = b_s[rows, :]
        qc = q_ref[0, rows, :] * (GLA_DK ** -0.5)
        kc = k_ref[0, rows, :]
        if tv is not None:
            rowc = jb * tb + r0 + lax.broadcasted_iota(I32, (c, 1), 0)
            kc = jnp.where(rowc < tv, kc, 0.0)
        vc = v_ref[0, rows, :].astype(BF16)
        gc = g_ref[0, rows, :]
        dec = dec_s[ic]
        causal = lax.broadcasted_iota(I32, (c, LANES), 1) <= lax.broadcasted_iota(I32, (c, LANES), 0)
        heads, states = [], []
        for h in range(GLA_HEADS):
            sk = slice(h * GLA_DK, (h + 1) * GLA_DK)
            sv = slice(h * GLA_DV, (h + 1) * GLA_DV)
            bh, qh, kh, vh = b[:, sk], qc[:, sk], kc[:, sk], vc[:, sv]
            st = st_s[h]
            att = []
            for i in range(c // sub):
                lo, hi = i * sub, (i + 1) * sub
                ref_b = bh[lo - 1:lo, :] if i > 0 else jnp.zeros((1, GLA_DK), F32)
                qt = (qh[lo:hi] * jnp.exp(bh[lo:hi] - ref_b)).astype(BF16)
                kt = (kh[0:hi] * jnp.exp(ref_b - bh[0:hi])).astype(BF16)
                kt = jnp.concatenate([kt, jnp.zeros((LANES - hi, GLA_DK), BF16)], axis=0)
                att.append(_dot_nt(qt, kt))
            att = jnp.concatenate(att, axis=0) if len(att) > 1 else att[0]
            att = jnp.where(causal, att, 0.0).astype(BF16)
            lhs = jnp.concatenate([(qh * jnp.exp(bh)).astype(BF16), att], axis=1)
            vpad = vh if c == LANES else jnp.concatenate([vh, jnp.zeros((LANES - c, GLA_DV), BF16)], axis=0)
            rhs = jnp.concatenate([st.astype(BF16), vpad], axis=0)
            o = jnp.dot(lhs, rhs, preferred_element_type=F32)
            bl = bh[c - 1:c, :]
            kd = (kh * jnp.exp(bl - bh)).astype(BF16)
            dh = dec[sk, :]
            states.append(jnp.concatenate([dh, dh], axis=1) * st + _dot_tn(kd, vh))
            gh = gc[:, sv]
            heads.append(_rms(o, gn_ref[...]) * (gh * jax.nn.sigmoid(gh)))
        on_ref[0, rows, :] = jnp.concatenate(heads, axis=1).astype(on_ref.dtype)
        for h in range(GLA_HEADS):
            st_s[h] = states[h]

    if nfull == 1:
        do_chunk(0, 0, chunk)
    elif nfull > 1:
        def body(ic, carry):
            do_chunk(ic, pl.multiple_of(ic * chunk, BF16_ROWS), chunk)
            return carry
        lax.fori_loop(0, nfull, body, 0)
    if rem:
        do_chunk(nfull, nfull * chunk, rem)

    @pl.when(jb == nb - 1)
    def _():
        for h in range(GLA_HEADS):
            sout_ref[0, h] = st_s[h]


def gla_mixer(q, k, v, g, gd, wgu, bg, gn, s0, s0_layer, tb, chunk, tv):
    b, t, _ = q.shape
    sub = BF16_ROWS
    assert t % tb == 0 and tb % sub == 0 and chunk % sub == 0 and chunk <= LANES and GLA_DV == 2 * LANES
    nspans = -(-tb // chunk)
    blk = lambda c: pl.BlockSpec((1, tb, c), lambda i, j: (i, j, 0))
    cst = lambda a: pl.BlockSpec(a.shape, lambda i, j: (0,) * a.ndim)
    st_spec = pl.BlockSpec((1, GLA_HEADS, GLA_DK, GLA_DV), lambda i, j: (i, 0, 0, 0))
    bg2, gn2 = bg.reshape(1, -1), gn.reshape(1, -1)
    args = [q, k, v, g, gd, wgu, bg2, gn2]
    in_specs = [blk(q.shape[2]), blk(k.shape[2]), blk(v.shape[2]), blk(g.shape[2]), blk(gd.shape[2]),
                cst(wgu), cst(bg2), cst(gn2)]
    if s0 is not None:
        args.append(s0)
        in_specs.append(pl.BlockSpec((1, 1, GLA_HEADS, GLA_DK, GLA_DV), lambda i, j: (s0_layer, i, 0, 0, 0)))
    return pl.pallas_call(
        functools.partial(_gla_kernel, tb=tb, tv=tv, chunk=chunk, sub=sub, has_s0=s0 is not None),
        grid=(b, t // tb),
        in_specs=in_specs,
        out_specs=[blk(v.shape[2]), st_spec],
        out_shape=[jax.ShapeDtypeStruct(v.shape, BF16),
                   jax.ShapeDtypeStruct((b, GLA_HEADS, GLA_DK, GLA_DV), F32)],
        scratch_shapes=[pltpu.VMEM((tb, GLA_HEADS * GLA_DK), F32),
                        pltpu.VMEM((nspans, GLA_HEADS * GLA_DK, LANES), F32),
                        pltpu.VMEM((GLA_HEADS, GLA_DK, GLA_DV), F32)],
        compiler_params=_cparams("parallel", "arbitrary"),
        name="gla_mixer",
    )(*args)


def _pick_tile(n, candidates):
    for c in candidates:
        if n % c == 0:
            return c
    return n


def kernel(x_prompt, x_sample, cache_k, cache_v, cache_kidx, state_pool, state_gla, page_table,
           meta_tokens, norm_mix, norm_ffn, norm_final, w_in_even, w_out_even, pool_w, pool_scale,
           w_in_odd, gla_w_gate_up, gla_b_gate, gla_norm, w_out_odd, ffn_w_gate, ffn_w_up, ffn_w_down):
    bsz, seq, d = x_prompt.shape
    db, dseq, _ = x_sample.shape
    depth = norm_mix.shape[0]
    length = seq + N_META
    npg = page_table.shape[1]
    past = npg * PAGE_SIZE
    a_width = A_HEADS * A_HEAD_DIM
    kv_width = A_KV_HEADS * A_HEAD_DIM
    pool_width = pool_scale.shape[1]
    gk = GLA_HEADS * GLA_DK
    gv = GLA_HEADS * GLA_DV
    d_ff = ffn_w_gate.shape[2]

    meta = jnp.broadcast_to(meta_tokens[None], (bsz, N_META, d))
    xp = jnp.concatenate([meta, x_prompt], axis=1).reshape(bsz * length, d)
    xs = x_sample.reshape(db * dseq, d)
    n_p, n_s = xp.shape[0], xs.shape[0]

    tm_p = _pick_tile(n_p, (688, 512, 256, 128, 64, 32, 16, 8))
    tm_s = _pick_tile(n_s, (128, 64, 32, 16, 8))
    tf = _pick_tile(d_ff, (1408, 1024, 512, 256, 128))
    tq = _pick_tile(length, (344, 256, 128, 64, 32, 16, 8))
    pool_rc = _pick_tile(length, (48, 32, 16, 8))
    gla_tb = _pick_tile(length, (688, 512, 256, 128, 64, 32, 16))
    pp = _pick_tile(npg, (64, 32, 16, 8, 4, 2, 1))

    e_off = [0]
    for s in (a_width, kv_width, kv_width, IDX_HEADS * IDX_DIM, IDX_DIM, IDX_HEADS, pool_width):
        e_off.append(e_off[-1] + s)
    kiwi_pad = LANES - IDX_DIM - IDX_HEADS
    even_widths = (a_width, kv_width, kv_width, IDX_HEADS * IDX_DIM, pool_width, LANES)
    odd_widths = (gk, gk, gv, gv, LANES)

    cache_kt = cache_k.transpose(0, 1, 3, 4, 2)
    cache_vt = cache_v.transpose(0, 1, 3, 4, 2)
    cache_kit = cache_kidx.transpose(0, 1, 3, 2)
    page_flat = page_table.reshape(-1).astype(I32)

    kp_l, vp_l, kip_l, bp_l, ks_l, vs_l, kis_l, bs_l, sp_l, ss_l = ([] for _ in range(10))
    for l in range(depth):
        if l % 2 == 0:
            e = l // 2
            w = w_in_even[e]
            w_in = jnp.concatenate(
                [w[:, e_off[0]:e_off[4]], w[:, e_off[6]:e_off[7]], w[:, e_off[4]:e_off[6]],
                 jnp.zeros((d, kiwi_pad), w.dtype)], axis=1).astype(BF16)
            w_out = w_out_even[e].astype(BF16)
            pw = pool_w[e].astype(BF16)

            q, k, v, qi, u, kiwi = norm_matmul(xp, norm_mix[l], w_in, even_widths, tm_p)
            r3 = lambda a: a.reshape(bsz, length, a.shape[1])
            oa = dsa_prompt(r3(q), r3(qi), r3(k), r3(v), r3(kiwi), tq, DSA_KEY_CHUNK)
            ob, buf = pool_mixer(r3(u), jnp.zeros((bsz, POOL_BUF + 1, pool_width), F32), pw,
                                 pool_scale[e], 0, pool_rc, BF16)
            mix_p = [(oa.reshape(n_p, a_width), w_out[:a_width]),
                     (ob.reshape(n_p, pool_width), w_out[a_width:])]
            kp_l.append(k.reshape(bsz, length, A_KV_HEADS, A_HEAD_DIM))
            vp_l.append(v.reshape(bsz, length, A_KV_HEADS, A_HEAD_DIM))
            kip_l.append(kiwi[:, :IDX_DIM].reshape(bsz, length, IDX_DIM))
            bp_l.append(buf)

            q, k, v, qi, u, kiwi = norm_matmul(xs, norm_mix[l], w_in, even_widths, tm_s)
            s3 = lambda a: a.reshape(db, dseq, a.shape[1])
            q_g = q.reshape(db, dseq, A_KV_HEADS, A_GROUP, A_HEAD_DIM).transpose(0, 2, 3, 1, 4)
            q_g = q_g.reshape(db, A_KV_HEADS, A_GROUP * dseq, A_HEAD_DIM)
            qi_t = qi.reshape(db, dseq * IDX_HEADS, IDX_DIM)
            wi_t = kiwi[:, IDX_DIM:IDX_DIM + IDX_HEADS].reshape(db, dseq * IDX_HEADS, 1)
            ki_new = kiwi[:, :IDX_DIM].reshape(db, dseq, IDX_DIM)
            oa = dsa_sample(q_g, qi_t, wi_t, ki_new, s3(k), s3(v), cache_kt, cache_vt, cache_kit,
                            page_flat, e, pp)
            oa = oa.reshape(db, A_KV_HEADS, A_GROUP, dseq, A_HEAD_DIM).transpose(0, 3, 1, 2, 4)
            prefix = jnp.pad(state_pool[e], ((0, 0), (1, 0), (0, 0)))
            ob, buf = pool_mixer(s3(u), prefix, pw, pool_scale[e], past, BF16_ROWS, F32)
            mix_s = [(oa.reshape(n_s, a_width), w_out[:a_width]),
                     (ob.reshape(n_s, pool_width), w_out[a_width:])]
            ks_l.append(k.reshape(db, dseq, A_KV_HEADS, A_HEAD_DIM))
            vs_l.append(v.reshape(db, dseq, A_KV_HEADS, A_HEAD_DIM))
            kis_l.append(ki_new)
            bs_l.append(buf)
        else:
            o = l // 2
            w = w_in_odd[o]
            w_in = jnp.concatenate([w, jnp.zeros((d, LANES - GLA_GATE_RANK), w.dtype)], axis=1).astype(BF16)
            w_out = w_out_odd[o].astype(BF16)
            wgu = jnp.pad(gla_w_gate_up[o], ((0, LANES - GLA_GATE_RANK), (0, 0)))

            q, k, v, g, gd = norm_matmul(xp, norm_mix[l], w_in, odd_widths, tm_p)
            r3 = lambda a: a.reshape(bsz, length, a.shape[1])
            on, s_p = gla_mixer(r3(q), r3(k), r3(v), r3(g), r3(gd), wgu, gla_b_gate[o], gla_norm[o],
                                None, None, gla_tb, GLA_CHUNK, None)
            mix_p = [(on.reshape(n_p, gv), w_out)]
            sp_l.append(s_p)

            q, k, v, g, gd = norm_matmul(xs, norm_mix[l], w_in, odd_widths, tm_s)
            tpad = _round_up(dseq, BF16_ROWS)
            p3 = lambda a: jnp.pad(a.reshape(db, dseq, a.shape[1]), ((0, 0), (0, tpad - dseq), (0, 0)))
            on, s_s = gla_mixer(p3(q), p3(k), p3(v), p3(g), p3(gd), wgu, gla_b_gate[o], gla_norm[o],
                                state_gla, o, tpad, tpad, dseq)
            mix_s = [(on[:, :dseq].reshape(n_s, gv), w_out)]
            ss_l.append(s_s)

        wg, wu, wd = ffn_w_gate[l].astype(BF16), ffn_w_up[l].astype(BF16), ffn_w_down[l].astype(BF16)
        xp = mix_ffn(xp, mix_p, norm_ffn[l], wg, wu, wd, tm_p, tf)
        xs = mix_ffn(xs, mix_s, norm_ffn[l], wg, wu, wd, tm_s, tf)

    y_prompt = final_norm(xp.reshape(bsz, length, d), norm_final, N_META)
    y_sample = final_norm(xs.reshape(1, n_s, d), norm_final, 0).reshape(db, dseq, d)
    return (y_prompt, y_sample,
            jnp.stack(kp_l), jnp.stack(vp_l), jnp.stack(kip_l),
            jnp.stack(ks_l), jnp.stack(vs_l), jnp.stack(kis_l),
            jnp.stack(bp_l), jnp.stack(bs_l),
            jnp.stack(sp_l), jnp.stack(ss_l))
```

```python
import functools

import jax
import jax.numpy as jnp
from jax import lax
from jax.experimental import pallas as pl
from jax.experimental.pallas import tpu as pltpu

F32 = jnp.float32
BF16 = jnp.bfloat16
I32 = jnp.int32

NORM_EPS = 1e-6
N_META = 16
A_HEADS = 8
A_KV_HEADS = 2
A_GROUP = A_HEADS // A_KV_HEADS
A_HEAD_DIM = 64
IDX_HEADS = 8
IDX_DIM = 64
TOPK_MAX = 256
PAGE_SIZE = 128
POOL_WINDOWS = (2, 4, 8, 16)
POOL_GROUP_DIM = 128
POOL_BUF = 15
GLA_HEADS = 4
GLA_DK = 128
GLA_DV = 256
GLA_GATE_RANK = 16
GLA_GATE_NORM = 16.0

LANES = 128
SUBLANES = 8
BF16_ROWS = 16
VMEM_LIMIT_BYTES = 56 * 1024 * 1024

DSA_KEY_CHUNK = 2 * LANES
GLA_CHUNK = 96

INT_MIN = -(2 ** 31)
MASK_BIAS = -1e30
HIGHEST = lax.Precision.HIGHEST
QK_SCALE_LOG2 = (A_HEAD_DIM ** -0.5) * 1.4426950408889634


def _round_up(n, m):
    return (n + m - 1) // m * m


def _cparams(*sem):
    return pltpu.CompilerParams(dimension_semantics=sem, vmem_limit_bytes=VMEM_LIMIT_BYTES)


def _rms(x, g):
    ms = jnp.mean(x * x, axis=-1, keepdims=True)
    return (x * lax.rsqrt(ms + NORM_EPS)) * g


def _dot_nt(a, b):
    return lax.dot_general(a, b, (((1,), (1,)), ((), ())), preferred_element_type=F32)


def _dot_tn(a, b):
    return lax.dot_general(a, b, (((0,), (0,)), ((), ())), preferred_element_type=F32)


def _norm_matmul_kernel(x_ref, g_ref, w_ref, *o_refs, widths):
    h = _rms(x_ref[...], g_ref[...]).astype(BF16)
    off = 0
    for o_ref, wd in zip(o_refs, widths):
        o_ref[...] = jnp.dot(h, w_ref[:, off:off + wd], preferred_element_type=F32).astype(o_ref.dtype)
        off += wd


def norm_matmul(x, g, w, widths, tm, dtypes=None):
    n, d = x.shape
    assert n % tm == 0 and sum(widths) == w.shape[1]
    dtypes = dtypes or (F32,) * len(widths)
    return pl.pallas_call(
        functools.partial(_norm_matmul_kernel, widths=widths),
        grid=(n // tm,),
        in_specs=[pl.BlockSpec((tm, d), lambda i: (i, 0)),
                  pl.BlockSpec((1, d), lambda i: (0, 0)),
                  pl.BlockSpec((d, w.shape[1]), lambda i: (0, 0))],
        out_specs=[pl.BlockSpec((tm, wd), lambda i: (i, 0)) for wd in widths],
        out_shape=[jax.ShapeDtypeStruct((n, wd), dt) for wd, dt in zip(widths, dtypes)],
        compiler_params=_cparams("parallel"),
        name="norm_matmul",
    )(x, g.reshape(1, d), w)


def _mix_ffn_kernel(*refs, n_mix):
    x_ref = refs[0]
    mix = refs[1:1 + 2 * n_mix]
    g_ref, wg_ref, wu_ref, wd_ref, o_ref, h_s = refs[1 + 2 * n_mix:]
    j = pl.program_id(1)

    @pl.when(j == 0)
    def _():
        xm = x_ref[...]
        for m in range(n_mix):
            a_ref, w_ref = mix[2 * m], mix[2 * m + 1]
            xm = xm + jnp.dot(a_ref[...].astype(BF16), w_ref[...], preferred_element_type=F32)
        o_ref[...] = xm
        h_s[...] = _rms(xm, g_ref[...]).astype(BF16)

    h = h_s[...]
    gate = jnp.dot(h, wg_ref[...], preferred_element_type=F32)
    up = jnp.dot(h, wu_ref[...], preferred_element_type=F32)
    act = (gate * jax.nn.sigmoid(gate) * up).astype(BF16)
    o_ref[...] += jnp.dot(act, wd_ref[...], preferred_element_type=F32)


def mix_ffn(x, mixes, g, wg, wu, wd, tm, tf):
    n, d = x.shape
    f = wg.shape[1]
    assert n % tm == 0 and f % tf == 0
    in_specs = [pl.BlockSpec((tm, d), lambda i, j: (i, 0))]
    args = [x]
    for a, w in mixes:
        in_specs += [pl.BlockSpec((tm, a.shape[1]), lambda i, j: (i, 0)),
                     pl.BlockSpec(w.shape, lambda i, j: (0, 0))]
        args += [a, w]
    in_specs += [pl.BlockSpec((1, d), lambda i, j: (0, 0)),
                 pl.BlockSpec((d, tf), lambda i, j: (0, j)),
                 pl.BlockSpec((d, tf), lambda i, j: (0, j)),
                 pl.BlockSpec((tf, d), lambda i, j: (j, 0))]
    args += [g.reshape(1, d), wg, wu, wd]
    return pl.pallas_call(
        functools.partial(_mix_ffn_kernel, n_mix=len(mixes)),
        grid=(n // tm, f // tf),
        in_specs=in_specs,
        out_specs=pl.BlockSpec((tm, d), lambda i, j: (i, 0)),
        out_shape=jax.ShapeDtypeStruct((n, d), F32),
        scratch_shapes=[pltpu.VMEM((tm, d), BF16)],
        compiler_params=_cparams("parallel", "arbitrary"),
        name="mix_ffn",
    )(*args)


def _final_norm_kernel(x_ref, g_ref, o_ref, *, skip, rows, chunk):
    for c in range(rows // chunk):
        x = x_ref[0, skip + c * chunk: skip + (c + 1) * chunk, :]
        o_ref[0, c * chunk:(c + 1) * chunk, :] = _rms(x, g_ref[...])


def final_norm(x, g, skip):
    b, t, d = x.shape
    rows = t - skip
    chunk = 256 if rows % 256 == 0 else rows
    return pl.pallas_call(
        functools.partial(_final_norm_kernel, skip=skip, rows=rows, chunk=chunk),
        grid=(b,),
        in_specs=[pl.BlockSpec((1, t, d), lambda i: (i, 0, 0)),
                  pl.BlockSpec((1, d), lambda i: (0, 0))],
        out_specs=pl.BlockSpec((1, rows, d), lambda i: (i, 0, 0)),
        out_shape=jax.ShapeDtypeStruct((b, rows, d), F32),
        compiler_params=_cparams("parallel"),
        name="final_norm",
    )(x, g.reshape(1, d))


def _pool_kernel(u_ref, pre_ref, pw_ref, sc_ref, ob_ref, buf_ref, ext_s, *, t, pos0, rc):
    head = POOL_BUF + 1
    tproc = _round_up(t, rc)
    ext_s[0:head, :] = pre_ref[0]
    ext_s[head:head + t, :] = u_ref[0]
    if tproc > t:
        ext_s[head + t:head + tproc, :] = jnp.zeros((tproc - t, ext_s.shape[1]), F32)
    buf_ref[0] = ext_s[t + 1:t + head, :]
    for c in range(tproc // rc):
        r0 = c * rc
        nst = min(rc, t - r0)
        pos = pos0 + r0 + lax.broadcasted_iota(I32, (rc, 1), 0)
        for gi, w in enumerate(POOL_WINDOWS):
            sl = slice(gi * POOL_GROUP_DIM, (gi + 1) * POOL_GROUP_DIM)
            xt = ext_s[head + r0:head + r0 + rc, sl]
            acc = xt
            for r in range(1, w):
                acc = acc + ext_s[head + r0 - r:head + r0 - r + rc, sl]
            cnt = jnp.minimum(w, pos + 1).astype(F32)
            dlt = acc / cnt - xt
            y = jnp.dot(dlt.astype(BF16), pw_ref[gi], preferred_element_type=F32) * sc_ref[:, sl]
            ob_ref[0, r0:r0 + nst, sl] = y[0:nst].astype(ob_ref.dtype)


def pool_mixer(u, prefix16, pw, scale, pos0, rc, out_dtype):
    b, t, c = u.shape
    head = POOL_BUF + 1
    return pl.pallas_call(
        functools.partial(_pool_kernel, t=t, pos0=pos0, rc=rc),
        grid=(b,),
        in_specs=[pl.BlockSpec((1, t, c), lambda i: (i, 0, 0)),
                  pl.BlockSpec((1, head, c), lambda i: (i, 0, 0)),
                  pl.BlockSpec(pw.shape, lambda i: (0, 0, 0)),
                  pl.BlockSpec((1, c), lambda i: (0, 0))],
        out_specs=[pl.BlockSpec((1, t, c), lambda i: (i, 0, 0)),
                   pl.BlockSpec((1, POOL_BUF, c), lambda i: (i, 0, 0))],
        out_shape=[jax.ShapeDtypeStruct((b, t, c), out_dtype),
                   jax.ShapeDtypeStruct((b, POOL_BUF, c), F32)],
        scratch_shapes=[pltpu.VMEM((head + _round_up(t, rc), c), F32)],
        compiler_params=_cparams("parallel"),
        name="pool_mixer",
    )(u, prefix16, pw, scale.reshape(1, c))


KEY_NEG_INF = INT_MIN + 0x7FFFFF


def _key_to_float(key):
    bits = key ^ ((key >> 31) & 0x7FFFFFFF)
    return jnp.where(key <= KEY_NEG_INF, -jnp.inf, lax.bitcast_convert_type(bits, F32))


def _lane_tile(x, n):
    return x if n == 1 else jnp.concatenate([x] * n, axis=1)


def _row_total(x):
    return jnp.broadcast_to(jnp.sum(x, axis=-1, keepdims=True), x.shape)


def _dsa_prompt_kernel(q_ref, qi_ref, k_ref, v_ref, kiwi_ref, o_ref,
                       kb_s, vb_s, kib_s, qb_s, qib_s, wib_s, sc_s, sct_s, m_s, acc_s,
                       *, length, tq, kc, topk):
    nkc = kib_s.shape[0]
    nl = kc // LANES
    tqp = sct_s.shape[2]
    tqb = qb_s.shape[1] // A_GROUP
    it = pl.program_id(1)
    r0 = it * tq
    nk = (r0 + tq + kc - 1) // kc
    kf = float(topk)

    def pad_rows(x, n):
        return x if x.shape[0] == n else jnp.concatenate([x, jnp.zeros((n - x.shape[0], x.shape[1]), x.dtype)], axis=0)

    @pl.when(it == 0)
    def _():
        ones_col = (lax.broadcasted_iota(I32, (kc, LANES - A_HEAD_DIM), 1) == 0).astype(BF16)
        for c in range(nkc):
            lo = c * kc
            valid = min(kc, length - lo)
            k_t = pad_rows(k_ref[0, lo:lo + valid, :], kc).T
            ki_t = pad_rows(kiwi_ref[0, lo:lo + valid, :], kc).T
            kib_s[c] = ki_t[0:IDX_DIM].astype(BF16)
            for kh in range(A_KV_HEADS):
                sl = slice(kh * A_HEAD_DIM, (kh + 1) * A_HEAD_DIM)
                kb_s[kh, c] = k_t[sl].astype(BF16)
                vb_s[kh, c, :, 0:A_HEAD_DIM] = pad_rows(v_ref[0, lo:lo + valid, sl], kc).astype(BF16)
                vb_s[kh, c, :, A_HEAD_DIM:LANES] = ones_col

    kiwi_t = pad_rows(kiwi_ref[0, pl.ds(pl.multiple_of(r0, SUBLANES), tq), :], tqp)
    for h in range(IDX_HEADS):
        qih = pad_rows(qi_ref[0, :, h * IDX_DIM:(h + 1) * IDX_DIM], tqp)
        qib_s[h * tqp:(h + 1) * tqp, :] = qih.astype(BF16)
        wib_s[h] = jnp.broadcast_to(kiwi_t[:, IDX_DIM + h:IDX_DIM + h + 1], (tqp, LANES))
    for hd in range(A_HEADS):
        kh, g = divmod(hd, A_GROUP)
        qh = pad_rows(q_ref[0, :, hd * A_HEAD_DIM:(hd + 1) * A_HEAD_DIM] * QK_SCALE_LOG2, tqb)
        qb_s[kh, g * tqb:(g + 1) * tqb, :] = qh.astype(BF16)
        m_s[hd] = jnp.full((tq, LANES), MASK_BIAS, F32)
        acc_s[hd] = jnp.zeros((tq, LANES), F32)

    def score_chunk(c, carry):
        s_all = jnp.dot(qib_s[...], kib_s[c], preferred_element_type=F32)
        score = jnp.zeros((tqp, kc), F32)
        for h in range(IDX_HEADS):
            score = score + jnp.maximum(s_all[h * tqp:(h + 1) * tqp], 0.0) * _lane_tile(wib_s[h], nl)
        col = c * kc + lax.broadcasted_iota(I32, (tqp, kc), 1)
        pos = r0 + lax.broadcasted_iota(I32, (tqp, kc), 0)
        vis_score = jnp.where(col <= pos, score, -jnp.inf)
        sc_s[c] = vis_score
        sct_s[c] = vis_score.T
        return carry

    lax.fori_loop(0, nk, score_chunk, 0)

    def count(pred):
        def body(c, acc):
            hit = pred(sct_s[c], c).astype(F32)
            return acc + jnp.sum(hit.reshape(kc // SUBLANES, SUBLANES, tqp), axis=0)
        acc = lax.fori_loop(0, nk, body, jnp.zeros((SUBLANES, tqp), F32))
        return jnp.sum(acc, axis=0, keepdims=True)

    t0 = jnp.where(count(lambda s, c: s >= 0.0) >= kf, 0, INT_MIN).astype(I32)

    def bit_step(b, t):
        cand = t + lax.shift_left(jnp.int32(1), 30 - b)
        cand_f = _key_to_float(cand)
        return jnp.where(count(lambda s, c: s >= cand_f) >= kf, cand, t)

    thr_key = lax.fori_loop(0, 31, bit_step, t0)
    thr_q = _key_to_float(thr_key)
    tie_q = (count(lambda s, c: s >= thr_q) > kf) & (thr_key > KEY_NEG_INF)
    tie_q = tie_q & (lax.broadcasted_iota(I32, (1, tqp), 1) < tq)

    ncols = nkc * kc

    def break_ties():
        need = kf - count(lambda s, c: s > thr_q)
        kidx = lax.broadcasted_iota(I32, (kc, tqp), 0).astype(F32)
        ltri = (lax.broadcasted_iota(I32, (kc, kc), 0) >= lax.broadcasted_iota(I32, (kc, kc), 1)).astype(BF16)

        def body(c, carry):
            seen, jc = carry
            tied = sct_s[c] == thr_q
            rank = seen + jnp.dot(ltri, tied.astype(BF16), preferred_element_type=F32)
            take = tied & (rank <= need)
            last = jnp.max(jnp.where(take, kidx + (c * kc).astype(F32), -1.0), axis=0, keepdims=True)
            return rank[kc - 1:kc, :], jnp.maximum(jc, last)

        _, jc = lax.fori_loop(0, nk, body, (jnp.zeros((1, tqp), F32), jnp.full((1, tqp), -1.0, F32)))
        return jnp.where(tie_q, jc.astype(I32), ncols)

    jcut_q = lax.cond(jnp.max(tie_q.astype(F32)) > 0.0, break_ties,
                      lambda: jnp.full((1, tqp), ncols, I32))

    def per_row(x_q):
        return jnp.broadcast_to(x_q, (LANES, tqp)).T[0:tq]

    thr_t = _lane_tile(per_row(thr_q), nl)
    jcut_t = _lane_tile(per_row(jcut_q), nl)

    def attend_chunk(c, carry):
        sc = sc_s[c, 0:tq, :]
        col = c * kc + lax.broadcasted_iota(I32, (tq, kc), 1)
        pos = r0 + lax.broadcasted_iota(I32, (tq, kc), 0)
        sel = (col <= pos) & ((sc > thr_t) | ((sc == thr_t) & (col <= jcut_t)))
        bias = jnp.where(sel, 0.0, MASK_BIAS).astype(F32)
        for kh in range(A_KV_HEADS):
            l_all = jnp.dot(qb_s[kh], kb_s[kh, c], preferred_element_type=F32)
            es, stats = [], []
            for g in range(A_GROUP):
                hd = kh * A_GROUP + g
                logits = l_all[g * tqb:g * tqb + tq] + bias
                m_old = m_s[hd]
                m_new = jnp.maximum(m_old, jnp.broadcast_to(jnp.max(logits, axis=-1, keepdims=True), (tq, LANES)))
                es.append(pad_rows(jnp.exp2(logits - _lane_tile(m_new, nl)), tqb).astype(BF16))
                stats.append((hd, m_old, m_new))
            pv_all = jnp.dot(jnp.concatenate(es, axis=0), vb_s[kh, c], preferred_element_type=F32)
            for g, (hd, m_old, m_new) in enumerate(stats):
                acc_s[hd] = jnp.exp2(m_old - m_new) * acc_s[hd] + pv_all[g * tqb:g * tqb + tq]
                m_s[hd] = m_new
        return carry

    lax.fori_loop(0, nk, attend_chunk, 0)

    for hd in range(A_HEADS):
        acc = acc_s[hd]
        o_ref[0, :, hd * A_HEAD_DIM:(hd + 1) * A_HEAD_DIM] = (
            acc[:, 0:A_HEAD_DIM] / acc[:, A_HEAD_DIM:A_HEAD_DIM + 1])


def dsa_prompt(q, qi, k, v, kiwi, tq, kc):
    b, length, _ = q.shape
    assert length % tq == 0 and kc % LANES == 0
    topk = min(TOPK_MAX, length // 4)
    nkc = -(-length // kc)
    tqp = _round_up(tq, LANES)
    tqb = _round_up(tq, BF16_ROWS)
    seq = lambda c: pl.BlockSpec((1, length, c), lambda i, j: (i, 0, 0))
    til = lambda c: pl.BlockSpec((1, tq, c), lambda i, j: (i, j, 0))
    return pl.pallas_call(
        functools.partial(_dsa_prompt_kernel, length=length, tq=tq, kc=kc, topk=topk),
        grid=(b, length // tq),
        in_specs=[til(q.shape[2]), til(qi.shape[2]), seq(k.shape[2]), seq(v.shape[2]), seq(kiwi.shape[2])],
        out_specs=til(q.shape[2]),
        out_shape=jax.ShapeDtypeStruct(q.shape, F32),
        scratch_shapes=[pltpu.VMEM((A_KV_HEADS, nkc, A_HEAD_DIM, kc), BF16),
                        pltpu.VMEM((A_KV_HEADS, nkc, kc, LANES), BF16),
                        pltpu.VMEM((nkc, IDX_DIM, kc), BF16),
                        pltpu.VMEM((A_KV_HEADS, A_GROUP * tqb, A_HEAD_DIM), BF16),
                        pltpu.VMEM((IDX_HEADS * tqp, IDX_DIM), BF16),
                        pltpu.VMEM((IDX_HEADS, tqp, LANES), F32),
                        pltpu.VMEM((nkc, tqp, kc), F32),
                        pltpu.VMEM((nkc, kc, tqp), F32),
                        pltpu.VMEM((A_HEADS, tq, LANES), F32),
                        pltpu.VMEM((A_HEADS, tq, LANES), F32)],
        compiler_params=_cparams("parallel", "arbitrary"),
        name="dsa_prompt",
    )(q, qi, k, v, kiwi)


def _dsa_sample_score_kernel(pt_ref, qi_ref, wi_ref, kinew_ref, *rest, pp, t):
    pages = rest[:pp]
    sc_ref, kin_s = rest[pp:]
    j = pl.program_id(1)
    nsteps = pl.num_programs(1)
    cw = pp * PAGE_SIZE
    past = sc_ref.shape[2] - LANES
    qi = qi_ref[0].astype(BF16)
    wi = wi_ref[0]

    def scores(s):
        s = jnp.maximum(s, 0.0) * wi
        return jnp.sum(s.reshape(t, IDX_HEADS, s.shape[1]), axis=1)

    ki_t = jnp.concatenate([p[0, 0] for p in pages], axis=1).astype(BF16)
    sc_chunk = scores(jnp.dot(qi, ki_t, preferred_element_type=F32))
    for jj in range(past // cw):
        @pl.when(j == jj)
        def _(jj=jj):
            sc_ref[0, :, jj * cw:(jj + 1) * cw] = sc_chunk

    @pl.when(j == nsteps - 1)
    def _():
        kin_s[...] = jnp.zeros(kin_s.shape, F32)
        kin_s[0:t, :] = kinew_ref[0]
        snew = scores(_dot_nt(qi, kin_s[...].astype(BF16)))
        cn = lax.broadcasted_iota(I32, (t, LANES), 1)
        rn = lax.broadcasted_iota(I32, (t, LANES), 0)
        sc_ref[0, :, past:past + LANES] = jnp.where(cn <= rn, snew, -jnp.inf)


def _dsa_select_kernel(sc_ref, thr_ref, jcut_ref, *, topk):
    rows, ncols = sc_ref.shape
    kf = float(topk)

    def count(pred):
        acc = jnp.zeros((rows, LANES), F32)
        for c0 in range(0, ncols, LANES):
            acc = acc + pred(sc_ref[:, c0:c0 + LANES], c0).astype(F32)
        return _row_total(acc)

    t0 = jnp.where(count(lambda s, c0: s >= 0.0) >= kf, 0, INT_MIN).astype(I32)

    def bit_step(b, t):
        cand = t + lax.shift_left(jnp.int32(1), 30 - b)
        cand_f = _key_to_float(cand)
        return jnp.where(count(lambda s, c0: s >= cand_f) >= kf, cand, t)

    thr_key = lax.fori_loop(0, 31, bit_step, t0)
    thr = _key_to_float(thr_key)
    tie = (count(lambda s, c0: s >= thr) > kf) & (thr_key > KEY_NEG_INF)
    nbits = max(1, (ncols - 1).bit_length())

    def break_ties():
        need = kf - count(lambda s, c0: s > thr)
        lane = lax.broadcasted_iota(I32, (rows, LANES), 1)

        def idx_step(b, jc):
            cand = jc + lax.shift_left(jnp.int32(1), nbits - 1 - b)
            n_lt = count(lambda s, c0: (s == thr) & (c0 + lane < cand))
            return jnp.where(n_lt < need, cand, jc)

        jc = lax.fori_loop(0, nbits, idx_step, jnp.zeros((rows, LANES), I32))
        return jnp.where(tie, jc, ncols)

    thr_ref[...] = thr
    jcut_ref[...] = lax.cond(jnp.max(tie.astype(F32)) > 0.0, break_ties,
                             lambda: jnp.full((rows, LANES), ncols, I32))


def _dsa_sample_attn_kernel(pt_ref, q_ref, sc_ref, scnew_ref, thr_ref, jcut_ref, knew_ref, vnew_ref,
                            *rest, pp, t):
    kpages = rest[:pp]
    vpages = rest[pp:2 * pp]
    o_ref, m_s, l_s, acc_s, new_s = rest[2 * pp:]
    j = pl.program_id(1)
    nsteps = pl.num_programs(1)
    cw = pp * PAGE_SIZE

    def bias_of(sc, c0):
        n = sc.shape[1] // LANES
        thr, jcut = _lane_tile(thr_ref[0], n), _lane_tile(jcut_ref[0], n)
        col = c0 + lax.broadcasted_iota(I32, sc.shape, 1)
        sel = (sc > -jnp.inf) & ((sc > thr) | ((sc == thr) & (col <= jcut)))
        bias = jnp.where(sel, 0.0, MASK_BIAS).astype(F32)
        return jnp.concatenate([bias] * A_GROUP, axis=0)

    @pl.when(j == 0)
    def _():
        m_s[...] = jnp.full(m_s.shape, MASK_BIAS, F32)
        l_s[...] = jnp.zeros(l_s.shape, F32)
        acc_s[...] = jnp.zeros(acc_s.shape, F32)

    def update(kh, kb, vb, bias, key_major):
        qh = (q_ref[0, kh] * (A_HEAD_DIM ** -0.5)).astype(BF16)
        qk = _dot_nt(qh, kb) if key_major else jnp.dot(qh, kb, preferred_element_type=F32)
        logits = qk + bias
        m_old = m_s[kh]
        m_new = jnp.maximum(m_old, jnp.max(logits, axis=-1, keepdims=True))
        alpha = jnp.exp(m_old - m_new)
        e = jnp.exp(logits - m_new)
        p = e.astype(BF16)
        pv = jnp.dot(p, vb, preferred_element_type=F32) if key_major else _dot_nt(p, vb)
        l_s[kh] = alpha * l_s[kh] + jnp.sum(e, axis=-1, keepdims=True)
        acc_s[kh] = alpha * acc_s[kh] + pv
        m_s[kh] = m_new

    bias = bias_of(sc_ref[0], j * cw)
    for kh in range(A_KV_HEADS):
        k_t = jnp.concatenate([p[0, 0, kh] for p in kpages], axis=1).astype(BF16)
        v_t = jnp.concatenate([p[0, 0, kh] for p in vpages], axis=1).astype(BF16)
        update(kh, k_t, v_t, bias, False)

    @pl.when(j == nsteps - 1)
    def _():
        bias_new = bias_of(scnew_ref[0], nsteps * cw)
        for kh in range(A_KV_HEADS):
            sl = slice(kh * A_HEAD_DIM, (kh + 1) * A_HEAD_DIM)
            new_s[...] = jnp.zeros(new_s.shape, F32)
            new_s[0, 0:t, :] = knew_ref[0, :, sl]
            new_s[1, 0:t, :] = vnew_ref[0, :, sl]
            update(kh, new_s[0].astype(BF16), new_s[1].astype(BF16), bias_new, True)
            o_ref[0, kh] = acc_s[kh] / l_s[kh]


def dsa_sample(q_g, qi_t, wi_t, ki_new, k_new, v_new, cache_k, cache_v, cache_ki, page_flat, layer, pp):
    db, _, gt, hd = q_g.shape
    t = gt // A_GROUP
    npg = page_flat.shape[0] // db
    assert npg % pp == 0
    nsteps = npg // pp
    past = npg * PAGE_SIZE
    ncols = past + LANES
    cw = pp * PAGE_SIZE
    kvw = k_new.shape[2]

    def page_spec(shape, i):
        nz = (0,) * len(shape)
        return pl.BlockSpec((1, 1) + shape, lambda b, j, pt, i=i: (layer, pt[b * npg + j * pp + i]) + nz)

    ki_page = (IDX_DIM, PAGE_SIZE)
    kv_page = (A_KV_HEADS, hd, PAGE_SIZE)

    sc = pl.pallas_call(
        functools.partial(_dsa_sample_score_kernel, pp=pp, t=t),
        grid_spec=pltpu.PrefetchScalarGridSpec(
            num_scalar_prefetch=1,
            grid=(db, nsteps),
            in_specs=[pl.BlockSpec((1, t * IDX_HEADS, IDX_DIM), lambda b, j, pt: (b, 0, 0)),
                      pl.BlockSpec((1, t * IDX_HEADS, 1), lambda b, j, pt: (b, 0, 0)),
                      pl.BlockSpec((1, t, IDX_DIM), lambda b, j, pt: (b, 0, 0))]
                     + [page_spec(ki_page, i) for i in range(pp)],
            out_specs=pl.BlockSpec((1, t, ncols), lambda b, j, pt: (b, 0, 0)),
            scratch_shapes=[pltpu.VMEM((LANES, IDX_DIM), F32)]),
        out_shape=jax.ShapeDtypeStruct((db, t, ncols), F32),
        compiler_params=_cparams("parallel", "arbitrary"),
        name="dsa_sample_score",
    )(page_flat, qi_t, wi_t, ki_new, *([cache_ki] * pp))

    thr, jcut = pl.pallas_call(
        functools.partial(_dsa_select_kernel, topk=min(TOPK_MAX, (past + t) // 4)),
        out_shape=[jax.ShapeDtypeStruct((db * t, LANES), F32), jax.ShapeDtypeStruct((db * t, LANES), I32)],
        compiler_params=pltpu.CompilerParams(vmem_limit_bytes=VMEM_LIMIT_BYTES),
        name="dsa_sample_select",
    )(sc.reshape(db * t, ncols))
    thr, jcut = thr.reshape(db, t, LANES), jcut.reshape(db, t, LANES)

    out = pl.pallas_call(
        functools.partial(_dsa_sample_attn_kernel, pp=pp, t=t),
        grid_spec=pltpu.PrefetchScalarGridSpec(
            num_scalar_prefetch=1,
            grid=(db, nsteps),
            in_specs=[pl.BlockSpec((1, A_KV_HEADS, gt, hd), lambda b, j, pt: (b, 0, 0, 0)),
                      pl.BlockSpec((1, t, cw), lambda b, j, pt: (b, 0, j)),
                      pl.BlockSpec((1, t, LANES), lambda b, j, pt: (b, 0, past // LANES)),
                      pl.BlockSpec((1, t, LANES), lambda b, j, pt: (b, 0, 0)),
                      pl.BlockSpec((1, t, LANES), lambda b, j, pt: (b, 0, 0)),
                      pl.BlockSpec((1, t, kvw), lambda b, j, pt: (b, 0, 0)),
                      pl.BlockSpec((1, t, kvw), lambda b, j, pt: (b, 0, 0))]
                     + [page_spec(kv_page, i) for i in range(pp)]
                     + [page_spec(kv_page, i) for i in range(pp)],
            out_specs=pl.BlockSpec((1, A_KV_HEADS, gt, hd), lambda b, j, pt: (b, 0, 0, 0)),
            scratch_shapes=[pltpu.VMEM((A_KV_HEADS, gt, 1), F32),
                            pltpu.VMEM((A_KV_HEADS, gt, 1), F32),
                            pltpu.VMEM((A_KV_HEADS, gt, hd), F32),
                            pltpu.VMEM((2, LANES, hd), F32)]),
        out_shape=jax.ShapeDtypeStruct((db, A_KV_HEADS, gt, hd), F32),
        compiler_params=_cparams("parallel", "arbitrary"),
        name="dsa_sample_attn",
    )(page_flat, q_g, sc, sc, thr, jcut, k_new, v_new, *([cache_k] * pp), *([cache_v] * pp))
    return out


def _gla_kernel(*refs, tb, tv, chunk, sub, has_s0):
    if has_s0:
        (q_ref, k_ref, v_ref, g_ref, gd_ref, wgu_ref, bg_ref, gn_ref, s0_ref,
         on_ref, sout_ref, b_s, dec_s, st_s) = refs
    else:
        q_ref, k_ref, v_ref, g_ref, gd_ref, wgu_ref, bg_ref, gn_ref, on_ref, sout_ref, b_s, dec_s, st_s = refs
    jb = pl.program_id(1)
    nb = pl.num_programs(1)
    nfull, rem = tb // chunk, tb % chunk
    spans = [(ic * chunk, chunk) for ic in range(nfull)] + ([(nfull * chunk, rem)] if rem else [])

    @pl.when(jb == 0)
    def _():
        for h in range(GLA_HEADS):
            st_s[h] = s0_ref[0, 0, h] if has_s0 else jnp.zeros((GLA_DK, GLA_DV), F32)

    x = jnp.dot(gd_ref[0], wgu_ref[...], precision=HIGHEST, preferred_element_type=F32) + bg_ref[...]
    la = (jnp.minimum(x, 0.0) - jnp.log1p(jnp.exp(-jnp.abs(x)))) * (1.0 / GLA_GATE_NORM)
    if tv is not None:
        row = jb * tb + lax.broadcasted_iota(I32, (tb, 1), 0)
        la = jnp.where(row < tv, la, 0.0)
    for ic, (r0, c) in enumerate(spans):
        lac = la[r0:r0 + c]
        tri = (lax.broadcasted_iota(I32, (c, c), 0) >= lax.broadcasted_iota(I32, (c, c), 1)).astype(BF16)
        hi = lac.astype(BF16)
        r1 = lac - hi.astype(F32)
        mid = r1.astype(BF16)
        lo = (r1 - mid.astype(F32)).astype(BF16)
        bc = (jnp.dot(tri, hi, preferred_element_type=F32) + jnp.dot(tri, mid, preferred_element_type=F32)
              + jnp.dot(tri, lo, preferred_element_type=F32))
        b_s[r0:r0 + c, :] = bc
        dec_s[ic] = jnp.exp(jnp.broadcast_to(bc[c - 1:c, :], (LANES, bc.shape[1])).T)

    def do_chunk(ic, r0, c):
        rows = pl.ds(r0, c)
        b = b_s[rows, :]
        qc = q_ref[0, rows, :] * (GLA_DK ** -0.5)
        kc = k_ref[0, rows, :]
        if tv is not None:
            rowc = jb * tb + r0 + lax.broadcasted_iota(I32, (c, 1), 0)
            kc = jnp.where(rowc < tv, kc, 0.0)
        vc = v_ref[0, rows, :].astype(BF16)
        gc = g_ref[0, rows, :]
        dec = dec_s[ic]
        causal = lax.broadcasted_iota(I32, (c, LANES), 1) <= lax.broadcasted_iota(I32, (c, LANES), 0)
        heads, states = [], []
        for h in range(GLA_HEADS):
            sk = slice(h * GLA_DK, (h + 1) * GLA_DK)
            sv = slice(h * GLA_DV, (h + 1) * GLA_DV)
            bh, qh, kh, vh = b[:, sk], qc[:, sk], kc[:, sk], vc[:, sv]
            st = st_s[h]
            att = []
            for i in range(c // sub):
                lo, hi = i * sub, (i + 1) * sub
                ref_b = bh[lo - 1:lo, :] if i > 0 else jnp.zeros((1, GLA_DK), F32)
                qt = (qh[lo:hi] * jnp.exp(bh[lo:hi] - ref_b)).astype(BF16)
                kt = (kh[0:hi] * jnp.exp(ref_b - bh[0:hi])).astype(BF16)
                kt = jnp.concatenate([kt, jnp.zeros((LANES - hi, GLA_DK), BF16)], axis=0)
                att.append(_dot_nt(qt, kt))
            att = jnp.concatenate(att, axis=0) if len(att) > 1 else att[0]
            att = jnp.where(causal, att, 0.0).astype(BF16)
            lhs = jnp.concatenate([(qh * jnp.exp(bh)).astype(BF16), att], axis=1)
            vpad = vh if c == LANES else jnp.concatenate([vh, jnp.zeros((LANES - c, GLA_DV), BF16)], axis=0)
            rhs = jnp.concatenate([st.astype(BF16), vpad], axis=0)
            o = jnp.dot(lhs, rhs, preferred_element_type=F32)
            bl = bh[c - 1:c, :]
            kd = (kh * jnp.exp(bl - bh)).astype(BF16)
            dh = dec[sk, :]
            states.append(jnp.concatenate([dh, dh], axis=1) * st + _dot_tn(kd, vh))
            gh = gc[:, sv]
            heads.append(_rms(o, gn_ref[...]) * (gh * jax.nn.sigmoid(gh)))
        on_ref[0, rows, :] = jnp.concatenate(heads, axis=1).astype(on_ref.dtype)
        for h in range(GLA_HEADS):
            st_s[h] = states[h]

    if nfull == 1:
        do_chunk(0, 0, chunk)
    elif nfull > 1:
        def body(ic, carry):
            do_chunk(ic, pl.multiple_of(ic * chunk, BF16_ROWS), chunk)
            return carry
        lax.fori_loop(0, nfull, body, 0)
    if rem:
        do_chunk(nfull, nfull * chunk, rem)

    @pl.when(jb == nb - 1)
    def _():
        for h in range(GLA_HEADS):
            sout_ref[0, h] = st_s[h]


def gla_mixer(q, k, v, g, gd, wgu, bg, gn, s0, s0_layer, tb, chunk, tv):
    b, t, _ = q.shape
    sub = BF16_ROWS
    assert t % tb == 0 and tb % sub == 0 and chunk % sub == 0 and chunk <= LANES and GLA_DV == 2 * LANES
    nspans = -(-tb // chunk)
    blk = lambda c: pl.BlockSpec((1, tb, c), lambda i, j: (i, j, 0))
    cst = lambda a: pl.BlockSpec(a.shape, lambda i, j: (0,) * a.ndim)
    st_spec = pl.BlockSpec((1, GLA_HEADS, GLA_DK, GLA_DV), lambda i, j: (i, 0, 0, 0))
    bg2, gn2 = bg.reshape(1, -1), gn.reshape(1, -1)
    args = [q, k, v, g, gd, wgu, bg2, gn2]
    in_specs = [blk(q.shape[2]), blk(k.shape[2]), blk(v.shape[2]), blk(g.shape[2]), blk(gd.shape[2]),
                cst(wgu), cst(bg2), cst(gn2)]
    if s0 is not None:
        args.append(s0)
        in_specs.append(pl.BlockSpec((1, 1, GLA_HEADS, GLA_DK, GLA_DV), lambda i, j: (s0_layer, i, 0, 0, 0)))
    return pl.pallas_call(
        functools.partial(_gla_kernel, tb=tb, tv=tv, chunk=chunk, sub=sub, has_s0=s0 is not None),
        grid=(b, t // tb),
        in_specs=in_specs,
        out_specs=[blk(v.shape[2]), st_spec],
        out_shape=[jax.ShapeDtypeStruct(v.shape, BF16),
                   jax.ShapeDtypeStruct((b, GLA_HEADS, GLA_DK, GLA_DV), F32)],
        scratch_shapes=[pltpu.VMEM((tb, GLA_HEADS * GLA_DK), F32),
                        pltpu.VMEM((nspans, GLA_HEADS * GLA_DK, LANES), F32),
                        pltpu.VMEM((GLA_HEADS, GLA_DK, GLA_DV), F32)],
        compiler_params=_cparams("parallel", "arbitrary"),
        name="gla_mixer",
    )(*args)


def _pick_tile(n, candidates):
    for c in candidates:
        if n % c == 0:
            return c
    return n


def kernel(x_prompt, x_sample, cache_k, cache_v, cache_kidx, state_pool, state_gla, page_table,
           meta_tokens, norm_mix, norm_ffn, norm_final, w_in_even, w_out_even, pool_w, pool_scale,
           w_in_odd, gla_w_gate_up, gla_b_gate, gla_norm, w_out_odd, ffn_w_gate, ffn_w_up, ffn_w_down):
    bsz, seq, d = x_prompt.shape
    db, dseq, _ = x_sample.shape
    depth = norm_mix.shape[0]
    length = seq + N_META
    npg = page_table.shape[1]
    past = npg * PAGE_SIZE
    a_width = A_HEADS * A_HEAD_DIM
    kv_width = A_KV_HEADS * A_HEAD_DIM
    pool_width = pool_scale.shape[1]
    gk = GLA_HEADS * GLA_DK
    gv = GLA_HEADS * GLA_DV
    d_ff = ffn_w_gate.shape[2]

    meta = jnp.broadcast_to(meta_tokens[None], (bsz, N_META, d))
    xp = jnp.concatenate([meta, x_prompt], axis=1).reshape(bsz * length, d)
    xs = x_sample.reshape(db * dseq, d)
    n_p, n_s = xp.shape[0], xs.shape[0]

    tm_p = _pick_tile(n_p, (688, 512, 256, 128, 64, 32, 16, 8))
    tm_s = _pick_tile(n_s, (128, 64, 32, 16, 8))
    tf = _pick_tile(d_ff, (1408, 1024, 512, 256, 128))
    tq = _pick_tile(length, (344, 256, 128, 64, 32, 16, 8))
    pool_rc = _pick_tile(length, (48, 32, 16, 8))
    gla_tb = _pick_tile(length, (688, 512, 256, 128, 64, 32, 16))
    pp = _pick_tile(npg, (64, 32, 16, 8, 4, 2, 1))

    e_off = [0]
    for s in (a_width, kv_width, kv_width, IDX_HEADS * IDX_DIM, IDX_DIM, IDX_HEADS, pool_width):
        e_off.append(e_off[-1] + s)
    kiwi_pad = LANES - IDX_DIM - IDX_HEADS
    even_widths = (a_width, kv_width, kv_width, IDX_HEADS * IDX_DIM, pool_width, LANES)
    odd_widths = (gk, gk, gv, gv, LANES)

    cache_kt = cache_k.transpose(0, 1, 3, 4, 2)
    cache_vt = cache_v.transpose(0, 1, 3, 4, 2)
    cache_kit = cache_kidx.transpose(0, 1, 3, 2)
    page_flat = page_table.reshape(-1).astype(I32)

    kp_l, vp_l, kip_l, bp_l, ks_l, vs_l, kis_l, bs_l, sp_l, ss_l = ([] for _ in range(10))
    for l in range(depth):
        if l % 2 == 0:
            e = l // 2
            w = w_in_even[e]
            w_in = jnp.concatenate(
                [w[:, e_off[0]:e_off[4]], w[:, e_off[6]:e_off[7]], w[:, e_off[4]:e_off[6]],
                 jnp.zeros((d, kiwi_pad), w.dtype)], axis=1).astype(BF16)
            w_out = w_out_even[e].astype(BF16)
            pw = pool_w[e].astype(BF16)

            q, k, v, qi, u, kiwi = norm_matmul(xp, norm_mix[l], w_in, even_widths, tm_p)
            r3 = lambda a: a.reshape(bsz, length, a.shape[1])
            oa = dsa_prompt(r3(q), r3(qi), r3(k), r3(v), r3(kiwi), tq, DSA_KEY_CHUNK)
            ob, buf = pool_mixer(r3(u), jnp.zeros((bsz, POOL_BUF + 1, pool_width), F32), pw,
                                 pool_scale[e], 0, pool_rc, BF16)
            mix_p = [(oa.reshape(n_p, a_width), w_out[:a_width]),
                     (ob.reshape(n_p, pool_width), w_out[a_width:])]
            kp_l.append(k.reshape(bsz, length, A_KV_HEADS, A_HEAD_DIM))
            vp_l.append(v.reshape(bsz, length, A_KV_HEADS, A_HEAD_DIM))
            kip_l.append(kiwi[:, :IDX_DIM].reshape(bsz, length, IDX_DIM))
            bp_l.append(buf)

            q, k, v, qi, u, kiwi = norm_matmul(xs, norm_mix[l], w_in, even_widths, tm_s)
            s3 = lambda a: a.reshape(db, dseq, a.shape[1])
            q_g = q.reshape(db, dseq, A_KV_HEADS, A_GROUP, A_HEAD_DIM).transpose(0, 2, 3, 1, 4)
            q_g = q_g.reshape(db, A_KV_HEADS, A_GROUP * dseq, A_HEAD_DIM)
            qi_t = qi.reshape(db, dseq * IDX_HEADS, IDX_DIM)
            wi_t = kiwi[:, IDX_DIM:IDX_DIM + IDX_HEADS].reshape(db, dseq * IDX_HEADS, 1)
            ki_new = kiwi[:, :IDX_DIM].reshape(db, dseq, IDX_DIM)
            oa = dsa_sample(q_g, qi_t, wi_t, ki_new, s3(k), s3(v), cache_kt, cache_vt, cache_kit,
                            page_flat, e, pp)
            oa = oa.reshape(db, A_KV_HEADS, A_GROUP, dseq, A_HEAD_DIM).transpose(0, 3, 1, 2, 4)
            prefix = jnp.pad(state_pool[e], ((0, 0), (1, 0), (0, 0)))
            ob, buf = pool_mixer(s3(u), prefix, pw, pool_scale[e], past, BF16_ROWS, F32)
            mix_s = [(oa.reshape(n_s, a_width), w_out[:a_width]),
                     (ob.reshape(n_s, pool_width), w_out[a_width:])]
            ks_l.append(k.reshape(db, dseq, A_KV_HEADS, A_HEAD_DIM))
            vs_l.append(v.reshape(db, dseq, A_KV_HEADS, A_HEAD_DIM))
            kis_l.append(ki_new)
            bs_l.append(buf)
        else:
            o = l // 2
            w = w_in_odd[o]
            w_in = jnp.concatenate([w, jnp.zeros((d, LANES - GLA_GATE_RANK), w.dtype)], axis=1).astype(BF16)
            w_out = w_out_odd[o].astype(BF16)
            wgu = jnp.pad(gla_w_gate_up[o], ((0, LANES - GLA_GATE_RANK), (0, 0)))

            odd_dtypes = (F32, F32, BF16, F32, F32)
            q, k, v, g, gd = norm_matmul(xp, norm_mix[l], w_in, odd_widths, tm_p, odd_dtypes)
            r3 = lambda a: a.reshape(bsz, length, a.shape[1])
            on, s_p = gla_mixer(r3(q), r3(k), r3(v), r3(g), r3(gd), wgu, gla_b_gate[o], gla_norm[o],
                                None, None, gla_tb, GLA_CHUNK, None)
            mix_p = [(on.reshape(n_p, gv), w_out)]
            sp_l.append(s_p)

            q, k, v, g, gd = norm_matmul(xs, norm_mix[l], w_in, odd_widths, tm_s, odd_dtypes)
            tpad = _round_up(dseq, BF16_ROWS)
            p3 = lambda a: jnp.pad(a.reshape(db, dseq, a.shape[1]), ((0, 0), (0, tpad - dseq), (0, 0)))
            on, s_s = gla_mixer(p3(q), p3(k), p3(v), p3(g), p3(gd), wgu, gla_b_gate[o], gla_norm[o],
                                state_gla, o, tpad, tpad, dseq)
            mix_s = [(on[:, :dseq].reshape(n_s, gv), w_out)]
            ss_l.append(s_s)

        wg, wu, wd = ffn_w_gate[l].astype(BF16), ffn_w_up[l].astype(BF16), ffn_w_down[l].astype(BF16)
        xp = mix_ffn(xp, mix_p, norm_ffn[l], wg, wu, wd, tm_p, tf)
        xs = mix_ffn(xs, mix_s, norm_ffn[l], wg, wu, wd, tm_s, tf)

    y_prompt = final_norm(xp.reshape(bsz, length, d), norm_final, N_META)
    y_sample = final_norm(xs.reshape(1, n_s, d), norm_final, 0).reshape(db, dseq, d)
    return (y_prompt, y_sample,
            jnp.stack(kp_l), jnp.stack(vp_l), jnp.stack(kip_l),
            jnp.stack(ks_l), jnp.stack(vs_l), jnp.stack(kis_l),
            jnp.stack(bp_l), jnp.stack(bs_l),
            jnp.stack(sp_l), jnp.stack(ss_l))
```
